```python
import numpy as np
import jax
import jax.numpy as jnp
from jax import lax

D_MODEL = 1024
BATCH = 8
SEQ = 4096
DEPTH = 2

SWA_Q_HEADS = 8
SWA_KV_HEADS = 2
SWA_HEAD_DIM = 64
WINDOW = 128
ROT_DIM = SWA_HEAD_DIM // 4
ROPE_THETA = 500000.0

GDN_HEADS = 4
GDN_HEAD_DIM = 128
CONV_K = 4
GDN_CHUNK = 64

RET_HEADS = 4
RET_QK_DIM = 256
RET_V_DIM = 512
RET_CHUNK = 128
XPOS_THETA = 10000.0

N_GROUPS = 4
EXPERTS_PER_GROUP = 8
N_EXPERTS = N_GROUPS * EXPERTS_PER_GROUP
TOP_K = 2
D_EXPERT = 512
MOE_BLOCK = 128

EPS = 1e-6

SWA_Q_W = SWA_Q_HEADS * SWA_HEAD_DIM
SWA_KV_W = SWA_KV_HEADS * SWA_HEAD_DIM
GDN_W = GDN_HEADS * GDN_HEAD_DIM
EVEN_SPLITS = (SWA_Q_W, SWA_KV_W, SWA_KV_W, 3 * GDN_W, GDN_W, GDN_HEADS, GDN_HEADS)
EVEN_IN = SWA_Q_W + 2 * SWA_KV_W + 4 * GDN_W + 2 * GDN_HEADS
EVEN_MIX = SWA_Q_W + GDN_W
RET_QK_W = RET_HEADS * RET_QK_DIM
RET_V_W = RET_HEADS * RET_V_DIM
ODD_SPLITS = (RET_QK_W, RET_QK_W, RET_V_W, RET_V_W)
ODD_IN = 2 * RET_QK_W + 2 * RET_V_W
ODD_MIX = RET_V_W

kernel_name = 'hybrid_swa_gdn_retention_hmoe'


def _split(t, sizes):
    cuts = np.cumsum(np.asarray(sizes))[:-1].tolist()
    return jnp.split(t, cuts, axis=-1)


def rms_norm(x, gain=None):
    xf = x.astype(jnp.float32)
    y = xf * lax.rsqrt(jnp.mean(jnp.square(xf), axis=-1, keepdims=True) + EPS)
    if gain is not None:
        y = y * gain.astype(jnp.float32)
    return y.astype(x.dtype)


def l2_norm(x):
    xf = x.astype(jnp.float32)
    return xf * lax.rsqrt(jnp.sum(xf * xf, axis=-1, keepdims=True) + EPS)


def partial_rope(x, pos):
    half = ROT_DIM // 2
    inv_freq = 1.0 / (ROPE_THETA ** (jnp.arange(half, dtype=jnp.float32) * 2.0 / ROT_DIM))
    ang = pos[:, None] * inv_freq[None, :]
    cos = jnp.cos(ang)[None, :, None, :].astype(x.dtype)
    sin = jnp.sin(ang)[None, :, None, :].astype(x.dtype)
    x1 = x[..., :half]
    x2 = x[..., half:ROT_DIM]
    return jnp.concatenate([x1 * cos - x2 * sin, x2 * cos + x1 * sin, x[..., ROT_DIM:]], axis=-1)


def xpos_rotate(x, pos):
    half = x.shape[-1] // 2
    freq = 1.0 / (XPOS_THETA ** jnp.linspace(0.0, 1.0, half, dtype=jnp.float32))
    ang = pos[:, None] * freq[None, :]
    cos = jnp.cos(ang)[None, :, None, :].astype(x.dtype)
    sin = jnp.sin(ang)[None, :, None, :].astype(x.dtype)
    x1 = x[..., 0::2]
    x2 = x[..., 1::2]
    return jnp.stack([x1 * cos - x2 * sin, x2 * cos + x1 * sin], axis=-1).reshape(x.shape)


def causal_depthwise_conv(x, w):
    k_len, ch = w.shape
    return lax.conv_general_dilated(x, w[:, None, :], window_strides=(1,), padding=[(k_len - 1, 0)],
                                    dimension_numbers=('NWC', 'WIO', 'NWC'), feature_group_count=ch)


def sliding_window_attention(q, k, v, sinks):
    bsz, s, hq, dh = q.shape
    hkv = k.shape[2]
    grp = hq // hkv
    n = s // WINDOW
    qb = q.reshape(bsz, n, WINDOW, hkv, grp, dh)

    def band(t):
        t = t.reshape(bsz, n, WINDOW, hkv, dh)
        prev = jnp.pad(t, ((0, 0), (1, 0), (0, 0), (0, 0), (0, 0)))[:, :-1]
        return jnp.concatenate([prev, t], axis=2)

    kb, vb = band(k), band(v)
    scores = jnp.einsum('bnqhgd,bnkhd->bhgnqk', qb, kb).astype(jnp.float32) * (dh ** -0.5)
    blk = jnp.arange(n)[:, None] * WINDOW
    qpos = blk + jnp.arange(WINDOW)[None, :]
    kpos = blk - WINDOW + jnp.arange(2 * WINDOW)[None, :]
    rel = qpos[:, :, None] - kpos[:, None, :]
    mask = (rel >= 0) & (rel < WINDOW) & (kpos[:, None, :] >= 0)
    scores = jnp.where(mask, scores, -jnp.inf)
    sink = jnp.broadcast_to(sinks.astype(jnp.float32).reshape(1, hkv, grp, 1, 1, 1), scores.shape[:-1] + (1,))
    probs = jax.nn.softmax(jnp.concatenate([scores, sink], axis=-1), axis=-1)[..., :-1]
    out = jnp.einsum('bhgnqk,bnkhd->bnqhgd', probs.astype(v.dtype), vb)
    return out.reshape(bsz, s, hq * dh)


def gated_delta_rule(q, k, v, beta, g):
    bsz, s, h, dk = q.shape
    dv = v.shape[-1]
    c = GDN_CHUNK
    n = s // c

    def to_chunks(t):
        t = t.reshape((bsz, n, c, h) + t.shape[3:])
        return jnp.moveaxis(t, 3, 1)

    q, k, v, beta, g = map(to_chunks, (q, k, v, beta, g))
    g = jnp.cumsum(g, axis=-1)
    incl = jnp.tril(jnp.ones((c, c), dtype=bool))
    strict = jnp.tril(jnp.ones((c, c), dtype=bool), -1)
    decay = jnp.exp(jnp.where(incl, g[..., :, None] - g[..., None, :], -jnp.inf))
    kb = k * beta[..., None]
    lower = jnp.where(strict, jnp.einsum('bhnid,bhnjd->bhnij', kb, k) * decay, 0.0)
    eye = jnp.eye(c, dtype=jnp.float32)
    t_inv = lax.linalg.triangular_solve(eye + lower, jnp.broadcast_to(eye, lower.shape),
                                        left_side=True, lower=True)
    u = t_inv @ (v * beta[..., None])
    w = t_inv @ (kb * jnp.exp(g)[..., None])
    intra = jnp.einsum('bhnid,bhnjd->bhnij', q, k) * decay
    q_dec = q * jnp.exp(g)[..., None]
    g_last = g[..., -1]
    k_dec = k * jnp.exp(g_last[..., None] - g)[..., None]
    xs = tuple(jnp.moveaxis(t, 2, 0) for t in (u, w, intra, q_dec, k_dec, jnp.exp(g_last)))

    def step(state, inp):
        u_c, w_c, a_c, qd_c, kd_c, gl_c = inp
        v_new = u_c - jnp.einsum('bhck,bhkv->bhcv', w_c, state)
        out = jnp.einsum('bhck,bhkv->bhcv', qd_c, state) + jnp.einsum('bhij,bhjv->bhiv', a_c, v_new)
        state = state * gl_c[..., None, None] + jnp.einsum('bhck,bhcv->bhkv', kd_c, v_new)
        return state, out

    state0 = jnp.zeros((bsz, h, dk, dv), jnp.float32)
    _, out = lax.scan(step, state0, xs)
    return jnp.transpose(out, (1, 0, 3, 2, 4)).reshape(bsz, s, h, dv)


def multiscale_retention(q, k, v):
    bsz, s, h, dk = q.shape
    dv = v.shape[-1]
    c = RET_CHUNK
    n = s // c
    log_gamma = jnp.log(1.0 - 2.0 ** (-5.0 - jnp.arange(h, dtype=jnp.float32)))
    idx = jnp.arange(c, dtype=jnp.float32)
    rel = idx[:, None] - idx[None, :]
    decay = jnp.exp(jnp.where(rel >= 0, rel * log_gamma[:, None, None], -jnp.inf))
    xi = jnp.exp((idx + 1.0) * log_gamma[:, None])[..., None]
    zeta = jnp.exp((c - 1.0 - idx) * log_gamma[:, None])[..., None]
    gamma_c = jnp.exp(c * log_gamma)[:, None, None]

    def to_chunks(t):
        return jnp.transpose(t.reshape(bsz, n, c, h, t.shape[-1]), (1, 0, 3, 2, 4))

    def step(state, inp):
        q_c, k_c, v_c = inp
        inner = jnp.einsum('bhik,bhjk->bhij', q_c, k_c) * decay
        out = jnp.einsum('bhij,bhjv->bhiv', inner, v_c) + jnp.einsum('bhik,bhkv->bhiv', q_c, state) * xi
        state = state * gamma_c + jnp.einsum('bhjk,bhjv->bhkv', k_c * zeta, v_c)
        return state, out

    state0 = jnp.zeros((bsz, h, dk, dv), jnp.float32)
    _, out = lax.scan(step, state0, (to_chunks(q), to_chunks(k), to_chunks(v)))
    return jnp.transpose(out, (1, 0, 3, 2, 4)).reshape(bsz, s, h, dv)


def attn_delta_mixer(h, pos, w_in, conv_w, a_log, dt_bias, q_gain, k_gain, sinks, gdn_gain, w_out):
    bsz, s, _ = h.shape
    aq, ak, av, bqkv, bz, ba, bb = _split(h @ w_in, EVEN_SPLITS)
    q = rms_norm(aq.reshape(bsz, s, SWA_Q_HEADS, SWA_HEAD_DIM), q_gain)
    k = rms_norm(ak.reshape(bsz, s, SWA_KV_HEADS, SWA_HEAD_DIM), k_gain)
    v = av.reshape(bsz, s, SWA_KV_HEADS, SWA_HEAD_DIM)
    y_a = sliding_window_attention(partial_rope(q, pos), partial_rope(k, pos), v, sinks)
    qkv = jax.nn.silu(causal_depthwise_conv(bqkv, conv_w))
    bq, bk, bv = jnp.split(qkv, 3, axis=-1)
    head_shape = (bsz, s, GDN_HEADS, GDN_HEAD_DIM)
    bq = l2_norm(bq.reshape(head_shape)) * (GDN_HEAD_DIM ** -0.5)
    bk = l2_norm(bk.reshape(head_shape))
    bv = bv.reshape(head_shape).astype(jnp.float32)
    beta = jax.nn.sigmoid(bb.astype(jnp.float32))
    g = -jnp.exp(a_log.astype(jnp.float32)) * jax.nn.softplus(ba.astype(jnp.float32) + dt_bias.astype(jnp.float32))
    o = gated_delta_rule(bq, bk, bv, beta, g)
    o = rms_norm(o, gdn_gain).astype(h.dtype) * jax.nn.silu(bz.reshape(head_shape))
    y_b = o.reshape(bsz, s, GDN_W)
    return jnp.concatenate([y_a, y_b], axis=-1) @ w_out


def retention_mixer(h, pos, w_in, w_out):
    bsz, s, _ = h.shape
    q, k, v, z = _split(h @ w_in, ODD_SPLITS)
    qk_shape = (bsz, s, RET_HEADS, RET_QK_DIM)
    q = xpos_rotate(q.reshape(qk_shape), pos).astype(jnp.float32)
    k = xpos_rotate(k.reshape(qk_shape), pos).astype(jnp.float32) * (RET_QK_DIM ** -0.5)
    v = v.reshape(bsz, s, RET_HEADS, RET_V_DIM).astype(jnp.float32)
    o = rms_norm(multiscale_retention(q, k, v)).astype(h.dtype).reshape(bsz, s, ODD_MIX)
    return (jax.nn.silu(z) * o) @ w_out


def hierarchical_moe(x, w_rg, b_rg, w_re, b_re, w_gate, w_up, w_down):
    bsz, s, d = x.shape
    xt = x.reshape(-1, d)
    n_tok = xt.shape[0]
    n_assign = n_tok * TOP_K
    grp_prob = jax.nn.softmax((xt @ w_rg).astype(jnp.float32) + b_rg.astype(jnp.float32), axis=-1)
    grp_p, grp_idx = lax.top_k(grp_prob, 1)
    exp_logits = ((xt @ w_re).astype(jnp.float32) + b_re.astype(jnp.float32)).reshape(n_tok, N_GROUPS, EXPERTS_PER_GROUP)
    exp_logits = exp_logits[jnp.arange(n_tok), grp_idx[:, 0]]
    exp_p, exp_idx = lax.top_k(jax.nn.softmax(exp_logits, axis=-1), TOP_K)
    gates = grp_p * exp_p / jnp.sum(exp_p, axis=-1, keepdims=True)
    expert_id = (grp_idx * EXPERTS_PER_GROUP + exp_idx).reshape(-1)
    token_id = jnp.repeat(jnp.arange(n_tok), TOP_K)
    order = jnp.argsort(expert_id)
    e_sorted = expert_id[order]
    tok_sorted = token_id[order]
    gate_sorted = gates.reshape(-1)[order]
    counts = jnp.bincount(expert_id, length=N_EXPERTS)
    start = jnp.cumsum(counts) - counts
    padded = (counts + MOE_BLOCK - 1) // MOE_BLOCK * MOE_BLOCK
    padded_end = jnp.cumsum(padded)
    dest = padded_end[e_sorted] - padded[e_sorted] + jnp.arange(n_assign) - start[e_sorted]
    n_rows = n_assign + N_EXPERTS * MOE_BLOCK
    n_blocks = n_rows // MOE_BLOCK
    x_rows = jnp.zeros((n_rows, d), x.dtype).at[dest].set(xt[tok_sorted])
    block_expert = jnp.minimum(jnp.searchsorted(padded_end, jnp.arange(n_blocks) * MOE_BLOCK, side='right'), N_EXPERTS - 1)

    def expert_block(args):
        xb, e = args
        hid = jax.nn.silu(xb @ w_gate[e]) * (xb @ w_up[e])
        return hid @ w_down[e]

    y_rows = lax.map(expert_block, (x_rows.reshape(n_blocks, MOE_BLOCK, d), block_expert)).reshape(n_rows, d)
    y = jnp.zeros((n_tok, d), x.dtype).at[tok_sorted].add(y_rows[dest] * gate_sorted[:, None].astype(x.dtype))
    return y.reshape(bsz, s, d)


def setup_inputs(seed: int = 0) -> dict:
    key = jax.random.key(seed)
    ks = jax.random.split(key, 24)
    f32 = jnp.float32
    n_even = (DEPTH + 1) // 2
    n_odd = DEPTH // 2

    def normal(k, shape, scale):
        return jax.random.normal(k, shape, f32) * scale

    def gain(k, shape):
        return 1.0 + 0.02 * jax.random.normal(k, shape, f32)

    dt = jnp.exp(jax.random.uniform(ks[5], (n_even, GDN_HEADS), f32, np.log(1e-3), np.log(1e-1)))
    return {
        'x': normal(ks[0], (BATCH, SEQ, D_MODEL), 1.0),
        'ln_mix_even': gain(ks[1], (n_even, D_MODEL)),
        'w_in_even': normal(ks[2], (n_even, D_MODEL, EVEN_IN), D_MODEL ** -0.5),
        'conv_w_even': normal(ks[3], (n_even, CONV_K, 3 * GDN_W), CONV_K ** -0.5),
        'a_log_even': jnp.log(jax.random.uniform(ks[4], (n_even, GDN_HEADS), f32, 1.0, 16.0)),
        'dt_bias_even': dt + jnp.log(-jnp.expm1(-dt)),
        'q_gain_even': gain(ks[6], (n_even, SWA_HEAD_DIM)),
        'k_gain_even': gain(ks[7], (n_even, SWA_HEAD_DIM)),
        'sinks_even': normal(ks[8], (n_even, SWA_Q_HEADS), 1.0),
        'gdn_gain_even': gain(ks[9], (n_even, GDN_HEAD_DIM)),
        'w_out_even': normal(ks[10], (n_even, EVEN_MIX, D_MODEL), EVEN_MIX ** -0.5),
        'ln_mix_odd': gain(ks[11], (n_odd, D_MODEL)),
        'w_in_odd': normal(ks[12], (n_odd, D_MODEL, ODD_IN), D_MODEL ** -0.5),
        'w_out_odd': normal(ks[13], (n_odd, ODD_MIX, D_MODEL), ODD_MIX ** -0.5),
        'ln_ffn': gain(ks[14], (DEPTH, D_MODEL)),
        'w_router_group': normal(ks[15], (DEPTH, D_MODEL, N_GROUPS), D_MODEL ** -0.5),
        'b_router_group': normal(ks[16], (DEPTH, N_GROUPS), 0.01),
        'w_router_expert': normal(ks[17], (DEPTH, D_MODEL, N_EXPERTS), D_MODEL ** -0.5),
        'b_router_expert': normal(ks[18], (DEPTH, N_EXPERTS), 0.01),
        'w_gate': normal(ks[19], (DEPTH, N_EXPERTS, D_MODEL, D_EXPERT), D_MODEL ** -0.5),
        'w_up': normal(ks[20], (DEPTH, N_EXPERTS, D_MODEL, D_EXPERT), D_MODEL ** -0.5),
        'w_down': normal(ks[21], (DEPTH, N_EXPERTS, D_EXPERT, D_MODEL), D_EXPERT ** -0.5),
    }


def reference(x, ln_mix_even, w_in_even, conv_w_even, a_log_even, dt_bias_even, q_gain_even, k_gain_even,
              sinks_even, gdn_gain_even, w_out_even, ln_mix_odd, w_in_odd, w_out_odd, ln_ffn,
              w_router_group, b_router_group, w_router_expert, b_router_expert, w_gate, w_up, w_down):
    pos = jnp.arange(x.shape[1], dtype=jnp.float32)
    for layer in range(DEPTH):
        i = layer // 2
        if layer % 2 == 0:
            x = x + attn_delta_mixer(rms_norm(x, ln_mix_even[i]), pos, w_in_even[i], conv_w_even[i],
                                     a_log_even[i], dt_bias_even[i], q_gain_even[i], k_gain_even[i],
                                     sinks_even[i], gdn_gain_even[i], w_out_even[i])
        else:
            x = x + retention_mixer(rms_norm(x, ln_mix_odd[i]), pos, w_in_odd[i], w_out_odd[i])
        x = x + hierarchical_moe(rms_norm(x, ln_ffn[layer]), w_router_group[layer], b_router_group[layer],
                                 w_router_expert[layer], b_router_expert[layer], w_gate[layer],
                                 w_up[layer], w_down[layer])
    return x
```

```python
import functools

import numpy as np
import jax
import jax.numpy as jnp
from jax import lax
from jax.experimental import pallas as pl
from jax.experimental.pallas import tpu as pltpu

F32 = jnp.float32
BF16 = jnp.bfloat16

D_MODEL = 1024
SWA_Q_HEADS = 8
SWA_KV_HEADS = 2
SWA_HEAD_DIM = 64
WINDOW = 128
ROT_DIM = SWA_HEAD_DIM // 4
ROPE_THETA = 500000.0
GDN_HEADS = 4
GDN_HEAD_DIM = 128
CONV_K = 4
GDN_CHUNK = 64
RET_HEADS = 4
RET_QK_DIM = 256
RET_V_DIM = 512
RET_CHUNK = 128
XPOS_THETA = 10000.0
N_GROUPS = 4
EXPERTS_PER_GROUP = 8
N_EXPERTS = N_GROUPS * EXPERTS_PER_GROUP
TOP_K = 2
D_EXPERT = 512
EPS = 1e-6

SWA_Q_W = SWA_Q_HEADS * SWA_HEAD_DIM
SWA_KV_W = SWA_KV_HEADS * SWA_HEAD_DIM
GDN_W = GDN_HEADS * GDN_HEAD_DIM
EVEN_MAIN = SWA_Q_W + 2 * SWA_KV_W + 4 * GDN_W
RET_QK_W = RET_HEADS * RET_QK_DIM
RET_V_W = RET_HEADS * RET_V_DIM
ODD_IN = 2 * RET_QK_W + 2 * RET_V_W

LANES = 128
VMEM_LIMIT = 56 * 1024 * 1024

NEG_INF = float("-inf")


def _cparams(*sem):
    return pltpu.CompilerParams(dimension_semantics=sem, vmem_limit_bytes=VMEM_LIMIT)


def _silu(x):
    return x * (1.0 / (1.0 + jnp.exp(-x)))


def _norm_proj_kernel(x_ref, g_ref, w_ref, *rest, tn, has_gate):
    if has_gate:
        wg_ref, o_ref, og_ref = rest
    else:
        (o_ref,) = rest
    x = x_ref[...]
    h = x * lax.rsqrt(jnp.mean(x * x, axis=-1, keepdims=True) + EPS) * g_ref[...]
    hb = h.astype(BF16)
    n = o_ref.shape[-1]
    for lo in range(0, n, tn):
        hi = min(lo + tn, n)
        o_ref[:, lo:hi] = jnp.dot(hb, w_ref[:, lo:hi], preferred_element_type=F32).astype(o_ref.dtype)
    if has_gate:
        og_ref[...] = jnp.dot(h, wg_ref[...], preferred_element_type=F32, precision=lax.Precision.HIGHEST)


def _norm_proj(x2, gain, w_bf16, w_gate_f32=None, *, tm=512, tn=512):
    t, d = x2.shape
    n = w_bf16.shape[1]
    has_gate = w_gate_f32 is not None
    in_specs = [pl.BlockSpec((tm, d), lambda i: (i, 0)),
                pl.BlockSpec((1, d), lambda i: (0, 0)),
                pl.BlockSpec((d, n), lambda i: (0, 0))]
    args = [x2, gain.reshape(1, d), w_bf16]
    out_shape = [jax.ShapeDtypeStruct((t, n), BF16)]
    out_specs = [pl.BlockSpec((tm, n), lambda i: (i, 0))]
    if has_gate:
        in_specs.append(pl.BlockSpec((d, LANES), lambda i: (0, 0)))
        args.append(w_gate_f32)
        out_shape.append(jax.ShapeDtypeStruct((t, LANES), F32))
        out_specs.append(pl.BlockSpec((tm, LANES), lambda i: (i, 0)))
    res = pl.pallas_call(
        functools.partial(_norm_proj_kernel, tn=tn, has_gate=has_gate),
        grid=(t // tm,), in_specs=in_specs, out_specs=out_specs, out_shape=out_shape,
        compiler_params=_cparams("parallel"), name="norm_proj")(*args)
    return res if has_gate else res[0]


def _out_proj_kernel(*refs, n_in):
    res_ref = refs[0]
    a_refs = refs[1:1 + n_in]
    w_refs = refs[1 + n_in:1 + 2 * n_in]
    o_ref = refs[1 + 2 * n_in]
    acc = res_ref[...]
    for a_ref, w_ref in zip(a_refs, w_refs):
        acc = acc + jnp.dot(a_ref[...], w_ref[...], preferred_element_type=F32)
    o_ref[...] = acc


def _out_proj(res, acts, ws, *, tm=512):
    t, d = res.shape
    n_in = len(acts)
    in_specs = [pl.BlockSpec((tm, d), lambda i: (i, 0))]
    in_specs += [pl.BlockSpec((tm, a.shape[1]), lambda i: (i, 0)) for a in acts]
    in_specs += [pl.BlockSpec(w.shape, lambda i: (0, 0)) for w in ws]
    return pl.pallas_call(
        functools.partial(_out_proj_kernel, n_in=n_in),
        grid=(t // tm,), in_specs=in_specs, out_specs=pl.BlockSpec((tm, d), lambda i: (i, 0)),
        out_shape=jax.ShapeDtypeStruct((t, d), F32),
        compiler_params=_cparams("parallel"), name="out_proj")(res, *acts, *ws)


def _retention_kernel(q_ref, k_ref, v_ref, z_ref, cos_ref, sin_ref, decay_ref, xi_ref, zeta_ref, gam_ref,
                      o_ref, state_ref, *, n_chunks):
    c = RET_CHUNK
    half = RET_QK_DIM // 2

    @pl.when(pl.program_id(2) == 0)
    def _():
        state_ref[...] = jnp.zeros_like(state_ref)

    cos = cos_ref[...]
    sin = sin_ref[...]

    def rotate(t):
        a = t[:, :half].astype(F32)
        b = t[:, half:].astype(F32)
        return jnp.concatenate([a * cos - b * sin, b * cos + a * sin], axis=-1)

    q = rotate(q_ref[0])
    k = rotate(k_ref[0]) * (RET_QK_DIM ** -0.5)
    decay = decay_ref[0]
    gamma_c = gam_ref[0, 0:1, 0:1]
    for ci in range(n_chunks):
        rows = slice(ci * c, (ci + 1) * c)
        qc = q[rows].astype(BF16)
        kc = k[rows]
        vc = v_ref[0, rows, :]
        state = state_ref[...]
        inner = lax.dot_general(qc, kc.astype(BF16), (((1,), (1,)), ((), ())),
                                preferred_element_type=F32) * decay
        out = jnp.dot(inner.astype(BF16), vc, preferred_element_type=F32)
        out = out + jnp.dot(qc, state.astype(BF16), preferred_element_type=F32) * xi_ref[0][:, 0:1]
        kz = (kc * zeta_ref[0][:, 0:1]).astype(BF16)
        state_ref[...] = state * gamma_c + lax.dot_general(
            kz, vc, (((0,), (0,)), ((), ())), preferred_element_type=F32)
        o = out * lax.rsqrt(jnp.mean(out * out, axis=-1, keepdims=True) + EPS)
        z = z_ref[0, rows, :].astype(F32)
        o_ref[0, rows, :] = (_silu(z) * o).astype(o_ref.dtype)


def _retention(p, bsz, s, *, blk=512):
    h = RET_HEADS
    c = RET_CHUNK
    half = RET_QK_DIM // 2
    pos = jnp.arange(s, dtype=F32)
    freq = 1.0 / (XPOS_THETA ** jnp.linspace(0.0, 1.0, half, dtype=F32))
    ang = pos[:, None] * freq[None, :]
    cos, sin = jnp.cos(ang), jnp.sin(ang)
    log_gamma = jnp.log(1.0 - 2.0 ** (-5.0 - jnp.arange(h, dtype=F32)))
    idx = jnp.arange(c, dtype=F32)
    rel = idx[:, None] - idx[None, :]
    decay = jnp.exp(jnp.where(rel >= 0, rel * log_gamma[:, None, None], NEG_INF))
    xi = jnp.broadcast_to(jnp.exp((idx + 1.0) * log_gamma[:, None])[..., None], (h, c, LANES))
    zeta = jnp.broadcast_to(jnp.exp((c - 1.0 - idx) * log_gamma[:, None])[..., None], (h, c, LANES))
    gam = jnp.broadcast_to(jnp.exp(c * log_gamma)[:, None, None], (h, 8, LANES))
    nq = RET_QK_W // RET_QK_DIM
    nv = RET_V_W // RET_V_DIM
    return pl.pallas_call(
        functools.partial(_retention_kernel, n_chunks=blk // c),
        grid=(bsz, h, s // blk),
        in_specs=[
            pl.BlockSpec((1, blk, RET_QK_DIM), lambda b, hh, i: (b, i, hh)),
            pl.BlockSpec((1, blk, RET_QK_DIM), lambda b, hh, i: (b, i, nq + hh)),
            pl.BlockSpec((1, blk, RET_V_DIM), lambda b, hh, i: (b, i, nv + hh)),
            pl.BlockSpec((1, blk, RET_V_DIM), lambda b, hh, i: (b, i, 2 * nv + hh)),
            pl.BlockSpec((blk, half), lambda b, hh, i: (i, 0)),
            pl.BlockSpec((blk, half), lambda b, hh, i: (i, 0)),
            pl.BlockSpec((1, c, c), lambda b, hh, i: (hh, 0, 0)),
            pl.BlockSpec((1, c, LANES), lambda b, hh, i: (hh, 0, 0)),
            pl.BlockSpec((1, c, LANES), lambda b, hh, i: (hh, 0, 0)),
            pl.BlockSpec((1, 8, LANES), lambda b, hh, i: (hh, 0, 0)),
        ],
        out_specs=pl.BlockSpec((1, blk, RET_V_DIM), lambda b, hh, i: (b, i, hh)),
        out_shape=jax.ShapeDtypeStruct((bsz, s, RET_V_W), BF16),
        scratch_shapes=[pltpu.VMEM((RET_QK_DIM, RET_V_DIM), F32)],
        compiler_params=_cparams("parallel", "parallel", "arbitrary"), name="retention",
    )(p, p, p, p, cos, sin, decay, xi, zeta, gam)


def _swa_kernel(sink_ref, q_ref, kc_ref, kp_ref, vc_ref, vp_ref, cos_ref, sin_ref, cosp_ref, sinp_ref,
                qg_ref, kg_ref, bd_ref, o_ref):
    w = WINDOW
    dh = SWA_HEAD_DIM
    grp = SWA_Q_HEADS // SWA_KV_HEADS
    n = pl.program_id(1)

    def head_rms(x, gain, bd):
        sq = x * x
        hi = sq.astype(BF16)
        lo = (sq - hi.astype(F32)).astype(BF16)
        ssum = jnp.dot(hi, bd, preferred_element_type=F32) + jnp.dot(lo, bd, preferred_element_type=F32)
        return x * lax.rsqrt(ssum * (1.0 / dh) + EPS) * gain

    def rope(x, cos, sin):
        width = x.shape[-1]
        lane = lax.broadcasted_iota(jnp.int32, x.shape, 1) % dh
        partner = jnp.where(lane < ROT_DIM // 2, pltpu.roll(x, width - ROT_DIM // 2, 1),
                            pltpu.roll(x, ROT_DIM // 2, 1))
        return x * cos + partner * sin

    cos, sin = cos_ref[...], sin_ref[...]
    cosq = jnp.concatenate([cos] * (SWA_Q_W // LANES), axis=1)
    sinq = jnp.concatenate([sin] * (SWA_Q_W // LANES), axis=1)
    q = rope(head_rms(q_ref[0].astype(F32), qg_ref[...], bd_ref[...]), cosq, sinq)
    q = (q * (dh ** -0.5)).astype(BF16)
    bdk = bd_ref[0:SWA_KV_W, 0:SWA_KV_W]
    kcur = rope(head_rms(kc_ref[0].astype(F32), kg_ref[...], bdk), cos, sin)
    kprev = rope(head_rms(kp_ref[0].astype(F32), kg_ref[...], bdk), cosp_ref[...], sinp_ref[...])
    kband = jnp.concatenate([kprev, kcur], axis=0).astype(BF16)
    vband = jnp.concatenate([vp_ref[0], vc_ref[0]], axis=0)

    i = lax.broadcasted_iota(jnp.int32, (w, 2 * w), 0)
    j = lax.broadcasted_iota(jnp.int32, (w, 2 * w), 1)
    mask = (j > i) & (j <= i + w) & ((j >= w) | (n > 0))
    outs = []
    for hq in range(SWA_Q_HEADS):
        hk = hq // grp
        qh = q[:, hq * dh:(hq + 1) * dh]
        kh = kband[:, hk * dh:(hk + 1) * dh]
        vh = vband[:, hk * dh:(hk + 1) * dh]
        sc = lax.dot_general(qh, kh, (((1,), (1,)), ((), ())), preferred_element_type=F32)
        sc = jnp.where(mask, sc, NEG_INF)
        sink = sink_ref[hq]
        m = jnp.maximum(jnp.max(sc, axis=-1, keepdims=True), sink)
        p = jnp.exp(sc - m)
        denom = jnp.sum(p, axis=-1, keepdims=True) + jnp.exp(sink - m)
        pv = jnp.dot(p.astype(BF16), vh, preferred_element_type=F32)
        outs.append(pv * (1.0 / denom))
    o_ref[0] = jnp.concatenate(outs, axis=-1).astype(o_ref.dtype)


def _rope_tables(s):
    half = ROT_DIM // 2
    pos = jnp.arange(s, dtype=F32)
    inv_freq = 1.0 / (ROPE_THETA ** (jnp.arange(half, dtype=F32) * 2.0 / ROT_DIM))
    ang = pos[:, None] * inv_freq[None, :]
    cos, sin = jnp.cos(ang), jnp.sin(ang)
    ones = jnp.ones((s, SWA_HEAD_DIM - ROT_DIM), F32)
    cos_h = jnp.concatenate([cos, cos, ones], axis=1)
    sin_h = jnp.concatenate([-sin, sin, 0.0 * ones], axis=1)
    reps = LANES // SWA_HEAD_DIM
    return jnp.tile(cos_h, (1, reps)), jnp.tile(sin_h, (1, reps))


def _swa(p, sinks, q_gain, k_gain, bsz, s):
    w = WINDOW
    cos, sin = _rope_tables(s)
    qg = jnp.tile(q_gain.astype(F32), SWA_Q_HEADS).reshape(1, SWA_Q_W)
    kg = jnp.tile(k_gain.astype(F32), SWA_KV_HEADS).reshape(1, SWA_KV_W)
    seg = np.arange(SWA_Q_W) // SWA_HEAD_DIM
    bd = jnp.asarray(seg[:, None] == seg[None, :], dtype=BF16)
    kblk = SWA_Q_W // SWA_KV_W
    prev = lambda b, i: (b, jnp.maximum(i - 1, 0), 0)
    return pl.pallas_call(
        _swa_kernel,
        grid=(bsz, s // w),
        in_specs=[
            pl.BlockSpec(memory_space=pltpu.SMEM),
            pl.BlockSpec((1, w, SWA_Q_W), lambda b, i: (b, i, 0)),
            pl.BlockSpec((1, w, SWA_KV_W), lambda b, i: (b, i, kblk)),
            pl.BlockSpec((1, w, SWA_KV_W), lambda b, i: (b, jnp.maximum(i - 1, 0), kblk)),
            pl.BlockSpec((1, w, SWA_KV_W), lambda b, i: (b, i, kblk + 1)),
            pl.BlockSpec((1, w, SWA_KV_W), lambda b, i: (b, jnp.maximum(i - 1, 0), kblk + 1)),
            pl.BlockSpec((w, LANES), lambda b, i: (i, 0)),
            pl.BlockSpec((w, LANES), lambda b, i: (i, 0)),
            pl.BlockSpec((w, LANES), lambda b, i: (jnp.maximum(i - 1, 0), 0)),
            pl.BlockSpec((w, LANES), lambda b, i: (jnp.maximum(i - 1, 0), 0)),
            pl.BlockSpec((1, SWA_Q_W), lambda b, i: (0, 0)),
            pl.BlockSpec((1, SWA_KV_W), lambda b, i: (0, 0)),
            pl.BlockSpec((SWA_Q_W, SWA_Q_W), lambda b, i: (0, 0)),
        ],
        out_specs=pl.BlockSpec((1, w, SWA_Q_W), lambda b, i: (b, i, 0)),
        out_shape=jax.ShapeDtypeStruct((bsz, s, SWA_Q_W), BF16),
        compiler_params=_cparams("parallel", "parallel"), name="swa",
    )(sinks.astype(F32), p, p, p, p, p, cos, sin, cos, sin, qg, kg, bd)


def _softplus(x):
    return jnp.maximum(x, 0.0) + jnp.log1p(jnp.exp(-jnp.abs(x)))


def _gdn_kernel(alog_ref, dt_ref, q_ref, k_ref, v_ref, z_ref, gate_ref, cwq_ref, cwk_ref, cwv_ref, gain_ref,
                o_ref, state_ref, xbuf_ref, *, blk):
    c = GDN_CHUNK
    dk = GDN_HEAD_DIM
    pad = 8
    hd = pl.program_id(1)

    @pl.when(pl.program_id(2) == 0)
    def _():
        state_ref[...] = jnp.zeros_like(state_ref)
        xbuf_ref[:, 0:pad, :] = jnp.zeros((3, pad, dk), F32)

    conv = []
    for idx, (src, cw) in enumerate(((q_ref, cwq_ref), (k_ref, cwk_ref), (v_ref, cwv_ref))):
        xbuf_ref[idx, pad:pad + blk, :] = src[0].astype(F32)
        acc = None
        for j in range(CONV_K):
            off = pad - (CONV_K - 1) + j
            term = xbuf_ref[idx, off:off + blk, :] * cw[j:j + 1, :]
            acc = term if acc is None else acc + term
        conv.append(_silu(acc))
        xbuf_ref[idx, 0:pad, :] = xbuf_ref[idx, blk:blk + pad, :]
    q, k, v = conv
    q = q * lax.rsqrt(jnp.sum(q * q, axis=-1, keepdims=True) + EPS) * (dk ** -0.5)
    k = k * lax.rsqrt(jnp.sum(k * k, axis=-1, keepdims=True) + EPS)

    gates = gate_ref[0]
    lane = lax.broadcasted_iota(jnp.int32, gates.shape, 1)
    ba_col = jnp.sum(jnp.where(lane == hd, gates, 0.0), axis=-1, keepdims=True)
    bb_col = jnp.sum(jnp.where(lane == hd + GDN_HEADS, gates, 0.0), axis=-1, keepdims=True)
    a_neg = -jnp.exp(alog_ref[hd])
    dt = dt_ref[hd]
    g_col = a_neg * _softplus(ba_col + dt)
    beta = 1.0 / (1.0 + jnp.exp(-bb_col))
    sr = lax.broadcasted_iota(jnp.int32, (8, LANES), 0)
    sc = lax.broadcasted_iota(jnp.int32, (8, LANES), 1)
    sel = jnp.where((sr == 0) & (sc == hd), 1.0, 0.0).astype(F32)
    ba_row = lax.dot_general(sel, gates, (((1,), (1,)), ((), ())), preferred_element_type=F32,
                             precision=lax.Precision.HIGHEST)
    g_row = a_neg * _softplus(ba_row + dt)

    ii = lax.broadcasted_iota(jnp.int32, (blk, blk), 0)
    jj = lax.broadcasted_iota(jnp.int32, (blk, blk), 1)
    same = (ii // c) == (jj // c)
    incl = same & (ii >= jj)
    strict = same & (ii > jj)
    tri = jnp.where(incl, 1.0, 0.0).astype(F32)
    gcb = jnp.dot(tri, jnp.broadcast_to(g_col, (blk, LANES)), preferred_element_type=F32,
                  precision=lax.Precision.HIGHEST)
    gcr = lax.dot_general(g_row, tri, (((1,), (1,)), ((), ())), preferred_element_type=F32,
                          precision=lax.Precision.HIGHEST)[0:1, :]
    diff = jnp.concatenate([gcb] * (blk // LANES), axis=1) - gcr
    decay = jnp.exp(jnp.where(incl, diff, NEG_INF))
    eg = jnp.exp(gcb)

    kb = k * beta
    kbf = k.astype(BF16)
    a = lax.dot_general(kb.astype(BF16), kbf, (((1,), (1,)), ((), ())), preferred_element_type=F32)
    m = jnp.where(strict, -(a * decay), 0.0)
    eye = jnp.where(ii == jj, 1.0, 0.0).astype(F32)
    r = eye + m
    mp = m
    steps = int(np.log2(c)) - 1
    for _ in range(steps):
        mpb = mp.astype(BF16)
        mp = jnp.dot(mpb, mpb, preferred_element_type=F32)
        r = r + jnp.dot(r.astype(BF16), mp.astype(BF16), preferred_element_type=F32)
    rhs = jnp.concatenate([v * beta, kb * eg], axis=1).astype(BF16)
    uw = jnp.dot(r.astype(BF16), rhs, preferred_element_type=F32)
    u, w = uw[:, :dk], uw[:, dk:]
    qk = lax.dot_general(q.astype(BF16), kbf, (((1,), (1,)), ((), ())), preferred_element_type=F32)
    intra = (qk * decay).astype(BF16)
    wu = jnp.concatenate([w, u], axis=1).astype(BF16)
    iwu = jnp.dot(intra, wu, preferred_element_type=F32)
    qe = (q * eg - iwu[:, :dk]).astype(BF16)
    o0 = iwu[:, dk:]

    outs = []
    for ci in range(blk // c):
        rows = slice(ci * c, (ci + 1) * c)
        g_last = gcb[(ci + 1) * c - 1:(ci + 1) * c, :]
        k_dec = (k[rows] * jnp.exp(g_last - gcb[rows])).astype(BF16)
        kwu = lax.dot_general(k_dec, wu[rows], (((0,), (0,)), ((), ())), preferred_element_type=F32)
        state = state_ref[...]
        sb = state.astype(BF16)
        outs.append(jnp.dot(qe[rows], sb, preferred_element_type=F32) + o0[rows])
        state_ref[...] = state * jnp.exp(g_last) + (
            kwu[:, dk:] - jnp.dot(kwu[:, :dk].astype(BF16), sb, preferred_element_type=F32))
    out = jnp.concatenate(outs, axis=0)
    o = out * lax.rsqrt(jnp.mean(out * out, axis=-1, keepdims=True) + EPS) * gain_ref[...]
    o_ref[0] = (o * _silu(z_ref[0].astype(F32))).astype(o_ref.dtype)


def _gdn(p, pg, conv_w, a_log, dt_bias, gdn_gain, bsz, s, *, blk=256):
    h = GDN_HEADS
    dk = GDN_HEAD_DIM
    base = (SWA_Q_W + 2 * SWA_KV_W) // dk
    smem = pl.BlockSpec(memory_space=pltpu.SMEM)

    def col(off):
        return pl.BlockSpec((1, blk, dk), lambda b, hh, i: (b, i, off + hh))

    def cwspec(off):
        return pl.BlockSpec((CONV_K, dk), lambda b, hh, i: (0, off + hh))

    cw = conv_w.astype(F32)
    return pl.pallas_call(
        functools.partial(_gdn_kernel, blk=blk),
        grid=(bsz, h, s // blk),
        in_specs=[smem, smem, col(base), col(base + h), col(base + 2 * h), col(base + 3 * h),
                  pl.BlockSpec((1, blk, LANES), lambda b, hh, i: (b, i, 0)),
                  cwspec(0), cwspec(h), cwspec(2 * h),
                  pl.BlockSpec((1, dk), lambda b, hh, i: (0, 0))],
        out_specs=pl.BlockSpec((1, blk, dk), lambda b, hh, i: (b, i, hh)),
        out_shape=jax.ShapeDtypeStruct((bsz, s, GDN_W), BF16),
        scratch_shapes=[pltpu.VMEM((dk, dk), F32), pltpu.VMEM((3, blk + 8, dk), F32)],
        compiler_params=_cparams("parallel", "parallel", "arbitrary"), name="gdn",
    )(a_log.astype(F32), dt_bias.astype(F32), p, p, p, p, pg, cw, cw, cw, gdn_gain.astype(F32).reshape(1, dk))


MOE_BM = 256
ROUTER_ROWS = LANES
EXPERT_ROW0 = 8


def _router_kernel(x_ref, g_ref, wr_ref, br_ref, h_ref, eid_ref, gate_ref):
    x = x_ref[...]
    h = x * lax.rsqrt(jnp.mean(x * x, axis=-1, keepdims=True) + EPS) * g_ref[...]
    h_ref[...] = h
    tm = x.shape[0]
    logits = lax.dot_general(wr_ref[...], h, (((1,), (1,)), ((), ())), preferred_element_type=F32,
                             precision=lax.Precision.HIGHEST) + br_ref[...]
    row = lax.broadcasted_iota(jnp.int32, (8, tm), 0)
    gl = jnp.where(row < N_GROUPS, logits[0:8, :], NEG_INF)
    gmax = jnp.max(gl, axis=0, keepdims=True)
    grp_p = 1.0 / jnp.sum(jnp.exp(gl - gmax), axis=0, keepdims=True)
    grp_idx = jnp.min(jnp.where(gl == gmax, row, 8), axis=0, keepdims=True)
    el = jnp.zeros((8, tm), F32)
    for g in range(N_GROUPS):
        lo = EXPERT_ROW0 + g * EXPERTS_PER_GROUP
        el = jnp.where(grp_idx == g, logits[lo:lo + EXPERTS_PER_GROUP, :], el)
    m1 = jnp.max(el, axis=0, keepdims=True)
    i1 = jnp.min(jnp.where(el == m1, row, 8), axis=0, keepdims=True)
    el2 = jnp.where(row == i1, NEG_INF, el)
    m2 = jnp.max(el2, axis=0, keepdims=True)
    i2 = jnp.min(jnp.where(el2 == m2, row, 8), axis=0, keepdims=True)
    z = jnp.sum(jnp.exp(el - m1), axis=0, keepdims=True)
    p1 = 1.0 / z
    p2 = jnp.exp(m2 - m1) / z
    scale = grp_p / (p1 + p2)
    e1 = grp_idx * EXPERTS_PER_GROUP + i1
    e2 = grp_idx * EXPERTS_PER_GROUP + i2
    eid_ref[...] = jnp.where(row == 0, e1, jnp.where(row == 1, e2, 0))
    gate_ref[...] = jnp.where(row == 0, p1 * scale, jnp.where(row == 1, p2 * scale, 0.0))


def _moe_router(x2, gain, w_rg, b_rg, w_re, b_re, *, tm=512):
    t, d = x2.shape
    wr = jnp.zeros((ROUTER_ROWS, d), F32)
    wr = wr.at[0:N_GROUPS].set(w_rg.astype(F32).T).at[EXPERT_ROW0:EXPERT_ROW0 + N_EXPERTS].set(w_re.astype(F32).T)
    br = jnp.zeros((ROUTER_ROWS, 1), F32)
    br = br.at[0:N_GROUPS, 0].set(b_rg.astype(F32)).at[EXPERT_ROW0:EXPERT_ROW0 + N_EXPERTS, 0].set(b_re.astype(F32))
    return pl.pallas_call(
        _router_kernel, grid=(t // tm,),
        in_specs=[pl.BlockSpec((tm, d), lambda i: (i, 0)),
                  pl.BlockSpec((1, d), lambda i: (0, 0)),
                  pl.BlockSpec((ROUTER_ROWS, d), lambda i: (0, 0)),
                  pl.BlockSpec((ROUTER_ROWS, 1), lambda i: (0, 0))],
        out_specs=[pl.BlockSpec((tm, d), lambda i: (i, 0)),
                   pl.BlockSpec((8, tm), lambda i: (0, i)),
                   pl.BlockSpec((8, tm), lambda i: (0, i))],
        out_shape=[jax.ShapeDtypeStruct((t, d), F32),
                   jax.ShapeDtypeStruct((8, t), jnp.int32),
                   jax.ShapeDtypeStruct((8, t), F32)],
        compiler_params=_cparams("parallel"), name="moe_router")(x2, gain.reshape(1, d), wr, br)


def _rank_kernel(eid_ref, dest_ref, meta_ref, cnt_ref, off_ref, *, bm):
    ph = pl.program_id(0)
    i = pl.program_id(1)
    tm = eid_ref.shape[1]
    erow = lax.broadcasted_iota(jnp.int32, (N_EXPERTS, tm), 0)
    oh0 = jnp.where(erow == eid_ref[0:1, :], 1.0, 0.0).astype(F32)
    oh1 = jnp.where(erow == eid_ref[1:2, :], 1.0, 0.0).astype(F32)
    oh = oh0 + oh1
    tile_cnt = jnp.sum(oh, axis=1, keepdims=True)

    @pl.when((ph == 0) & (i == 0))
    def _():
        cnt_ref[...] = jnp.zeros_like(cnt_ref)

    @pl.when(ph == 0)
    def _():
        cnt_ref[...] += tile_cnt

    @pl.when((ph == 1) & (i == 0))
    def _():
        cnt = cnt_ref[...]
        padded = jnp.ceil(cnt * (1.0 / bm)) * bm
        er = lax.broadcasted_iota(jnp.int32, (N_EXPERTS, N_EXPERTS), 0)
        ec = lax.broadcasted_iota(jnp.int32, (N_EXPERTS, N_EXPERTS), 1)
        lower = jnp.where(ec < er, 1.0, 0.0).astype(F32)
        start = jnp.dot(lower, padded, preferred_element_type=F32, precision=lax.Precision.HIGHEST)
        off_ref[...] = start
        end = start + padded
        nb = meta_ref.shape[1]
        blk_row = lax.broadcasted_iota(jnp.int32, (N_EXPERTS, nb), 1).astype(F32) * bm
        endb = jnp.concatenate([end] * (nb // LANES), axis=1)
        be = jnp.sum(jnp.where(endb <= blk_row, 1.0, 0.0), axis=0, keepdims=True)
        be = jnp.minimum(be, N_EXPERTS - 1.0)
        used = jnp.concatenate([end[N_EXPERTS - 1:N_EXPERTS, :]] * (nb // LANES), axis=1) * (1.0 / bm)
        mrow = lax.broadcasted_iota(jnp.int32, (8, nb), 0)
        meta_ref[...] = jnp.where(mrow == 0, be, jnp.where(mrow == 1, used, 0.0)).astype(jnp.int32)
        cnt_ref[...] = jnp.zeros_like(cnt_ref)

    @pl.when(ph == 1)
    def _():
        kr = lax.broadcasted_iota(jnp.int32, (tm, tm), 0)
        kc = lax.broadcasted_iota(jnp.int32, (tm, tm), 1)
        su = jnp.where(kr < kc, 1.0, 0.0).astype(BF16)
        csum = jnp.dot(oh.astype(BF16), su, preferred_element_type=F32)
        pos = csum + (off_ref[...] + cnt_ref[...])[:, 0:1]
        d0 = jnp.sum(oh0 * pos, axis=0, keepdims=True)
        d1 = jnp.sum(oh1 * pos, axis=0, keepdims=True)
        row = lax.broadcasted_iota(jnp.int32, (8, tm), 0)
        dest_ref[...] = jnp.where(row == 0, d0, jnp.where(row == 1, d1, 0.0)).astype(jnp.int32)
        cnt_ref[...] += tile_cnt


def _moe_rank(eid, n_blocks, *, bm, tm=1024):
    t = eid.shape[1]
    nb = -(-n_blocks // LANES) * LANES
    return pl.pallas_call(
        functools.partial(_rank_kernel, bm=bm), grid=(2, t // tm),
        in_specs=[pl.BlockSpec((8, tm), lambda p, i: (0, i))],
        out_specs=[pl.BlockSpec((8, tm), lambda p, i: (0, i * p)),
                   pl.BlockSpec((8, nb), lambda p, i: (0, 0))],
        out_shape=[jax.ShapeDtypeStruct((8, t), jnp.int32), jax.ShapeDtypeStruct((8, nb), jnp.int32)],
        scratch_shapes=[pltpu.VMEM((N_EXPERTS, LANES), F32), pltpu.VMEM((N_EXPERTS, LANES), F32)],
        compiler_params=_cparams("arbitrary", "arbitrary"), name="moe_rank")(eid)


def _dispatch_kernel(dest_ref, h_hbm, rows_in_hbm, rows_hbm, sem, *, tm):
    del rows_in_hbm
    base = pl.program_id(0) * tm

    def row_copy(j):
        tok = base + lax.rem(j, tm)
        return pltpu.make_async_copy(h_hbm.at[pl.ds(tok, 1), :],
                                     rows_hbm.at[pl.ds(dest_ref[0, 0, j], 1), :], sem)

    def issue(j, carry):
        row_copy(j).start()
        return carry

    def drain(j, carry):
        row_copy(j).wait()
        return carry

    lax.fori_loop(0, 2 * tm, issue, 0)
    lax.fori_loop(0, 2 * tm, drain, 0)


def _moe_dispatch(dest_tiles, h, n_rows, *, tm):
    t, d = h.shape
    zeros = jnp.zeros((n_rows, d), F32)
    return pl.pallas_call(
        functools.partial(_dispatch_kernel, tm=tm), grid=(t // tm,),
        in_specs=[pl.BlockSpec((1, 1, 2 * tm), lambda i: (i, 0, 0), memory_space=pltpu.SMEM),
                  pl.BlockSpec(memory_space=pl.ANY), pl.BlockSpec(memory_space=pl.ANY)],
        out_specs=pl.BlockSpec(memory_space=pl.ANY),
        out_shape=jax.ShapeDtypeStruct((n_rows, d), F32),
        scratch_shapes=[pltpu.SemaphoreType.DMA(())],
        input_output_aliases={2: 0},
        compiler_params=pltpu.CompilerParams(dimension_semantics=("arbitrary",), has_side_effects=True),
        name="moe_dispatch")(dest_tiles, h, zeros)


def _experts_kernel(be_ref, used_ref, x_ref, wg_ref, wu_ref, wd_ref, y_ref, wgb_ref, wub_ref, wdb_ref):
    b = pl.program_id(0)
    prev = be_ref[jnp.maximum(b - 1, 0)]

    @pl.when((b == 0) | (be_ref[b] != prev))
    def _():
        wgb_ref[...] = wg_ref[0].astype(BF16)
        wub_ref[...] = wu_ref[0].astype(BF16)
        wdb_ref[...] = wd_ref[0].astype(BF16)

    @pl.when(b < used_ref[0])
    def _():
        x = x_ref[...].astype(BF16)
        gate = jnp.dot(x, wgb_ref[...], preferred_element_type=F32)
        up = jnp.dot(x, wub_ref[...], preferred_element_type=F32)
        hid = (_silu(gate) * up).astype(BF16)
        y_ref[...] = jnp.dot(hid, wdb_ref[...], preferred_element_type=F32)

    @pl.when(b >= used_ref[0])
    def _():
        y_ref[...] = jnp.zeros_like(y_ref)


def _moe_experts(block_expert, n_used, x_rows, w_gate, w_up, w_down, *, bm):
    n_rows, d = x_rows.shape
    de = w_gate.shape[2]
    grid_spec = pltpu.PrefetchScalarGridSpec(
        num_scalar_prefetch=2, grid=(n_rows // bm,),
        in_specs=[pl.BlockSpec((bm, d), lambda b, be, nu: (jnp.minimum(b, jnp.maximum(nu[0] - 1, 0)), 0)),
                  pl.BlockSpec((1, d, de), lambda b, be, nu: (be[b], 0, 0)),
                  pl.BlockSpec((1, d, de), lambda b, be, nu: (be[b], 0, 0)),
                  pl.BlockSpec((1, de, d), lambda b, be, nu: (be[b], 0, 0))],
        out_specs=pl.BlockSpec((bm, d), lambda b, be, nu: (b, 0)),
        scratch_shapes=[pltpu.VMEM((d, de), BF16), pltpu.VMEM((d, de), BF16), pltpu.VMEM((de, d), BF16)])
    return pl.pallas_call(
        _experts_kernel, grid_spec=grid_spec, out_shape=jax.ShapeDtypeStruct((n_rows, d), F32),
        compiler_params=_cparams("arbitrary"), name="moe_experts",
    )(block_expert, n_used, x_rows, w_gate, w_up, w_down)


def _combine_kernel(dest_ref, x_ref, gate_ref, y_hbm, o_ref, buf_ref, sem, *, tm):
    def row_copy(j):
        k = j // tm
        return pltpu.make_async_copy(y_hbm.at[pl.ds(dest_ref[0, 0, j], 1), :],
                                     buf_ref.at[k, pl.ds(j - k * tm, 1), :], sem)

    def issue(j, carry):
        row_copy(j).start()
        return carry

    def drain(j, carry):
        row_copy(j).wait()
        return carry

    lax.fori_loop(0, 2 * tm, issue, 0)
    er = lax.broadcasted_iota(jnp.int32, (8, LANES), 0)
    ec = lax.broadcasted_iota(jnp.int32, (8, LANES), 1)
    eye = jnp.where(er == ec, 1.0, 0.0).astype(F32)
    gcol = lax.dot_general(gate_ref[...], eye, (((0,), (0,)), ((), ())), preferred_element_type=F32,
                           precision=lax.Precision.HIGHEST)
    lax.fori_loop(0, 2 * tm, drain, 0)
    o_ref[...] = x_ref[...] + gcol[:, 0:1] * buf_ref[0] + gcol[:, 1:2] * buf_ref[1]


def _moe_combine(dest_tiles, x2, gates, y_rows, *, tm):
    t, d = x2.shape
    return pl.pallas_call(
        functools.partial(_combine_kernel, tm=tm), grid=(t // tm,),
        in_specs=[pl.BlockSpec((1, 1, 2 * tm), lambda i: (i, 0, 0), memory_space=pltpu.SMEM),
                  pl.BlockSpec((tm, d), lambda i: (i, 0)),
                  pl.BlockSpec((8, tm), lambda i: (0, i)),
                  pl.BlockSpec(memory_space=pl.ANY)],
        out_specs=pl.BlockSpec((tm, d), lambda i: (i, 0)),
        out_shape=jax.ShapeDtypeStruct((t, d), F32),
        scratch_shapes=[pltpu.VMEM((2, tm, d), F32), pltpu.SemaphoreType.DMA(())],
        compiler_params=_cparams("arbitrary"), name="moe_combine")(dest_tiles, x2, gates, y_rows)


def _moe(x2, gain, w_rg, b_rg, w_re, b_re, w_gate, w_up, w_down, *, tm_dma=256):
    t, d = x2.shape
    bm = MOE_BM
    n_rows = t * TOP_K + N_EXPERTS * bm
    n_blocks = n_rows // bm
    h, eid, gates = _moe_router(x2, gain, w_rg, b_rg, w_re, b_re)
    dest, meta = _moe_rank(eid, n_blocks, bm=bm)
    dest_tiles = dest[0:TOP_K].reshape(TOP_K, t // tm_dma, tm_dma).transpose(1, 0, 2).reshape(
        t // tm_dma, 1, TOP_K * tm_dma)
    x_rows = _moe_dispatch(dest_tiles, h, n_rows, tm=tm_dma)
    y_rows = _moe_experts(meta[0, :n_blocks], meta[1, 0:1], x_rows, w_gate, w_up, w_down, bm=bm)
    return _moe_combine(dest_tiles, x2, gates, y_rows, tm=tm_dma)


def _even_odd_cols(w):
    d = w.shape[0]
    w = w.reshape(d, RET_HEADS, RET_QK_DIM // 2, 2)
    return jnp.concatenate([w[..., 0], w[..., 1]], axis=-1).reshape(d, RET_QK_W)


def kernel(x, ln_mix_even, w_in_even, conv_w_even, a_log_even, dt_bias_even, q_gain_even, k_gain_even, sinks_even, gdn_gain_even, w_out_even, ln_mix_odd, w_in_odd, w_out_odd, ln_ffn, w_router_group, b_router_group, w_router_expert, b_router_expert, w_gate, w_up, w_down):
    bsz, s, d = x.shape
    t = bsz * s
    x2 = x.reshape(t, d).astype(F32)
    depth = ln_ffn.shape[0]
    for layer in range(depth):
        i = layer // 2
        if layer % 2 == 0:
            w_in = w_in_even[i]
            w_main = w_in[:, :EVEN_MAIN].astype(BF16)
            w_gates = jnp.pad(w_in[:, EVEN_MAIN:].astype(F32), ((0, 0), (0, LANES - 2 * GDN_HEADS)))
            p, pg = _norm_proj(x2, ln_mix_even[i], w_main, w_gates)
            p = p.reshape(bsz, s, EVEN_MAIN)
            y_a = _swa(p, sinks_even[i], q_gain_even[i], k_gain_even[i], bsz, s)
            y_b = _gdn(p, pg.reshape(bsz, s, LANES), conv_w_even[i], a_log_even[i], dt_bias_even[i],
                       gdn_gain_even[i], bsz, s)
            w_out = w_out_even[i].astype(BF16)
            x2 = _out_proj(x2, [y_a.reshape(t, SWA_Q_W), y_b.reshape(t, GDN_W)], [w_out[:SWA_Q_W], w_out[SWA_Q_W:]])
        else:
            w_in = w_in_odd[i]
            w_perm = jnp.concatenate([_even_odd_cols(w_in[:, :RET_QK_W]),
                                      _even_odd_cols(w_in[:, RET_QK_W:2 * RET_QK_W]),
                                      w_in[:, 2 * RET_QK_W:]], axis=1).astype(BF16)
            p = _norm_proj(x2, ln_mix_odd[i], w_perm).reshape(bsz, s, ODD_IN)
            y = _retention(p, bsz, s)
            x2 = _out_proj(x2, [y.reshape(t, RET_V_W)], [w_out_odd[i].astype(BF16)])
        x2 = _moe(x2, ln_ffn[layer], w_router_group[layer], b_router_group[layer], w_router_expert[layer],
                  b_router_expert[layer], w_gate[layer], w_up[layer], w_down[layer])
    return x2.reshape(bsz, s, d).astype(x.dtype)
```

```python
import functools

import numpy as np
import jax
import jax.numpy as jnp
from jax import lax
from jax.experimental import pallas as pl
from jax.experimental.pallas import tpu as pltpu

F32 = jnp.float32
BF16 = jnp.bfloat16

D_MODEL = 1024
SWA_Q_HEADS = 8
SWA_KV_HEADS = 2
SWA_HEAD_DIM = 64
WINDOW = 128
ROT_DIM = SWA_HEAD_DIM // 4
ROPE_THETA = 500000.0
GDN_HEADS = 4
GDN_HEAD_DIM = 128
CONV_K = 4
GDN_CHUNK = 64
RET_HEADS = 4
RET_QK_DIM = 256
RET_V_DIM = 512
RET_CHUNK = 128
XPOS_THETA = 10000.0
N_GROUPS = 4
EXPERTS_PER_GROUP = 8
N_EXPERTS = N_GROUPS * EXPERTS_PER_GROUP
TOP_K = 2
D_EXPERT = 512
EPS = 1e-6

SWA_Q_W = SWA_Q_HEADS * SWA_HEAD_DIM
SWA_KV_W = SWA_KV_HEADS * SWA_HEAD_DIM
GDN_W = GDN_HEADS * GDN_HEAD_DIM
EVEN_MAIN = SWA_Q_W + 2 * SWA_KV_W + 4 * GDN_W
RET_QK_W = RET_HEADS * RET_QK_DIM
RET_V_W = RET_HEADS * RET_V_DIM
ODD_IN = 2 * RET_QK_W + 2 * RET_V_W

LANES = 128
VMEM_LIMIT = 56 * 1024 * 1024

NEG_INF = float("-inf")


def _cparams(*sem):
    return pltpu.CompilerParams(dimension_semantics=sem, vmem_limit_bytes=VMEM_LIMIT)


def _silu(x):
    return x * (1.0 / (1.0 + jnp.exp(-x)))


def _norm_proj_kernel(x_ref, g_ref, w_ref, *rest, tn, has_gate):
    if has_gate:
        wg_ref, o_ref, og_ref = rest
    else:
        (o_ref,) = rest
    x = x_ref[...]
    h = x * lax.rsqrt(jnp.mean(x * x, axis=-1, keepdims=True) + EPS) * g_ref[...]
    hb = h.astype(BF16)
    n = o_ref.shape[-1]
    for lo in range(0, n, tn):
        hi = min(lo + tn, n)
        o_ref[:, lo:hi] = jnp.dot(hb, w_ref[:, lo:hi], preferred_element_type=F32).astype(o_ref.dtype)
    if has_gate:
        og_ref[...] = jnp.dot(h, wg_ref[...], preferred_element_type=F32, precision=lax.Precision.HIGHEST)


def _norm_proj(x2, gain, w_bf16, w_gate_f32=None, *, tm=512, tn=512):
    t, d = x2.shape
    n = w_bf16.shape[1]
    has_gate = w_gate_f32 is not None
    in_specs = [pl.BlockSpec((tm, d), lambda i: (i, 0)),
                pl.BlockSpec((1, d), lambda i: (0, 0)),
                pl.BlockSpec((d, n), lambda i: (0, 0))]
    args = [x2, gain.reshape(1, d), w_bf16]
    out_shape = [jax.ShapeDtypeStruct((t, n), BF16)]
    out_specs = [pl.BlockSpec((tm, n), lambda i: (i, 0))]
    if has_gate:
        in_specs.append(pl.BlockSpec((d, LANES), lambda i: (0, 0)))
        args.append(w_gate_f32)
        out_shape.append(jax.ShapeDtypeStruct((t, LANES), F32))
        out_specs.append(pl.BlockSpec((tm, LANES), lambda i: (i, 0)))
    res = pl.pallas_call(
        functools.partial(_norm_proj_kernel, tn=tn, has_gate=has_gate),
        grid=(t // tm,), in_specs=in_specs, out_specs=out_specs, out_shape=out_shape,
        compiler_params=_cparams("parallel"), name="norm_proj")(*args)
    return res if has_gate else res[0]


def _out_proj_kernel(*refs, n_in):
    res_ref = refs[0]
    a_refs = refs[1:1 + n_in]
    w_refs = refs[1 + n_in:1 + 2 * n_in]
    o_ref = refs[1 + 2 * n_in]
    acc = res_ref[...]
    for a_ref, w_ref in zip(a_refs, w_refs):
        acc = acc + jnp.dot(a_ref[...], w_ref[...], preferred_element_type=F32)
    o_ref[...] = acc


def _out_proj(res, acts, ws, *, tm=512):
    t, d = res.shape
    n_in = len(acts)
    in_specs = [pl.BlockSpec((tm, d), lambda i: (i, 0))]
    in_specs += [pl.BlockSpec((tm, a.shape[1]), lambda i: (i, 0)) for a in acts]
    in_specs += [pl.BlockSpec(w.shape, lambda i: (0, 0)) for w in ws]
    return pl.pallas_call(
        functools.partial(_out_proj_kernel, n_in=n_in),
        grid=(t // tm,), in_specs=in_specs, out_specs=pl.BlockSpec((tm, d), lambda i: (i, 0)),
        out_shape=jax.ShapeDtypeStruct((t, d), F32),
        compiler_params=_cparams("parallel"), name="out_proj")(res, *acts, *ws)


def _retention_kernel(q_ref, k_ref, v_ref, z_ref, cos_ref, sin_ref, decay_ref, xi_ref, zeta_ref, gam_ref,
                      o_ref, state_ref, *, n_chunks):
    c = RET_CHUNK
    half = RET_QK_DIM // 2

    @pl.when(pl.program_id(2) == 0)
    def _():
        state_ref[...] = jnp.zeros_like(state_ref)

    cos = cos_ref[...]
    sin = sin_ref[...]

    def rotate(t):
        a = t[:, :half].astype(F32)
        b = t[:, half:].astype(F32)
        return jnp.concatenate([a * cos - b * sin, b * cos + a * sin], axis=-1)

    q = rotate(q_ref[0])
    k = rotate(k_ref[0]) * (RET_QK_DIM ** -0.5)
    decay = decay_ref[0]
    gamma_c = gam_ref[0, 0:1, 0:1]
    for ci in range(n_chunks):
        rows = slice(ci * c, (ci + 1) * c)
        qc = q[rows].astype(BF16)
        kc = k[rows]
        vc = v_ref[0, rows, :]
        state = state_ref[...]
        inner = lax.dot_general(qc, kc.astype(BF16), (((1,), (1,)), ((), ())),
                                preferred_element_type=F32) * decay
        out = jnp.dot(inner.astype(BF16), vc, preferred_element_type=F32)
        out = out + jnp.dot(qc, state.astype(BF16), preferred_element_type=F32) * xi_ref[0][:, 0:1]
        kz = (kc * zeta_ref[0][:, 0:1]).astype(BF16)
        state_ref[...] = state * gamma_c + lax.dot_general(
            kz, vc, (((0,), (0,)), ((), ())), preferred_element_type=F32)
        o = out * lax.rsqrt(jnp.mean(out * out, axis=-1, keepdims=True) + EPS)
        z = z_ref[0, rows, :].astype(F32)
        o_ref[0, rows, :] = (_silu(z) * o).astype(o_ref.dtype)


def _retention(p, bsz, s, *, blk=512):
    h = RET_HEADS
    c = RET_CHUNK
    half = RET_QK_DIM // 2
    pos = jnp.arange(s, dtype=F32)
    freq = 1.0 / (XPOS_THETA ** jnp.linspace(0.0, 1.0, half, dtype=F32))
    ang = pos[:, None] * freq[None, :]
    cos, sin = jnp.cos(ang), jnp.sin(ang)
    log_gamma = jnp.log(1.0 - 2.0 ** (-5.0 - jnp.arange(h, dtype=F32)))
    idx = jnp.arange(c, dtype=F32)
    rel = idx[:, None] - idx[None, :]
    decay = jnp.exp(jnp.where(rel >= 0, rel * log_gamma[:, None, None], NEG_INF))
    xi = jnp.broadcast_to(jnp.exp((idx + 1.0) * log_gamma[:, None])[..., None], (h, c, LANES))
    zeta = jnp.broadcast_to(jnp.exp((c - 1.0 - idx) * log_gamma[:, None])[..., None], (h, c, LANES))
    gam = jnp.broadcast_to(jnp.exp(c * log_gamma)[:, None, None], (h, 8, LANES))
    nq = RET_QK_W // RET_QK_DIM
    nv = RET_V_W // RET_V_DIM
    return pl.pallas_call(
        functools.partial(_retention_kernel, n_chunks=blk // c),
        grid=(bsz, h, s // blk),
        in_specs=[
            pl.BlockSpec((1, blk, RET_QK_DIM), lambda b, hh, i: (b, i, hh)),
            pl.BlockSpec((1, blk, RET_QK_DIM), lambda b, hh, i: (b, i, nq + hh)),
            pl.BlockSpec((1, blk, RET_V_DIM), lambda b, hh, i: (b, i, nv + hh)),
            pl.BlockSpec((1, blk, RET_V_DIM), lambda b, hh, i: (b, i, 2 * nv + hh)),
            pl.BlockSpec((blk, half), lambda b, hh, i: (i, 0)),
            pl.BlockSpec((blk, half), lambda b, hh, i: (i, 0)),
            pl.BlockSpec((1, c, c), lambda b, hh, i: (hh, 0, 0)),
            pl.BlockSpec((1, c, LANES), lambda b, hh, i: (hh, 0, 0)),
            pl.BlockSpec((1, c, LANES), lambda b, hh, i: (hh, 0, 0)),
            pl.BlockSpec((1, 8, LANES), lambda b, hh, i: (hh, 0, 0)),
        ],
        out_specs=pl.BlockSpec((1, blk, RET_V_DIM), lambda b, hh, i: (b, i, hh)),
        out_shape=jax.ShapeDtypeStruct((bsz, s, RET_V_W), BF16),
        scratch_shapes=[pltpu.VMEM((RET_QK_DIM, RET_V_DIM), F32)],
        compiler_params=_cparams("parallel", "parallel", "arbitrary"), name="retention",
    )(p, p, p, p, cos, sin, decay, xi, zeta, gam)


def _swa_kernel(sink_ref, q_ref, kc_ref, kp_ref, vc_ref, vp_ref, cos_ref, sin_ref, cosp_ref, sinp_ref,
                qg_ref, kg_ref, bd_ref, o_ref):
    w = WINDOW
    dh = SWA_HEAD_DIM
    grp = SWA_Q_HEADS // SWA_KV_HEADS
    n = pl.program_id(1)

    def head_rms(x, gain, bd):
        sq = x * x
        hi = sq.astype(BF16)
        lo = (sq - hi.astype(F32)).astype(BF16)
        ssum = jnp.dot(hi, bd, preferred_element_type=F32) + jnp.dot(lo, bd, preferred_element_type=F32)
        return x * lax.rsqrt(ssum * (1.0 / dh) + EPS) * gain

    def rope(x, cos, sin):
        width = x.shape[-1]
        lane = lax.broadcasted_iota(jnp.int32, x.shape, 1) % dh
        partner = jnp.where(lane < ROT_DIM // 2, pltpu.roll(x, width - ROT_DIM // 2, 1),
                            pltpu.roll(x, ROT_DIM // 2, 1))
        return x * cos + partner * sin

    cos, sin = cos_ref[...], sin_ref[...]
    cosq = jnp.concatenate([cos] * (SWA_Q_W // LANES), axis=1)
    sinq = jnp.concatenate([sin] * (SWA_Q_W // LANES), axis=1)
    q = rope(head_rms(q_ref[0].astype(F32), qg_ref[...], bd_ref[...]), cosq, sinq)
    q = (q * (dh ** -0.5)).astype(BF16)
    bdk = bd_ref[0:SWA_KV_W, 0:SWA_KV_W]
    kcur = rope(head_rms(kc_ref[0].astype(F32), kg_ref[...], bdk), cos, sin)
    kprev = rope(head_rms(kp_ref[0].astype(F32), kg_ref[...], bdk), cosp_ref[...], sinp_ref[...])
    kband = jnp.concatenate([kprev, kcur], axis=0).astype(BF16)
    vband = jnp.concatenate([vp_ref[0], vc_ref[0]], axis=0)

    i = lax.broadcasted_iota(jnp.int32, (w, 2 * w), 0)
    j = lax.broadcasted_iota(jnp.int32, (w, 2 * w), 1)
    mask = (j > i) & (j <= i + w) & ((j >= w) | (n > 0))
    outs = []
    for hq in range(SWA_Q_HEADS):
        hk = hq // grp
        qh = q[:, hq * dh:(hq + 1) * dh]
        kh = kband[:, hk * dh:(hk + 1) * dh]
        vh = vband[:, hk * dh:(hk + 1) * dh]
        sc = lax.dot_general(qh, kh, (((1,), (1,)), ((), ())), preferred_element_type=F32)
        sc = jnp.where(mask, sc, NEG_INF)
        sink = sink_ref[hq]
        m = jnp.maximum(jnp.max(sc, axis=-1, keepdims=True), sink)
        p = jnp.exp(sc - m)
        denom = jnp.sum(p, axis=-1, keepdims=True) + jnp.exp(sink - m)
        pv = jnp.dot(p.astype(BF16), vh, preferred_element_type=F32)
        outs.append(pv * (1.0 / denom))
    o_ref[0] = jnp.concatenate(outs, axis=-1).astype(o_ref.dtype)


def _rope_tables(s):
    half = ROT_DIM // 2
    pos = jnp.arange(s, dtype=F32)
    inv_freq = 1.0 / (ROPE_THETA ** (jnp.arange(half, dtype=F32) * 2.0 / ROT_DIM))
    ang = pos[:, None] * inv_freq[None, :]
    cos, sin = jnp.cos(ang), jnp.sin(ang)
    ones = jnp.ones((s, SWA_HEAD_DIM - ROT_DIM), F32)
    cos_h = jnp.concatenate([cos, cos, ones], axis=1)
    sin_h = jnp.concatenate([-sin, sin, 0.0 * ones], axis=1)
    reps = LANES // SWA_HEAD_DIM
    return jnp.tile(cos_h, (1, reps)), jnp.tile(sin_h, (1, reps))


def _swa(p, sinks, q_gain, k_gain, bsz, s):
    w = WINDOW
    cos, sin = _rope_tables(s)
    qg = jnp.tile(q_gain.astype(F32), SWA_Q_HEADS).reshape(1, SWA_Q_W)
    kg = jnp.tile(k_gain.astype(F32), SWA_KV_HEADS).reshape(1, SWA_KV_W)
    seg = np.arange(SWA_Q_W) // SWA_HEAD_DIM
    bd = jnp.asarray(seg[:, None] == seg[None, :], dtype=BF16)
    kblk = SWA_Q_W // SWA_KV_W
    prev = lambda b, i: (b, jnp.maximum(i - 1, 0), 0)
    return pl.pallas_call(
        _swa_kernel,
        grid=(bsz, s // w),
        in_specs=[
            pl.BlockSpec(memory_space=pltpu.SMEM),
            pl.BlockSpec((1, w, SWA_Q_W), lambda b, i: (b, i, 0)),
            pl.BlockSpec((1, w, SWA_KV_W), lambda b, i: (b, i, kblk)),
            pl.BlockSpec((1, w, SWA_KV_W), lambda b, i: (b, jnp.maximum(i - 1, 0), kblk)),
            pl.BlockSpec((1, w, SWA_KV_W), lambda b, i: (b, i, kblk + 1)),
            pl.BlockSpec((1, w, SWA_KV_W), lambda b, i: (b, jnp.maximum(i - 1, 0), kblk + 1)),
            pl.BlockSpec((w, LANES), lambda b, i: (i, 0)),
            pl.BlockSpec((w, LANES), lambda b, i: (i, 0)),
            pl.BlockSpec((w, LANES), lambda b, i: (jnp.maximum(i - 1, 0), 0)),
            pl.BlockSpec((w, LANES), lambda b, i: (jnp.maximum(i - 1, 0), 0)),
            pl.BlockSpec((1, SWA_Q_W), lambda b, i: (0, 0)),
            pl.BlockSpec((1, SWA_KV_W), lambda b, i: (0, 0)),
            pl.BlockSpec((SWA_Q_W, SWA_Q_W), lambda b, i: (0, 0)),
        ],
        out_specs=pl.BlockSpec((1, w, SWA_Q_W), lambda b, i: (b, i, 0)),
        out_shape=jax.ShapeDtypeStruct((bsz, s, SWA_Q_W), BF16),
        compiler_params=_cparams("parallel", "parallel"), name="swa",
    )(sinks.astype(F32), p, p, p, p, p, cos, sin, cos, sin, qg, kg, bd)


def _softplus(x):
    return jnp.maximum(x, 0.0) + jnp.log1p(jnp.exp(-jnp.abs(x)))


def _gdn_kernel(alog_ref, dt_ref, q_ref, k_ref, v_ref, z_ref, gate_ref, cwq_ref, cwk_ref, cwv_ref, gain_ref,
                o_ref, state_ref, xbuf_ref, *, blk):
    c = GDN_CHUNK
    dk = GDN_HEAD_DIM
    pad = 8
    hd = pl.program_id(1)

    @pl.when(pl.program_id(2) == 0)
    def _():
        state_ref[...] = jnp.zeros_like(state_ref)
        xbuf_ref[:, 0:pad, :] = jnp.zeros((3, pad, dk), F32)

    conv = []
    for idx, (src, cw) in enumerate(((q_ref, cwq_ref), (k_ref, cwk_ref), (v_ref, cwv_ref))):
        xbuf_ref[idx, pad:pad + blk, :] = src[0].astype(F32)
        acc = None
        for j in range(CONV_K):
            off = pad - (CONV_K - 1) + j
            term = xbuf_ref[idx, off:off + blk, :] * cw[j:j + 1, :]
            acc = term if acc is None else acc + term
        conv.append(_silu(acc))
        xbuf_ref[idx, 0:pad, :] = xbuf_ref[idx, blk:blk + pad, :]
    q, k, v = conv
    q = q * lax.rsqrt(jnp.sum(q * q, axis=-1, keepdims=True) + EPS) * (dk ** -0.5)
    k = k * lax.rsqrt(jnp.sum(k * k, axis=-1, keepdims=True) + EPS)

    gates = gate_ref[0]
    lane = lax.broadcasted_iota(jnp.int32, gates.shape, 1)
    ba_col = jnp.sum(jnp.where(lane == hd, gates, 0.0), axis=-1, keepdims=True)
    bb_col = jnp.sum(jnp.where(lane == hd + GDN_HEADS, gates, 0.0), axis=-1, keepdims=True)
    a_neg = -jnp.exp(alog_ref[hd])
    dt = dt_ref[hd]
    g_col = a_neg * _softplus(ba_col + dt)
    beta = 1.0 / (1.0 + jnp.exp(-bb_col))
    sr = lax.broadcasted_iota(jnp.int32, (8, LANES), 0)
    sc = lax.broadcasted_iota(jnp.int32, (8, LANES), 1)
    sel = jnp.where((sr == 0) & (sc == hd), 1.0, 0.0).astype(F32)
    ba_row = lax.dot_general(sel, gates, (((1,), (1,)), ((), ())), preferred_element_type=F32,
                             precision=lax.Precision.HIGHEST)
    g_row = a_neg * _softplus(ba_row + dt)

    ii = lax.broadcasted_iota(jnp.int32, (blk, blk), 0)
    jj = lax.broadcasted_iota(jnp.int32, (blk, blk), 1)
    same = (ii // c) == (jj // c)
    incl = same & (ii >= jj)
    strict = same & (ii > jj)
    tri = jnp.where(incl, 1.0, 0.0).astype(F32)
    gcb = jnp.dot(tri, jnp.broadcast_to(g_col, (blk, LANES)), preferred_element_type=F32,
                  precision=lax.Precision.HIGHEST)
    gcr = lax.dot_general(g_row, tri, (((1,), (1,)), ((), ())), preferred_element_type=F32,
                          precision=lax.Precision.HIGHEST)[0:1, :]
    diff = jnp.concatenate([gcb] * (blk // LANES), axis=1) - gcr
    decay = jnp.exp(jnp.where(incl, diff, NEG_INF))
    eg = jnp.exp(gcb)

    kb = k * beta
    kbf = k.astype(BF16)
    a = lax.dot_general(kb.astype(BF16), kbf, (((1,), (1,)), ((), ())), preferred_element_type=F32)
    m = jnp.where(strict, -(a * decay), 0.0)
    eye = jnp.where(ii == jj, 1.0, 0.0).astype(F32)
    r = eye + m
    mp = m
    steps = int(np.log2(c)) - 1
    for _ in range(steps):
        mpb = mp.astype(BF16)
        mp = jnp.dot(mpb, mpb, preferred_element_type=F32)
        r = r + jnp.dot(r.astype(BF16), mp.astype(BF16), preferred_element_type=F32)
    rhs = jnp.concatenate([v * beta, kb * eg], axis=1).astype(BF16)
    uw = jnp.dot(r.astype(BF16), rhs, preferred_element_type=F32)
    u, w = uw[:, :dk], uw[:, dk:]
    qk = lax.dot_general(q.astype(BF16), kbf, (((1,), (1,)), ((), ())), preferred_element_type=F32)
    intra = (qk * decay).astype(BF16)
    wu = jnp.concatenate([w, u], axis=1).astype(BF16)
    iwu = jnp.dot(intra, wu, preferred_element_type=F32)
    qe = (q * eg - iwu[:, :dk]).astype(BF16)
    o0 = iwu[:, dk:]

    outs = []
    for ci in range(blk // c):
        rows = slice(ci * c, (ci + 1) * c)
        g_last = gcb[(ci + 1) * c - 1:(ci + 1) * c, :]
        k_dec = (k[rows] * jnp.exp(g_last - gcb[rows])).astype(BF16)
        kwu = lax.dot_general(k_dec, wu[rows], (((0,), (0,)), ((), ())), preferred_element_type=F32)
        state = state_ref[...]
        sb = state.astype(BF16)
        outs.append(jnp.dot(qe[rows], sb, preferred_element_type=F32) + o0[rows])
        state_ref[...] = state * jnp.exp(g_last) + (
            kwu[:, dk:] - jnp.dot(kwu[:, :dk].astype(BF16), sb, preferred_element_type=F32))
    out = jnp.concatenate(outs, axis=0)
    o = out * lax.rsqrt(jnp.mean(out * out, axis=-1, keepdims=True) + EPS) * gain_ref[...]
    o_ref[0] = (o * _silu(z_ref[0].astype(F32))).astype(o_ref.dtype)


def _gdn(p, pg, conv_w, a_log, dt_bias, gdn_gain, bsz, s, *, blk=256):
    h = GDN_HEADS
    dk = GDN_HEAD_DIM
    base = (SWA_Q_W + 2 * SWA_KV_W) // dk
    smem = pl.BlockSpec(memory_space=pltpu.SMEM)

    def col(off):
        return pl.BlockSpec((1, blk, dk), lambda b, hh, i: (b, i, off + hh))

    def cwspec(off):
        return pl.BlockSpec((CONV_K, dk), lambda b, hh, i: (0, off + hh))

    cw = conv_w.astype(F32)
    return pl.pallas_call(
        functools.partial(_gdn_kernel, blk=blk),
        grid=(bsz, h, s // blk),
        in_specs=[smem, smem, col(base), col(base + h), col(base + 2 * h), col(base + 3 * h),
                  pl.BlockSpec((1, blk, LANES), lambda b, hh, i: (b, i, 0)),
                  cwspec(0), cwspec(h), cwspec(2 * h),
                  pl.BlockSpec((1, dk), lambda b, hh, i: (0, 0))],
        out_specs=pl.BlockSpec((1, blk, dk), lambda b, hh, i: (b, i, hh)),
        out_shape=jax.ShapeDtypeStruct((bsz, s, GDN_W), BF16),
        scratch_shapes=[pltpu.VMEM((dk, dk), F32), pltpu.VMEM((3, blk + 8, dk), F32)],
        compiler_params=_cparams("parallel", "parallel", "arbitrary"), name="gdn",
    )(a_log.astype(F32), dt_bias.astype(F32), p, p, p, p, pg, cw, cw, cw, gdn_gain.astype(F32).reshape(1, dk))


MOE_BM = 256
ROUTER_ROWS = LANES
EXPERT_ROW0 = 8


SLAB = 8
DMA_UNROLL = 4


def _store_slabs(ref, val, rows):
    for sl in range(SLAB):
        ref[pl.ds(sl, rows, stride=SLAB), :] = val[:, sl * LANES:(sl + 1) * LANES]


def _load_slabs(ref, rows):
    return jnp.concatenate([ref[pl.ds(sl, rows, stride=SLAB), :] for sl in range(SLAB)], axis=1)


def _slab(ref, row):
    return ref.at[pl.ds(pl.multiple_of(row * SLAB, SLAB), SLAB), :]


def _router_kernel(x_ref, g_ref, wr_ref, br_ref, h_ref, eid_ref, gate_ref):
    x = x_ref[...]
    h = x * lax.rsqrt(jnp.mean(x * x, axis=-1, keepdims=True) + EPS) * g_ref[...]
    tm = x.shape[0]
    _store_slabs(h_ref, h, tm)
    logits = lax.dot_general(wr_ref[...], h, (((1,), (1,)), ((), ())), preferred_element_type=F32,
                             precision=lax.Precision.HIGHEST) + br_ref[...]
    row = lax.broadcasted_iota(jnp.int32, (8, tm), 0)
    gl = jnp.where(row < N_GROUPS, logits[0:8, :], NEG_INF)
    gmax = jnp.max(gl, axis=0, keepdims=True)
    grp_p = 1.0 / jnp.sum(jnp.exp(gl - gmax), axis=0, keepdims=True)
    grp_idx = jnp.min(jnp.where(gl == gmax, row, 8), axis=0, keepdims=True)
    el = jnp.zeros((8, tm), F32)
    for g in range(N_GROUPS):
        lo = EXPERT_ROW0 + g * EXPERTS_PER_GROUP
        el = jnp.where(grp_idx == g, logits[lo:lo + EXPERTS_PER_GROUP, :], el)
    m1 = jnp.max(el, axis=0, keepdims=True)
    i1 = jnp.min(jnp.where(el == m1, row, 8), axis=0, keepdims=True)
    el2 = jnp.where(row == i1, NEG_INF, el)
    m2 = jnp.max(el2, axis=0, keepdims=True)
    i2 = jnp.min(jnp.where(el2 == m2, row, 8), axis=0, keepdims=True)
    z = jnp.sum(jnp.exp(el - m1), axis=0, keepdims=True)
    p1 = 1.0 / z
    p2 = jnp.exp(m2 - m1) / z
    scale = grp_p / (p1 + p2)
    e1 = grp_idx * EXPERTS_PER_GROUP + i1
    e2 = grp_idx * EXPERTS_PER_GROUP + i2
    eid_ref[...] = jnp.where(row == 0, e1, jnp.where(row == 1, e2, 0))
    gate_ref[...] = jnp.where(row == 0, p1 * scale, jnp.where(row == 1, p2 * scale, 0.0))


def _moe_router(x2, gain, w_rg, b_rg, w_re, b_re, *, tm=512):
    t, d = x2.shape
    wr = jnp.zeros((ROUTER_ROWS, d), F32)
    wr = wr.at[0:N_GROUPS].set(w_rg.astype(F32).T).at[EXPERT_ROW0:EXPERT_ROW0 + N_EXPERTS].set(w_re.astype(F32).T)
    br = jnp.zeros((ROUTER_ROWS, 1), F32)
    br = br.at[0:N_GROUPS, 0].set(b_rg.astype(F32)).at[EXPERT_ROW0:EXPERT_ROW0 + N_EXPERTS, 0].set(b_re.astype(F32))
    return pl.pallas_call(
        _router_kernel, grid=(t // tm,),
        in_specs=[pl.BlockSpec((tm, d), lambda i: (i, 0)),
                  pl.BlockSpec((1, d), lambda i: (0, 0)),
                  pl.BlockSpec((ROUTER_ROWS, d), lambda i: (0, 0)),
                  pl.BlockSpec((ROUTER_ROWS, 1), lambda i: (0, 0))],
        out_specs=[pl.BlockSpec((tm * SLAB, LANES), lambda i: (i, 0)),
                   pl.BlockSpec((8, tm), lambda i: (0, i)),
                   pl.BlockSpec((8, tm), lambda i: (0, i))],
        out_shape=[jax.ShapeDtypeStruct((t * SLAB, LANES), F32),
                   jax.ShapeDtypeStruct((8, t), jnp.int32),
                   jax.ShapeDtypeStruct((8, t), F32)],
        compiler_params=_cparams("parallel"), name="moe_router")(x2, gain.reshape(1, d), wr, br)


def _rank_kernel(eid_ref, dest_ref, meta_ref, cnt_ref, off_ref, *, bm):
    ph = pl.program_id(0)
    i = pl.program_id(1)
    tm = eid_ref.shape[1]
    erow = lax.broadcasted_iota(jnp.int32, (N_EXPERTS, tm), 0)
    oh0 = jnp.where(erow == eid_ref[0:1, :], 1.0, 0.0).astype(F32)
    oh1 = jnp.where(erow == eid_ref[1:2, :], 1.0, 0.0).astype(F32)
    oh = oh0 + oh1
    tile_cnt = jnp.sum(oh, axis=1, keepdims=True)

    @pl.when((ph == 0) & (i == 0))
    def _():
        cnt_ref[...] = jnp.zeros_like(cnt_ref)

    @pl.when(ph == 0)
    def _():
        cnt_ref[...] += tile_cnt

    @pl.when((ph == 1) & (i == 0))
    def _():
        cnt = cnt_ref[...]
        padded = jnp.ceil(cnt * (1.0 / bm)) * bm
        er = lax.broadcasted_iota(jnp.int32, (N_EXPERTS, N_EXPERTS), 0)
        ec = lax.broadcasted_iota(jnp.int32, (N_EXPERTS, N_EXPERTS), 1)
        lower = jnp.where(ec < er, 1.0, 0.0).astype(F32)
        start = jnp.dot(lower, padded, preferred_element_type=F32, precision=lax.Precision.HIGHEST)
        off_ref[...] = start
        end = start + padded
        nb = meta_ref.shape[1]
        blk_row = lax.broadcasted_iota(jnp.int32, (N_EXPERTS, nb), 1).astype(F32) * bm
        endb = jnp.concatenate([end] * (nb // LANES), axis=1)
        be = jnp.sum(jnp.where(endb <= blk_row, 1.0, 0.0), axis=0, keepdims=True)
        be = jnp.minimum(be, N_EXPERTS - 1.0)
        used = jnp.concatenate([end[N_EXPERTS - 1:N_EXPERTS, :]] * (nb // LANES), axis=1) * (1.0 / bm)
        mrow = lax.broadcasted_iota(jnp.int32, (8, nb), 0)
        meta_ref[...] = jnp.where(mrow == 0, be, jnp.where(mrow == 1, used, 0.0)).astype(jnp.int32)
        cnt_ref[...] = jnp.zeros_like(cnt_ref)

    @pl.when(ph == 1)
    def _():
        kr = lax.broadcasted_iota(jnp.int32, (tm, tm), 0)
        kc = lax.broadcasted_iota(jnp.int32, (tm, tm), 1)
        su = jnp.where(kr < kc, 1.0, 0.0).astype(BF16)
        csum = jnp.dot(oh.astype(BF16), su, preferred_element_type=F32)
        pos = csum + (off_ref[...] + cnt_ref[...])[:, 0:1]
        d0 = jnp.sum(oh0 * pos, axis=0, keepdims=True)
        d1 = jnp.sum(oh1 * pos, axis=0, keepdims=True)
        row = lax.broadcasted_iota(jnp.int32, (8, tm), 0)
        dest_ref[...] = jnp.where(row == 0, d0, jnp.where(row == 1, d1, 0.0)).astype(jnp.int32)
        cnt_ref[...] += tile_cnt


def _moe_rank(eid, n_blocks, *, bm, tm=1024):
    t = eid.shape[1]
    nb = -(-n_blocks // LANES) * LANES
    return pl.pallas_call(
        functools.partial(_rank_kernel, bm=bm), grid=(2, t // tm),
        in_specs=[pl.BlockSpec((8, tm), lambda p, i: (0, i))],
        out_specs=[pl.BlockSpec((8, tm), lambda p, i: (0, i * p)),
                   pl.BlockSpec((8, nb), lambda p, i: (0, 0))],
        out_shape=[jax.ShapeDtypeStruct((8, t), jnp.int32), jax.ShapeDtypeStruct((8, nb), jnp.int32)],
        scratch_shapes=[pltpu.VMEM((N_EXPERTS, LANES), F32), pltpu.VMEM((N_EXPERTS, LANES), F32)],
        compiler_params=_cparams("arbitrary", "arbitrary"), name="moe_rank")(eid)


def _dispatch_kernel(dest_ref, h_ref, rows_in_hbm, rows_hbm, sem, *, tm):
    del rows_in_hbm

    def issue(g, carry):
        for u in range(DMA_UNROLL):
            tok = g * DMA_UNROLL + u
            for k in range(TOP_K):
                pltpu.make_async_copy(_slab(h_ref, tok), _slab(rows_hbm, dest_ref[0, 0, k * tm + tok]),
                                      sem).start(priority=k)
        return carry

    lax.fori_loop(0, tm // DMA_UNROLL, issue, 0)
    for _ in range(TOP_K):
        pltpu.make_async_copy(h_ref, rows_hbm.at[pl.ds(0, tm * SLAB), :], sem).wait()


def _moe_dispatch(dest_tiles, h_slabs, n_rows, *, tm):
    t = h_slabs.shape[0] // SLAB
    zeros = jnp.zeros((n_rows * SLAB, LANES), F32)
    return pl.pallas_call(
        functools.partial(_dispatch_kernel, tm=tm), grid=(t // tm,),
        in_specs=[pl.BlockSpec((1, 1, TOP_K * tm), lambda i: (i, 0, 0), memory_space=pltpu.SMEM),
                  pl.BlockSpec((tm * SLAB, LANES), lambda i: (i, 0)),
                  pl.BlockSpec(memory_space=pl.ANY)],
        out_specs=pl.BlockSpec(memory_space=pl.ANY),
        out_shape=jax.ShapeDtypeStruct((n_rows * SLAB, LANES), F32),
        scratch_shapes=[pltpu.SemaphoreType.DMA(())],
        input_output_aliases={2: 0},
        compiler_params=pltpu.CompilerParams(dimension_semantics=("arbitrary",), has_side_effects=True),
        name="moe_dispatch")(dest_tiles, h_slabs, zeros)


def _experts_kernel(be_ref, used_ref, x_ref, wg_ref, wu_ref, wd_ref, y_ref, wgb_ref, wub_ref, wdb_ref):
    b = pl.program_id(0)
    prev = be_ref[jnp.maximum(b - 1, 0)]

    @pl.when((b == 0) | (be_ref[b] != prev))
    def _():
        wgb_ref[...] = wg_ref[0].astype(BF16)
        wub_ref[...] = wu_ref[0].astype(BF16)
        wdb_ref[...] = wd_ref[0].astype(BF16)

    bm = x_ref.shape[0] // SLAB

    @pl.when(b < used_ref[0])
    def _():
        x = _load_slabs(x_ref, bm).astype(BF16)
        gate = jnp.dot(x, wgb_ref[...], preferred_element_type=F32)
        up = jnp.dot(x, wub_ref[...], preferred_element_type=F32)
        hid = (_silu(gate) * up).astype(BF16)
        _store_slabs(y_ref, jnp.dot(hid, wdb_ref[...], preferred_element_type=F32), bm)

    @pl.when(b >= used_ref[0])
    def _():
        y_ref[...] = jnp.zeros_like(y_ref)


def _moe_experts(block_expert, n_used, x_rows, w_gate, w_up, w_down, *, bm):
    n_rows = x_rows.shape[0] // SLAB
    d, de = w_gate.shape[1], w_gate.shape[2]
    grid_spec = pltpu.PrefetchScalarGridSpec(
        num_scalar_prefetch=2, grid=(n_rows // bm,),
        in_specs=[pl.BlockSpec((bm * SLAB, LANES),
                               lambda b, be, nu: (jnp.minimum(b, jnp.maximum(nu[0] - 1, 0)), 0)),
                  pl.BlockSpec((1, d, de), lambda b, be, nu: (be[b], 0, 0)),
                  pl.BlockSpec((1, d, de), lambda b, be, nu: (be[b], 0, 0)),
                  pl.BlockSpec((1, de, d), lambda b, be, nu: (be[b], 0, 0))],
        out_specs=pl.BlockSpec((bm * SLAB, LANES), lambda b, be, nu: (b, 0)),
        scratch_shapes=[pltpu.VMEM((d, de), BF16), pltpu.VMEM((d, de), BF16), pltpu.VMEM((de, d), BF16)])
    return pl.pallas_call(
        _experts_kernel, grid_spec=grid_spec, out_shape=jax.ShapeDtypeStruct((n_rows * SLAB, LANES), F32),
        compiler_params=_cparams("arbitrary"), name="moe_experts",
    )(block_expert, n_used, x_rows, w_gate, w_up, w_down)


def _combine_kernel(dest_ref, x_ref, gate_ref, y_hbm, o_ref, buf_ref, sem, *, tm):
    def issue(g, carry):
        for u in range(DMA_UNROLL):
            tok = g * DMA_UNROLL + u
            for k in range(TOP_K):
                pltpu.make_async_copy(_slab(y_hbm, dest_ref[0, 0, k * tm + tok]), _slab(buf_ref.at[k], tok),
                                      sem).start(priority=k)
        return carry

    lax.fori_loop(0, tm // DMA_UNROLL, issue, 0)
    er = lax.broadcasted_iota(jnp.int32, (8, LANES), 0)
    ec = lax.broadcasted_iota(jnp.int32, (8, LANES), 1)
    eye = jnp.where(er == ec, 1.0, 0.0).astype(F32)
    gcol = lax.dot_general(gate_ref[...], eye, (((0,), (0,)), ((), ())), preferred_element_type=F32,
                           precision=lax.Precision.HIGHEST)
    for k in range(TOP_K):
        pltpu.make_async_copy(y_hbm.at[pl.ds(0, tm * SLAB), :], buf_ref.at[k], sem).wait()
    o_ref[...] = (x_ref[...] + gcol[:, 0:1] * _load_slabs(buf_ref.at[0], tm)
                  + gcol[:, 1:2] * _load_slabs(buf_ref.at[1], tm))


def _moe_combine(dest_tiles, x2, gates, y_rows, *, tm):
    t, d = x2.shape
    return pl.pallas_call(
        functools.partial(_combine_kernel, tm=tm), grid=(t // tm,),
        in_specs=[pl.BlockSpec((1, 1, TOP_K * tm), lambda i: (i, 0, 0), memory_space=pltpu.SMEM),
                  pl.BlockSpec((tm, d), lambda i: (i, 0)),
                  pl.BlockSpec((8, tm), lambda i: (0, i)),
                  pl.BlockSpec(memory_space=pl.ANY)],
        out_specs=pl.BlockSpec((tm, d), lambda i: (i, 0)),
        out_shape=jax.ShapeDtypeStruct((t, d), F32),
        scratch_shapes=[pltpu.VMEM((TOP_K, tm * SLAB, LANES), F32), pltpu.SemaphoreType.DMA(())],
        compiler_params=_cparams("arbitrary"), name="moe_combine")(dest_tiles, x2, gates, y_rows)


def _moe(x2, gain, w_rg, b_rg, w_re, b_re, w_gate, w_up, w_down, *, tm_dma=512):
    t, d = x2.shape
    bm = MOE_BM
    n_rows = t * TOP_K + N_EXPERTS * bm
    n_blocks = n_rows // bm
    h, eid, gates = _moe_router(x2, gain, w_rg, b_rg, w_re, b_re)
    dest, meta = _moe_rank(eid, n_blocks, bm=bm)
    dest_tiles = dest[0:TOP_K].reshape(TOP_K, t // tm_dma, tm_dma).transpose(1, 0, 2).reshape(
        t // tm_dma, 1, TOP_K * tm_dma)
    x_rows = _moe_dispatch(dest_tiles, h, n_rows, tm=tm_dma)
    y_rows = _moe_experts(meta[0, :n_blocks], meta[1, 0:1], x_rows, w_gate, w_up, w_down, bm=bm)
    return _moe_combine(dest_tiles, x2, gates, y_rows, tm=tm_dma)


def _even_odd_cols(w):
    d = w.shape[0]
    w = w.reshape(d, RET_HEADS, RET_QK_DIM // 2, 2)
    return jnp.concatenate([w[..., 0], w[..., 1]], axis=-1).reshape(d, RET_QK_W)


def kernel(x, ln_mix_even, w_in_even, conv_w_even, a_log_even, dt_bias_even, q_gain_even, k_gain_even, sinks_even, gdn_gain_even, w_out_even, ln_mix_odd, w_in_odd, w_out_odd, ln_ffn, w_router_group, b_router_group, w_router_expert, b_router_expert, w_gate, w_up, w_down):
    bsz, s, d = x.shape
    t = bsz * s
    x2 = x.reshape(t, d).astype(F32)
    depth = ln_ffn.shape[0]
    for layer in range(depth):
        i = layer // 2
        if layer % 2 == 0:
            w_in = w_in_even[i]
            w_main = w_in[:, :EVEN_MAIN].astype(BF16)
            w_gates = jnp.pad(w_in[:, EVEN_MAIN:].astype(F32), ((0, 0), (0, LANES - 2 * GDN_HEADS)))
            p, pg = _norm_proj(x2, ln_mix_even[i], w_main, w_gates)
            p = p.reshape(bsz, s, EVEN_MAIN)
            y_a = _swa(p, sinks_even[i], q_gain_even[i], k_gain_even[i], bsz, s)
            y_b = _gdn(p, pg.reshape(bsz, s, LANES), conv_w_even[i], a_log_even[i], dt_bias_even[i],
                       gdn_gain_even[i], bsz, s)
            w_out = w_out_even[i].astype(BF16)
            x2 = _out_proj(x2, [y_a.reshape(t, SWA_Q_W), y_b.reshape(t, GDN_W)], [w_out[:SWA_Q_W], w_out[SWA_Q_W:]])
        else:
            w_in = w_in_odd[i]
            w_perm = jnp.concatenate([_even_odd_cols(w_in[:, :RET_QK_W]),
                                      _even_odd_cols(w_in[:, RET_QK_W:2 * RET_QK_W]),
                                      w_in[:, 2 * RET_QK_W:]], axis=1).astype(BF16)
            p = _norm_proj(x2, ln_mix_odd[i], w_perm).reshape(bsz, s, ODD_IN)
            y = _retention(p, bsz, s)
            x2 = _out_proj(x2, [y.reshape(t, RET_V_W)], [w_out_odd[i].astype(BF16)])
        x2 = _moe(x2, ln_ffn[layer], w_router_group[layer], b_router_group[layer], w_router_expert[layer],
                  b_router_expert[layer], w_gate[layer], w_up[layer], w_down[layer])
    return x2.reshape(bsz, s, d).astype(x.dtype)
```

```python
import functools

import numpy as np
import jax
import jax.numpy as jnp
from jax import lax
from jax.experimental import pallas as pl
from jax.experimental.pallas import tpu as pltpu

F32 = jnp.float32
BF16 = jnp.bfloat16

D_MODEL = 1024
SWA_Q_HEADS = 8
SWA_KV_HEADS = 2
SWA_HEAD_DIM = 64
WINDOW = 128
ROT_DIM = SWA_HEAD_DIM // 4
ROPE_THETA = 500000.0
GDN_HEADS = 4
GDN_HEAD_DIM = 128
CONV_K = 4
GDN_CHUNK = 64
RET_HEADS = 4
RET_QK_DIM = 256
RET_V_DIM = 512
RET_CHUNK = 128
XPOS_THETA = 10000.0
N_GROUPS = 4
EXPERTS_PER_GROUP = 8
N_EXPERTS = N_GROUPS * EXPERTS_PER_GROUP
TOP_K = 2
D_EXPERT = 512
EPS = 1e-6

SWA_Q_W = SWA_Q_HEADS * SWA_HEAD_DIM
SWA_KV_W = SWA_KV_HEADS * SWA_HEAD_DIM
GDN_W = GDN_HEADS * GDN_HEAD_DIM
EVEN_MAIN = SWA_Q_W + 2 * SWA_KV_W + 4 * GDN_W
EVEN_COL_Q = 0
EVEN_COL_GDN = SWA_Q_W
EVEN_COL_Z = EVEN_COL_GDN + 3 * GDN_W
EVEN_COL_K = EVEN_COL_Z + GDN_W
EVEN_COL_V = EVEN_COL_K + SWA_KV_W
RET_QK_W = RET_HEADS * RET_QK_DIM
RET_V_W = RET_HEADS * RET_V_DIM
ODD_IN = 2 * RET_QK_W + 2 * RET_V_W

LANES = 128
VMEM_LIMIT = 56 * 1024 * 1024

NEG_INF = float("-inf")


def _cparams(*sem):
    return pltpu.CompilerParams(dimension_semantics=sem, vmem_limit_bytes=VMEM_LIMIT)


def _silu(x):
    return x * (1.0 / (1.0 + jnp.exp(-x)))


def _norm_proj_kernel(x_ref, g_ref, w_ref, *rest, tn, has_gate, n_perm, perm_w, conv_cols, seq_tiles):
    rest = list(rest)
    wg_ref = rest.pop(0) if has_gate else None
    perm_ref = rest.pop(0) if n_perm else None
    cw_ref = rest.pop(0) if conv_cols else None
    o_ref = rest.pop(0)
    og_ref = rest.pop(0) if has_gate else None
    wp_ref = rest.pop(0) if n_perm else None
    cbuf_ref = rest.pop(0) if conv_cols else None
    tm = x_ref.shape[0]
    pad = 8

    if conv_cols:
        @pl.when(pl.program_id(0) % seq_tiles == 0)
        def _():
            cbuf_ref[0:pad, :] = jnp.zeros((pad, cbuf_ref.shape[1]), F32)

    if n_perm:
        @pl.when(pl.program_id(0) == 0)
        def _():
            for j in range(n_perm):
                cols = slice(j * perm_w, (j + 1) * perm_w)
                wp_ref[:, cols] = jnp.dot(w_ref[:, cols], perm_ref[...], preferred_element_type=F32).astype(BF16)

    x = x_ref[...]
    h = x * lax.rsqrt(jnp.mean(x * x, axis=-1, keepdims=True) + EPS) * g_ref[...]
    hb = h.astype(BF16)
    n = o_ref.shape[-1]
    for lo in range(0, n, tn):
        hi = min(lo + tn, n)
        src = wp_ref if hi <= n_perm * perm_w else w_ref
        res = jnp.dot(hb, src[:, lo:hi], preferred_element_type=F32)
        if conv_cols and conv_cols[0] <= lo and hi <= conv_cols[1]:
            cc = slice(lo - conv_cols[0], hi - conv_cols[0])
            cbuf_ref[pad:pad + tm, cc] = res
            res = None
            for j in range(CONV_K):
                off = pad - (CONV_K - 1) + j
                term = cbuf_ref[off:off + tm, cc] * cw_ref[j:j + 1, cc]
                res = term if res is None else res + term
            res = _silu(res)
            cbuf_ref[0:pad, cc] = cbuf_ref[tm:tm + pad, cc]
        o_ref[:, lo:hi] = res.astype(o_ref.dtype)
    if has_gate:
        h_lo = (h - hb.astype(F32)).astype(BF16)
        r = jnp.dot(hb, wg_ref[...], preferred_element_type=F32)
        og_ref[...] = r[:, :LANES] + r[:, LANES:] + jnp.dot(h_lo, wg_ref[:, :LANES], preferred_element_type=F32)


def _norm_proj(x2, gain, w_bf16, w_gate_f32=None, perm=None, n_perm=0, conv_w=None, conv_cols=None, seq_len=None,
               *, tm=512, tn=512):
    t, d = x2.shape
    n = w_bf16.shape[1]
    has_gate = w_gate_f32 is not None
    perm_w = perm.shape[0] if n_perm else 0
    assert (n_perm * perm_w) % tn == 0
    seq_tiles = seq_len // tm if conv_cols else 0
    assert not conv_cols or (conv_cols[0] % tn == 0 and conv_cols[1] % tn == 0 and seq_len % tm == 0)
    in_specs = [pl.BlockSpec((tm, d), lambda i: (i, 0)),
                pl.BlockSpec((1, d), lambda i: (0, 0)),
                pl.BlockSpec((d, n), lambda i: (0, 0))]
    args = [x2, gain.reshape(1, d), w_bf16]
    out_shape = [jax.ShapeDtypeStruct((t, n), BF16)]
    out_specs = [pl.BlockSpec((tm, n), lambda i: (i, 0))]
    scratch = []
    if has_gate:
        wg_hi = w_gate_f32.astype(BF16)
        wg_lo = (w_gate_f32 - wg_hi.astype(F32)).astype(BF16)
        in_specs.append(pl.BlockSpec((d, 2 * LANES), lambda i: (0, 0)))
        args.append(jnp.concatenate([wg_hi, wg_lo], axis=1))
        out_shape.append(jax.ShapeDtypeStruct((t, LANES), F32))
        out_specs.append(pl.BlockSpec((tm, LANES), lambda i: (i, 0)))
    if n_perm:
        in_specs.append(pl.BlockSpec((perm_w, perm_w), lambda i: (0, 0)))
        args.append(perm)
        scratch.append(pltpu.VMEM((d, n_perm * perm_w), BF16))
    if conv_cols:
        cw = conv_cols[1] - conv_cols[0]
        in_specs.append(pl.BlockSpec((CONV_K, cw), lambda i: (0, 0)))
        args.append(conv_w.astype(F32))
        scratch.append(pltpu.VMEM((tm + 8, cw), F32))
    res = pl.pallas_call(
        functools.partial(_norm_proj_kernel, tn=tn, has_gate=has_gate, n_perm=n_perm, perm_w=perm_w,
                          conv_cols=conv_cols, seq_tiles=seq_tiles),
        grid=(t // tm,), in_specs=in_specs, out_specs=out_specs, out_shape=out_shape, scratch_shapes=scratch,
        compiler_params=_cparams("arbitrary"), name="norm_proj")(*args)
    return res if has_gate else res[0]


def _out_proj_kernel(*refs, n_in):
    res_ref = refs[0]
    a_refs = refs[1:1 + n_in]
    w_refs = refs[1 + n_in:1 + 2 * n_in]
    o_ref = refs[1 + 2 * n_in]
    acc = res_ref[...]
    for a_ref, w_ref in zip(a_refs, w_refs):
        acc = acc + jnp.dot(a_ref[...], w_ref[...], preferred_element_type=F32)
    o_ref[...] = acc


def _out_proj(res, acts, ws, *, tm=512):
    t, d = res.shape
    n_in = len(acts)
    in_specs = [pl.BlockSpec((tm, d), lambda i: (i, 0))]
    in_specs += [pl.BlockSpec((tm, a.shape[1]), lambda i: (i, 0)) for a in acts]
    in_specs += [pl.BlockSpec(w.shape, lambda i: (0, 0)) for w in ws]
    return pl.pallas_call(
        functools.partial(_out_proj_kernel, n_in=n_in),
        grid=(t // tm,), in_specs=in_specs, out_specs=pl.BlockSpec((tm, d), lambda i: (i, 0)),
        out_shape=jax.ShapeDtypeStruct((t, d), F32),
        compiler_params=_cparams("parallel"), name="out_proj")(res, *acts, *ws)


def _retention_kernel(q_ref, k_ref, v_ref, z_ref, cos_ref, sin_ref, decay_ref, xi_ref, zeta_ref, gam_ref,
                      o_ref, state_ref, *, c, n_chunks):
    half = RET_QK_DIM // 2

    @pl.when(pl.program_id(2) == 0)
    def _():
        state_ref[...] = jnp.zeros_like(state_ref)

    cos = cos_ref[...]
    sin = sin_ref[...]

    def rotate(t):
        a = t[:, :half].astype(F32)
        b = t[:, half:].astype(F32)
        return jnp.concatenate([a * cos - b * sin, b * cos + a * sin], axis=-1)

    q = rotate(q_ref[0])
    k = rotate(k_ref[0]) * (RET_QK_DIM ** -0.5)
    decay = decay_ref[0]
    gamma_c = gam_ref[0, 0:1, 0:1]
    xi = xi_ref[0][:, 0:1]
    zeta = zeta_ref[0][:, 0:1]
    chunks = range(n_chunks)
    rows = [slice(ci * c, (ci + 1) * c) for ci in chunks]
    qb = [q[r].astype(BF16) for r in rows]
    kb = [k[r].astype(BF16) for r in rows]
    kz = [(k[r] * zeta).astype(BF16) for r in rows]
    vc = [v_ref[0, r, :] for r in rows]
    inner = [lax.dot_general(qb[i], kb[i], (((1,), (1,)), ((), ())), preferred_element_type=F32) * decay
             for i in chunks]
    kv = [lax.dot_general(kz[i], vc[i], (((0,), (0,)), ((), ())), preferred_element_type=F32) for i in chunks]
    intra = [jnp.dot(inner[i].astype(BF16), vc[i], preferred_element_type=F32) for i in chunks]
    states = [state_ref[...]]
    for i in chunks:
        states.append(states[i] * gamma_c + kv[i])
    state_ref[...] = states[-1]
    cross = [jnp.dot(qb[i], states[i].astype(BF16), preferred_element_type=F32) for i in chunks]
    for i in chunks:
        out = intra[i] + cross[i] * xi
        o = out * lax.rsqrt(jnp.mean(out * out, axis=-1, keepdims=True) + EPS)
        z = z_ref[0, rows[i], :].astype(F32)
        o_ref[0, rows[i], :] = (_silu(z) * o).astype(o_ref.dtype)


RET_KERNEL_CHUNK = 256


def _retention(p, bsz, s, *, blk=512):
    h = RET_HEADS
    c = RET_KERNEL_CHUNK
    half = RET_QK_DIM // 2
    pos = jnp.arange(s, dtype=F32)
    freq = 1.0 / (XPOS_THETA ** jnp.linspace(0.0, 1.0, half, dtype=F32))
    ang = pos[:, None] * freq[None, :]
    cos, sin = jnp.cos(ang), jnp.sin(ang)
    log_gamma = jnp.log(1.0 - 2.0 ** (-5.0 - jnp.arange(h, dtype=F32)))
    idx = jnp.arange(c, dtype=F32)
    rel = idx[:, None] - idx[None, :]
    decay = jnp.exp(jnp.where(rel >= 0, rel * log_gamma[:, None, None], NEG_INF))
    xi = jnp.broadcast_to(jnp.exp((idx + 1.0) * log_gamma[:, None])[..., None], (h, c, LANES))
    zeta = jnp.broadcast_to(jnp.exp((c - 1.0 - idx) * log_gamma[:, None])[..., None], (h, c, LANES))
    gam = jnp.broadcast_to(jnp.exp(c * log_gamma)[:, None, None], (h, 8, LANES))
    nq = RET_QK_W // RET_QK_DIM
    nv = RET_V_W // RET_V_DIM
    return pl.pallas_call(
        functools.partial(_retention_kernel, c=c, n_chunks=blk // c),
        grid=(bsz, h, s // blk),
        in_specs=[
            pl.BlockSpec((1, blk, RET_QK_DIM), lambda b, hh, i: (b, i, hh)),
            pl.BlockSpec((1, blk, RET_QK_DIM), lambda b, hh, i: (b, i, nq + hh)),
            pl.BlockSpec((1, blk, RET_V_DIM), lambda b, hh, i: (b, i, nv + hh)),
            pl.BlockSpec((1, blk, RET_V_DIM), lambda b, hh, i: (b, i, 2 * nv + hh)),
            pl.BlockSpec((blk, half), lambda b, hh, i: (i, 0)),
            pl.BlockSpec((blk, half), lambda b, hh, i: (i, 0)),
            pl.BlockSpec((1, c, c), lambda b, hh, i: (hh, 0, 0)),
            pl.BlockSpec((1, c, LANES), lambda b, hh, i: (hh, 0, 0)),
            pl.BlockSpec((1, c, LANES), lambda b, hh, i: (hh, 0, 0)),
            pl.BlockSpec((1, 8, LANES), lambda b, hh, i: (hh, 0, 0)),
        ],
        out_specs=pl.BlockSpec((1, blk, RET_V_DIM), lambda b, hh, i: (b, i, hh)),
        out_shape=jax.ShapeDtypeStruct((bsz, s, RET_V_W), BF16),
        scratch_shapes=[pltpu.VMEM((RET_QK_DIM, RET_V_DIM), F32)],
        compiler_params=_cparams("parallel", "parallel", "arbitrary"), name="retention",
    )(p, p, p, p, cos, sin, decay, xi, zeta, gam)


def _swa_kernel(sink_ref, q_ref, kc_ref, kp_ref, vc_ref, vp_ref, cos_ref, sin_ref, cosp_ref, sinp_ref,
                qg_ref, kg_ref, bd_ref, o_ref):
    w = WINDOW
    dh = SWA_HEAD_DIM
    grp = SWA_Q_HEADS // SWA_KV_HEADS
    n = pl.program_id(1)

    def head_rms(x, gain, bd):
        sq = x * x
        hi = sq.astype(BF16)
        lo = (sq - hi.astype(F32)).astype(BF16)
        ssum = jnp.dot(hi, bd, preferred_element_type=F32) + jnp.dot(lo, bd, preferred_element_type=F32)
        return x * lax.rsqrt(ssum * (1.0 / dh) + EPS) * gain

    def rope(x, cos, sin):
        width = x.shape[-1]
        lane = lax.broadcasted_iota(jnp.int32, x.shape, 1) % dh
        partner = jnp.where(lane < ROT_DIM // 2, pltpu.roll(x, width - ROT_DIM // 2, 1),
                            pltpu.roll(x, ROT_DIM // 2, 1))
        return x * cos + partner * sin

    cos, sin = cos_ref[...], sin_ref[...]
    cosq = jnp.concatenate([cos] * (SWA_Q_W // LANES), axis=1)
    sinq = jnp.concatenate([sin] * (SWA_Q_W // LANES), axis=1)
    q = rope(head_rms(q_ref[0].astype(F32), qg_ref[...], bd_ref[...]), cosq, sinq)
    q = (q * (dh ** -0.5)).astype(BF16)
    bdk = bd_ref[0:SWA_KV_W, 0:SWA_KV_W]
    kcur = rope(head_rms(kc_ref[0].astype(F32), kg_ref[...], bdk), cos, sin)
    kprev = rope(head_rms(kp_ref[0].astype(F32), kg_ref[...], bdk), cosp_ref[...], sinp_ref[...])
    kband = jnp.concatenate([kprev, kcur], axis=0).astype(BF16)
    vband = jnp.concatenate([vp_ref[0], vc_ref[0]], axis=0)

    i = lax.broadcasted_iota(jnp.int32, (w, 2 * w), 0)
    j = lax.broadcasted_iota(jnp.int32, (w, 2 * w), 1)
    mask = (j > i) & (j <= i + w) & ((j >= w) | (n > 0))
    heads = range(SWA_Q_HEADS)
    kh = [kband[:, hk * dh:(hk + 1) * dh] for hk in range(SWA_KV_HEADS)]
    vh = [vband[:, hk * dh:(hk + 1) * dh] for hk in range(SWA_KV_HEADS)]
    sc = [lax.dot_general(q[:, hq * dh:(hq + 1) * dh], kh[hq // grp], (((1,), (1,)), ((), ())),
                          preferred_element_type=F32) for hq in heads]
    sc = [jnp.where(mask, x, NEG_INF) for x in sc]
    m = [jnp.maximum(jnp.max(sc[hq], axis=-1, keepdims=True), sink_ref[hq]) for hq in heads]
    p = [jnp.exp(sc[hq] - m[hq]) for hq in heads]
    denom = [jnp.sum(p[hq], axis=-1, keepdims=True) + jnp.exp(sink_ref[hq] - m[hq]) for hq in heads]
    pv = [jnp.dot(p[hq].astype(BF16), vh[hq // grp], preferred_element_type=F32) for hq in heads]
    o_ref[0] = jnp.concatenate([pv[hq] * (1.0 / denom[hq]) for hq in heads], axis=-1).astype(o_ref.dtype)


def _rope_tables(s):
    half = ROT_DIM // 2
    pos = jnp.arange(s, dtype=F32)
    inv_freq = 1.0 / (ROPE_THETA ** (jnp.arange(half, dtype=F32) * 2.0 / ROT_DIM))
    ang = pos[:, None] * inv_freq[None, :]
    cos, sin = jnp.cos(ang), jnp.sin(ang)
    ones = jnp.ones((s, SWA_HEAD_DIM - ROT_DIM), F32)
    cos_h = jnp.concatenate([cos, cos, ones], axis=1)
    sin_h = jnp.concatenate([-sin, sin, 0.0 * ones], axis=1)
    reps = LANES // SWA_HEAD_DIM
    return jnp.tile(cos_h, (1, reps)), jnp.tile(sin_h, (1, reps))


def _swa(p, sinks, q_gain, k_gain, bsz, s):
    w = WINDOW
    cos, sin = _rope_tables(s)
    qg = jnp.tile(q_gain.astype(F32), SWA_Q_HEADS).reshape(1, SWA_Q_W)
    kg = jnp.tile(k_gain.astype(F32), SWA_KV_HEADS).reshape(1, SWA_KV_W)
    seg = np.arange(SWA_Q_W) // SWA_HEAD_DIM
    bd = jnp.asarray(seg[:, None] == seg[None, :], dtype=BF16)
    qblk, kblk, vblk = EVEN_COL_Q // SWA_Q_W, EVEN_COL_K // SWA_KV_W, EVEN_COL_V // SWA_KV_W
    return pl.pallas_call(
        _swa_kernel,
        grid=(bsz, s // w),
        in_specs=[
            pl.BlockSpec(memory_space=pltpu.SMEM),
            pl.BlockSpec((1, w, SWA_Q_W), lambda b, i: (b, i, qblk)),
            pl.BlockSpec((1, w, SWA_KV_W), lambda b, i: (b, i, kblk)),
            pl.BlockSpec((1, w, SWA_KV_W), lambda b, i: (b, jnp.maximum(i - 1, 0), kblk)),
            pl.BlockSpec((1, w, SWA_KV_W), lambda b, i: (b, i, vblk)),
            pl.BlockSpec((1, w, SWA_KV_W), lambda b, i: (b, jnp.maximum(i - 1, 0), vblk)),
            pl.BlockSpec((w, LANES), lambda b, i: (i, 0)),
            pl.BlockSpec((w, LANES), lambda b, i: (i, 0)),
            pl.BlockSpec((w, LANES), lambda b, i: (jnp.maximum(i - 1, 0), 0)),
            pl.BlockSpec((w, LANES), lambda b, i: (jnp.maximum(i - 1, 0), 0)),
            pl.BlockSpec((1, SWA_Q_W), lambda b, i: (0, 0)),
            pl.BlockSpec((1, SWA_KV_W), lambda b, i: (0, 0)),
            pl.BlockSpec((SWA_Q_W, SWA_Q_W), lambda b, i: (0, 0)),
        ],
        out_specs=pl.BlockSpec((1, w, SWA_Q_W), lambda b, i: (b, i, 0)),
        out_shape=jax.ShapeDtypeStruct((bsz, s, SWA_Q_W), BF16),
        compiler_params=_cparams("parallel", "parallel"), name="swa",
    )(sinks.astype(F32), p, p, p, p, p, cos, sin, cos, sin, qg, kg, bd)


def _softplus(x):
    return jnp.maximum(x, 0.0) + jnp.log1p(jnp.exp(-jnp.abs(x)))


def _split3(x):
    x1 = x.astype(BF16)
    r1 = x - x1.astype(F32)
    x2 = r1.astype(BF16)
    return x1, x2, (r1 - x2.astype(F32)).astype(BF16)


def _gdn_kernel(q_ref, k_ref, v_ref, z_ref, gate_ref, alog_ref, dt_ref, gain_ref, o_ref, state_ref, *, blk):
    c = GDN_CHUNK
    dk = GDN_HEAD_DIM
    nh = GDN_HEADS
    sub = LANES
    nt = (((1,), (1,)), ((), ()))

    @pl.when(pl.program_id(1) == 0)
    def _():
        state_ref[...] = jnp.zeros_like(state_ref)

    gates = gate_ref[0]
    g_all = -jnp.exp(alog_ref[...]) * _softplus(gates + dt_ref[...])
    beta_all = 1.0 / (1.0 + jnp.exp(-gates))
    ii = lax.broadcasted_iota(jnp.int32, (sub, sub), 0)
    jj = lax.broadcasted_iota(jnp.int32, (sub, sub), 1)
    same = (ii // c) == (jj // c)
    incl = same & (ii >= jj)
    strict = same & (ii > jj)
    tri = jnp.where(incl, 1.0, 0.0).astype(BF16)
    eye = jnp.where(ii == jj, 1.0, 0.0).astype(F32)
    er = lax.broadcasted_iota(jnp.int32, (8, LANES), 0)
    ec = lax.broadcasted_iota(jnp.int32, (8, LANES), 1)
    eye8 = jnp.where(er == ec, 1.0, 0.0).astype(BF16)
    g_parts = _split3(g_all)
    g_rows = sum(lax.dot_general(eye8, gp, nt, preferred_element_type=F32) for gp in g_parts)
    gr_parts = _split3(g_rows)
    gc_cols, gc_rows = [], []
    for sb in range(blk // sub):
        rows = slice(sb * sub, (sb + 1) * sub)
        gc_cols.append(sum(jnp.dot(tri, gp[rows], preferred_element_type=F32) for gp in g_parts))
        gc_rows.append(sum(lax.dot_general(gp[:, rows], tri, nt, preferred_element_type=F32) for gp in gr_parts))

    n_sub = blk // sub
    units = [(hd, sb) for hd in range(nh) for sb in range(n_sub)]
    dot = functools.partial(jnp.dot, preferred_element_type=F32)

    def dot_nt(x, y):
        return lax.dot_general(x, y, nt, preferred_element_type=F32)

    qs, ks, vs, betas = [], [], [], []
    for hd in range(nh):
        cols = slice(hd * dk, (hd + 1) * dk)
        q = q_ref[0, :, cols].astype(F32)
        k = k_ref[0, :, cols].astype(F32)
        qs.append(q * lax.rsqrt(jnp.sum(q * q, axis=-1, keepdims=True) + EPS) * (dk ** -0.5))
        ks.append(k * lax.rsqrt(jnp.sum(k * k, axis=-1, keepdims=True) + EPS))
        vs.append(v_ref[0, :, cols].astype(F32))
        betas.append(beta_all[:, nh + hd:nh + hd + 1])

    def rows_of(sb):
        return slice(sb * sub, (sb + 1) * sub)

    gcb = [jnp.broadcast_to(gc_cols[sb][:, hd:hd + 1], (sub, LANES)) for hd, sb in units]
    decay = [jnp.exp(jnp.where(incl, gcb[u] - gc_rows[sb][hd:hd + 1, :], NEG_INF)) for u, (hd, sb) in enumerate(units)]
    eg = [jnp.exp(g) for g in gcb]
    q_u = [qs[hd][rows_of(sb)] for hd, sb in units]
    k_u = [ks[hd][rows_of(sb)] for hd, sb in units]
    kb = [k_u[u] * betas[hd][rows_of(sb)] for u, (hd, sb) in enumerate(units)]
    kbf = [x.astype(BF16) for x in k_u]
    a = [dot_nt(kb[u].astype(BF16), kbf[u]) for u in range(len(units))]
    m = [jnp.where(strict, -(a[u] * decay[u]), 0.0) for u in range(len(units))]
    r = [eye + x for x in m]
    mb = [x.astype(BF16) for x in m]
    pw = [dot(x, x) for x in mb]
    for _ in range(int(np.log2(c)) - 2):
        pb = [x.astype(BF16) for x in pw]
        both = [dot(jnp.concatenate([r[u].astype(BF16), pb[u]], axis=0), pb[u]) for u in range(len(units))]
        r = [r[u] + both[u][:sub] for u in range(len(units))]
        pw = [x[sub:] for x in both]
    r = [r[u] + dot(r[u].astype(BF16), pw[u].astype(BF16)) for u in range(len(units))]
    rhs = [jnp.concatenate([kb[u] * eg[u], vs[hd][rows_of(sb)] * betas[hd][rows_of(sb)]], axis=1).astype(BF16)
           for u, (hd, sb) in enumerate(units)]
    wu = [dot(r[u].astype(BF16), rhs[u]).astype(BF16) for u in range(len(units))]
    qk = [dot_nt(q_u[u].astype(BF16), kbf[u]) for u in range(len(units))]
    iwu = [dot((qk[u] * decay[u]).astype(BF16), wu[u]) for u in range(len(units))]
    qe = [(q_u[u] * eg[u] - iwu[u][:, :dk]).astype(BF16) for u in range(len(units))]
    per_chunk = sub // c
    kwu, g_last = {}, {}
    for u, (hd, sb) in enumerate(units):
        for ci in range(per_chunk):
            crow = slice(ci * c, (ci + 1) * c)
            gl = gcb[u][(ci + 1) * c - 1:(ci + 1) * c, :]
            k_dec = (k_u[u][crow] * jnp.exp(gl - gcb[u][crow])).astype(BF16)
            g_last[hd, sb * per_chunk + ci] = jnp.exp(gl)
            kwu[hd, sb * per_chunk + ci] = lax.dot_general(k_dec, wu[u][crow], (((0,), (0,)), ((), ())),
                                                           preferred_element_type=F32)
    outs = [[] for _ in range(nh)]
    for ch in range(blk // c):
        u_sb, ci = ch // per_chunk, ch % per_chunk
        crow = slice(ci * c, (ci + 1) * c)
        states = [state_ref[hd] for hd in range(nh)]
        sbf = [x.astype(BF16) for x in states]
        for hd in range(nh):
            u = hd * n_sub + u_sb
            outs[hd].append(dot(qe[u][crow], sbf[hd]) + iwu[u][crow, dk:])
        corr = [dot(kwu[hd, ch][:, :dk].astype(BF16), sbf[hd]) for hd in range(nh)]
        for hd in range(nh):
            state_ref[hd] = states[hd] * g_last[hd, ch] + (kwu[hd, ch][:, dk:] - corr[hd])
    for hd in range(nh):
        out = jnp.concatenate(outs[hd], axis=0)
        o = out * lax.rsqrt(jnp.mean(out * out, axis=-1, keepdims=True) + EPS) * gain_ref[...]
        z = z_ref[0, :, hd * dk:(hd + 1) * dk].astype(F32)
        o_ref[0, :, hd * dk:(hd + 1) * dk] = (o * _silu(z)).astype(o_ref.dtype)


def _gdn(p, pg, a_log, dt_bias, gdn_gain, bsz, s, *, blk=256):
    dk = GDN_HEAD_DIM
    assert EVEN_COL_GDN % GDN_W == 0 and EVEN_COL_Z % GDN_W == 0
    lane_pad = (0, LANES - GDN_HEADS)
    alog = jnp.pad(a_log.astype(F32), lane_pad).reshape(1, LANES)
    dt = jnp.pad(dt_bias.astype(F32), lane_pad).reshape(1, LANES)
    return pl.pallas_call(
        functools.partial(_gdn_kernel, blk=blk),
        grid=(bsz, s // blk),
        in_specs=[pl.BlockSpec((1, blk, GDN_W), lambda b, i: (b, i, EVEN_COL_GDN // GDN_W)),
                  pl.BlockSpec((1, blk, GDN_W), lambda b, i: (b, i, EVEN_COL_GDN // GDN_W + 1)),
                  pl.BlockSpec((1, blk, GDN_W), lambda b, i: (b, i, EVEN_COL_GDN // GDN_W + 2)),
                  pl.BlockSpec((1, blk, GDN_W), lambda b, i: (b, i, EVEN_COL_Z // GDN_W)),
                  pl.BlockSpec((1, blk, LANES), lambda b, i: (b, i, 0)),
                  pl.BlockSpec((1, LANES), lambda b, i: (0, 0)),
                  pl.BlockSpec((1, LANES), lambda b, i: (0, 0)),
                  pl.BlockSpec((1, dk), lambda b, i: (0, 0))],
        out_specs=pl.BlockSpec((1, blk, GDN_W), lambda b, i: (b, i, 0)),
        out_shape=jax.ShapeDtypeStruct((bsz, s, GDN_W), BF16),
        scratch_shapes=[pltpu.VMEM((GDN_HEADS, dk, dk), F32)],
        compiler_params=_cparams("parallel", "arbitrary"), name="gdn",
    )(p, p, p, p, pg, alog, dt, gdn_gain.astype(F32).reshape(1, dk))


MOE_BM = 256
ROUTER_ROWS = LANES
EXPERT_ROW0 = 8


SLAB = 8
DMA_UNROLL = 4


def _store_slabs(ref, val, rows):
    for sl in range(SLAB):
        ref[pl.ds(sl, rows, stride=SLAB), :] = val[:, sl * LANES:(sl + 1) * LANES]


def _load_slabs(ref, rows):
    return jnp.concatenate([ref[pl.ds(sl, rows, stride=SLAB), :] for sl in range(SLAB)], axis=1)


def _slab(ref, row):
    return ref.at[pl.ds(pl.multiple_of(row * SLAB, SLAB), SLAB), :]


def _router_kernel(x_ref, g_ref, wr_ref, wrlo_ref, br_ref, h_ref, eid_ref, gate_ref):
    x = x_ref[...]
    h = x * lax.rsqrt(jnp.mean(x * x, axis=-1, keepdims=True) + EPS) * g_ref[...]
    tm = x.shape[0]
    _store_slabs(h_ref, h, tm)
    h_hi = h.astype(BF16)
    h_lo = (h - h_hi.astype(F32)).astype(BF16)
    nt = (((1,), (1,)), ((), ()))
    logits = (lax.dot_general(wr_ref[...], h_hi, nt, preferred_element_type=F32)
              + lax.dot_general(wr_ref[...], h_lo, nt, preferred_element_type=F32)
              + lax.dot_general(wrlo_ref[...], h_hi, nt, preferred_element_type=F32)) + br_ref[...]
    row = lax.broadcasted_iota(jnp.int32, (8, tm), 0)
    gl = jnp.where(row < N_GROUPS, logits[0:8, :], NEG_INF)
    gmax = jnp.max(gl, axis=0, keepdims=True)
    grp_p = 1.0 / jnp.sum(jnp.exp(gl - gmax), axis=0, keepdims=True)
    grp_idx = jnp.min(jnp.where(gl == gmax, row, 8), axis=0, keepdims=True)
    el = jnp.zeros((8, tm), F32)
    for g in range(N_GROUPS):
        lo = EXPERT_ROW0 + g * EXPERTS_PER_GROUP
        el = jnp.where(grp_idx == g, logits[lo:lo + EXPERTS_PER_GROUP, :], el)
    m1 = jnp.max(el, axis=0, keepdims=True)
    i1 = jnp.min(jnp.where(el == m1, row, 8), axis=0, keepdims=True)
    el2 = jnp.where(row == i1, NEG_INF, el)
    m2 = jnp.max(el2, axis=0, keepdims=True)
    i2 = jnp.min(jnp.where(el2 == m2, row, 8), axis=0, keepdims=True)
    z = jnp.sum(jnp.exp(el - m1), axis=0, keepdims=True)
    p1 = 1.0 / z
    p2 = jnp.exp(m2 - m1) / z
    scale = grp_p / (p1 + p2)
    e1 = grp_idx * EXPERTS_PER_GROUP + i1
    e2 = grp_idx * EXPERTS_PER_GROUP + i2
    eid_ref[...] = jnp.where(row == 0, e1, jnp.where(row == 1, e2, 0))
    gate_ref[...] = jnp.where(row == 0, p1 * scale, jnp.where(row == 1, p2 * scale, 0.0))


def _moe_router(x2, gain, w_rg, b_rg, w_re, b_re, *, tm=512):
    t, d = x2.shape
    wr = jnp.zeros((ROUTER_ROWS, d), F32)
    wr = wr.at[0:N_GROUPS].set(w_rg.astype(F32).T).at[EXPERT_ROW0:EXPERT_ROW0 + N_EXPERTS].set(w_re.astype(F32).T)
    br = jnp.zeros((ROUTER_ROWS, 1), F32)
    br = br.at[0:N_GROUPS, 0].set(b_rg.astype(F32)).at[EXPERT_ROW0:EXPERT_ROW0 + N_EXPERTS, 0].set(b_re.astype(F32))
    wr_hi = wr.astype(BF16)
    wr_lo = (wr - wr_hi.astype(F32)).astype(BF16)
    return pl.pallas_call(
        _router_kernel, grid=(t // tm,),
        in_specs=[pl.BlockSpec((tm, d), lambda i: (i, 0)),
                  pl.BlockSpec((1, d), lambda i: (0, 0)),
                  pl.BlockSpec((ROUTER_ROWS, d), lambda i: (0, 0)),
                  pl.BlockSpec((ROUTER_ROWS, d), lambda i: (0, 0)),
                  pl.BlockSpec((ROUTER_ROWS, 1), lambda i: (0, 0))],
        out_specs=[pl.BlockSpec((tm * SLAB, LANES), lambda i: (i, 0)),
                   pl.BlockSpec((8, tm), lambda i: (0, i)),
                   pl.BlockSpec((8, tm), lambda i: (0, i))],
        out_shape=[jax.ShapeDtypeStruct((t * SLAB, LANES), F32),
                   jax.ShapeDtypeStruct((8, t), jnp.int32),
                   jax.ShapeDtypeStruct((8, t), F32)],
        compiler_params=_cparams("parallel"), name="moe_router")(x2, gain.reshape(1, d), wr_hi, wr_lo, br)


def _rank_kernel(eid_ref, dest_ref, meta_ref, cnt_ref, off_ref, *, bm):
    ph = pl.program_id(0)
    i = pl.program_id(1)
    tm = eid_ref.shape[1]
    erow = lax.broadcasted_iota(jnp.int32, (N_EXPERTS, tm), 0)
    oh0 = jnp.where(erow == eid_ref[0:1, :], 1.0, 0.0).astype(F32)
    oh1 = jnp.where(erow == eid_ref[1:2, :], 1.0, 0.0).astype(F32)
    oh = oh0 + oh1
    tile_cnt = jnp.sum(oh, axis=1, keepdims=True)

    @pl.when((ph == 0) & (i == 0))
    def _():
        cnt_ref[...] = jnp.zeros_like(cnt_ref)

    @pl.when(ph == 0)
    def _():
        cnt_ref[...] += tile_cnt

    @pl.when((ph == 1) & (i == 0))
    def _():
        cnt = cnt_ref[...]
        padded = jnp.ceil(cnt * (1.0 / bm)) * bm
        er = lax.broadcasted_iota(jnp.int32, (N_EXPERTS, N_EXPERTS), 0)
        ec = lax.broadcasted_iota(jnp.int32, (N_EXPERTS, N_EXPERTS), 1)
        lower = jnp.where(ec < er, 1.0, 0.0).astype(F32)
        start = jnp.dot(lower, padded, preferred_element_type=F32, precision=lax.Precision.HIGHEST)
        off_ref[...] = start
        end = start + padded
        nb = meta_ref.shape[1]
        blk_row = lax.broadcasted_iota(jnp.int32, (N_EXPERTS, nb), 1).astype(F32) * bm
        endb = jnp.concatenate([end] * (nb // LANES), axis=1)
        be = jnp.sum(jnp.where(endb <= blk_row, 1.0, 0.0), axis=0, keepdims=True)
        be = jnp.minimum(be, N_EXPERTS - 1.0)
        used = jnp.concatenate([end[N_EXPERTS - 1:N_EXPERTS, :]] * (nb // LANES), axis=1) * (1.0 / bm)
        elane = lax.broadcasted_iota(jnp.int32, (N_EXPERTS, nb), 1)
        erow2 = lax.broadcasted_iota(jnp.int32, (N_EXPERTS, nb), 0)
        ends = jnp.sum(jnp.where(elane == erow2, endb, 0.0), axis=0, keepdims=True)
        mrow = lax.broadcasted_iota(jnp.int32, (8, nb), 0)
        meta_ref[...] = jnp.where(mrow == 0, be, jnp.where(mrow == 1, used, jnp.where(mrow == 2, ends, 0.0))
                                  ).astype(jnp.int32)
        cnt_ref[...] = jnp.zeros_like(cnt_ref)

    @pl.when(ph == 1)
    def _():
        kr = lax.broadcasted_iota(jnp.int32, (tm, tm), 0)
        kc = lax.broadcasted_iota(jnp.int32, (tm, tm), 1)
        su = jnp.where(kr < kc, 1.0, 0.0).astype(BF16)
        csum = jnp.dot(oh.astype(BF16), su, preferred_element_type=F32)
        pos = csum + (off_ref[...] + cnt_ref[...])[:, 0:1]
        d0 = jnp.sum(oh0 * pos, axis=0, keepdims=True)
        d1 = jnp.sum(oh1 * pos, axis=0, keepdims=True)
        row = lax.broadcasted_iota(jnp.int32, (8, tm), 0)
        dest_ref[...] = jnp.where(row == 0, d0, jnp.where(row == 1, d1, 0.0)).astype(jnp.int32)
        cnt_ref[...] += tile_cnt


def _moe_rank(eid, n_blocks, *, bm, tm=1024):
    t = eid.shape[1]
    nb = -(-n_blocks // LANES) * LANES
    return pl.pallas_call(
        functools.partial(_rank_kernel, bm=bm), grid=(2, t // tm),
        in_specs=[pl.BlockSpec((8, tm), lambda p, i: (0, i))],
        out_specs=[pl.BlockSpec((8, tm), lambda p, i: (0, i * p)),
                   pl.BlockSpec((8, nb), lambda p, i: (0, 0))],
        out_shape=[jax.ShapeDtypeStruct((8, t), jnp.int32), jax.ShapeDtypeStruct((8, nb), jnp.int32)],
        scratch_shapes=[pltpu.VMEM((N_EXPERTS, LANES), F32), pltpu.VMEM((N_EXPERTS, LANES), F32)],
        compiler_params=_cparams("arbitrary", "arbitrary"), name="moe_rank")(eid)


def _dispatch_kernel(dest_ref, ends_ref, h_ref, rows_hbm, zero_ref, sem, *, tm, bm):
    @pl.when(pl.program_id(0) == 0)
    def _():
        zero_ref[...] = jnp.zeros_like(zero_ref)

        def tail(e):
            start = ends_ref[0, e] - bm
            return pltpu.make_async_copy(zero_ref, rows_hbm.at[pl.ds(pl.multiple_of(start * SLAB, SLAB), bm * SLAB), :],
                                         sem)

        def nonempty(e):
            return ends_ref[0, e] > jnp.where(e > 0, ends_ref[0, jnp.maximum(e - 1, 0)], 0)

        def fill(e, carry):
            @pl.when(nonempty(e))
            def _():
                tail(e).start()
            return carry

        def drain(e, carry):
            @pl.when(nonempty(e))
            def _():
                tail(e).wait()
            return carry

        lax.fori_loop(0, N_EXPERTS, fill, 0)
        lax.fori_loop(0, N_EXPERTS, drain, 0)

        def unused(b):
            return pltpu.make_async_copy(
                zero_ref, rows_hbm.at[pl.ds(pl.multiple_of(b * (bm * SLAB), SLAB), bm * SLAB), :], sem)

        def fill_unused(b, carry):
            unused(b).start()
            return carry

        def drain_unused(b, carry):
            unused(b).wait()
            return carry

        first_unused = ends_ref[0, N_EXPERTS - 1] // bm
        n_blocks = rows_hbm.shape[0] // (bm * SLAB)
        lax.fori_loop(first_unused, n_blocks, fill_unused, 0)
        lax.fori_loop(first_unused, n_blocks, drain_unused, 0)

    def issue(g, carry):
        for u in range(DMA_UNROLL):
            tok = g * DMA_UNROLL + u
            for k in range(TOP_K):
                pltpu.make_async_copy(_slab(h_ref, tok), _slab(rows_hbm, dest_ref[0, 0, k * tm + tok]),
                                      sem).start(priority=k)
        return carry

    lax.fori_loop(0, tm // DMA_UNROLL, issue, 0)
    for _ in range(TOP_K):
        pltpu.make_async_copy(h_ref, rows_hbm.at[pl.ds(0, tm * SLAB), :], sem).wait()


def _moe_dispatch(dest_tiles, expert_ends, h_slabs, n_rows, *, tm, bm):
    t = h_slabs.shape[0] // SLAB
    return pl.pallas_call(
        functools.partial(_dispatch_kernel, tm=tm, bm=bm), grid=(t // tm,),
        in_specs=[pl.BlockSpec((1, 1, TOP_K * tm), lambda i: (i, 0, 0), memory_space=pltpu.SMEM),
                  pl.BlockSpec(memory_space=pltpu.SMEM),
                  pl.BlockSpec((tm * SLAB, LANES), lambda i: (i, 0))],
        out_specs=pl.BlockSpec(memory_space=pl.ANY),
        out_shape=jax.ShapeDtypeStruct((n_rows * SLAB, LANES), F32),
        scratch_shapes=[pltpu.VMEM((bm * SLAB, LANES), F32), pltpu.SemaphoreType.DMA(())],
        compiler_params=pltpu.CompilerParams(dimension_semantics=("arbitrary",), has_side_effects=True,
                                             vmem_limit_bytes=VMEM_LIMIT),
        name="moe_dispatch")(dest_tiles, expert_ends, h_slabs)


def _experts_kernel(be_ref, used_ref, x_ref, wg_ref, wu_ref, wd_ref, y_ref, wgb_ref, wub_ref, wdb_ref):
    b = pl.program_id(0)
    prev = be_ref[jnp.maximum(b - 1, 0)]

    @pl.when((b == 0) | (be_ref[b] != prev))
    def _():
        wgb_ref[...] = wg_ref[0].astype(BF16)
        wub_ref[...] = wu_ref[0].astype(BF16)
        wdb_ref[...] = wd_ref[0].astype(BF16)

    bm = x_ref.shape[0] // SLAB

    @pl.when(b < used_ref[0])
    def _():
        x = _load_slabs(x_ref, bm).astype(BF16)
        gate = jnp.dot(x, wgb_ref[...], preferred_element_type=F32)
        up = jnp.dot(x, wub_ref[...], preferred_element_type=F32)
        hid = (_silu(gate) * up).astype(BF16)
        _store_slabs(y_ref, jnp.dot(hid, wdb_ref[...], preferred_element_type=F32), bm)

    @pl.when(b >= used_ref[0])
    def _():
        y_ref[...] = jnp.zeros_like(y_ref)


def _moe_experts(block_expert, n_used, x_rows, w_gate, w_up, w_down, *, bm):
    n_rows = x_rows.shape[0] // SLAB
    d, de = w_gate.shape[1], w_gate.shape[2]
    grid_spec = pltpu.PrefetchScalarGridSpec(
        num_scalar_prefetch=2, grid=(n_rows // bm,),
        in_specs=[pl.BlockSpec((bm * SLAB, LANES),
                               lambda b, be, nu: (jnp.minimum(b, jnp.maximum(nu[0] - 1, 0)), 0)),
                  pl.BlockSpec((1, d, de), lambda b, be, nu: (be[b], 0, 0)),
                  pl.BlockSpec((1, d, de), lambda b, be, nu: (be[b], 0, 0)),
                  pl.BlockSpec((1, de, d), lambda b, be, nu: (be[b], 0, 0))],
        out_specs=pl.BlockSpec((bm * SLAB, LANES), lambda b, be, nu: (b, 0)),
        scratch_shapes=[pltpu.VMEM((d, de), BF16), pltpu.VMEM((d, de), BF16), pltpu.VMEM((de, d), BF16)])
    return pl.pallas_call(
        _experts_kernel, grid_spec=grid_spec, out_shape=jax.ShapeDtypeStruct((n_rows * SLAB, LANES), F32),
        compiler_params=_cparams("arbitrary"), name="moe_experts",
    )(block_expert, n_used, x_rows, w_gate, w_up, w_down)


def _combine_kernel(dest_ref, x_ref, gate_ref, y_hbm, o_ref, buf_ref, sem, *, tm):
    def issue(g, carry):
        for u in range(DMA_UNROLL):
            tok = g * DMA_UNROLL + u
            for k in range(TOP_K):
                pltpu.make_async_copy(_slab(y_hbm, dest_ref[0, 0, k * tm + tok]), _slab(buf_ref.at[k], tok),
                                      sem).start(priority=k)
        return carry

    lax.fori_loop(0, tm // DMA_UNROLL, issue, 0)
    er = lax.broadcasted_iota(jnp.int32, (8, LANES), 0)
    ec = lax.broadcasted_iota(jnp.int32, (8, LANES), 1)
    eye = jnp.where(er == ec, 1.0, 0.0).astype(F32)
    gcol = lax.dot_general(gate_ref[...], eye, (((0,), (0,)), ((), ())), preferred_element_type=F32,
                           precision=lax.Precision.HIGHEST)
    for k in range(TOP_K):
        pltpu.make_async_copy(y_hbm.at[pl.ds(0, tm * SLAB), :], buf_ref.at[k], sem).wait()
    o_ref[...] = (x_ref[...] + gcol[:, 0:1] * _load_slabs(buf_ref.at[0], tm)
                  + gcol[:, 1:2] * _load_slabs(buf_ref.at[1], tm))


def _moe_combine(dest_tiles, x2, gates, y_rows, *, tm):
    t, d = x2.shape
    return pl.pallas_call(
        functools.partial(_combine_kernel, tm=tm), grid=(t // tm,),
        in_specs=[pl.BlockSpec((1, 1, TOP_K * tm), lambda i: (i, 0, 0), memory_space=pltpu.SMEM),
                  pl.BlockSpec((tm, d), lambda i: (i, 0)),
                  pl.BlockSpec((8, tm), lambda i: (0, i)),
                  pl.BlockSpec(memory_space=pl.ANY)],
        out_specs=pl.BlockSpec((tm, d), lambda i: (i, 0)),
        out_shape=jax.ShapeDtypeStruct((t, d), F32),
        scratch_shapes=[pltpu.VMEM((TOP_K, tm * SLAB, LANES), F32), pltpu.SemaphoreType.DMA(())],
        compiler_params=_cparams("arbitrary"), name="moe_combine")(dest_tiles, x2, gates, y_rows)


def _moe(x2, gain, w_rg, b_rg, w_re, b_re, w_gate, w_up, w_down, *, tm_dma=512):
    t, d = x2.shape
    bm = MOE_BM
    n_rows = t * TOP_K + N_EXPERTS * bm
    n_blocks = n_rows // bm
    h, eid, gates = _moe_router(x2, gain, w_rg, b_rg, w_re, b_re)
    dest, meta = _moe_rank(eid, n_blocks, bm=bm)
    dest_tiles = dest[0:TOP_K].reshape(TOP_K, t // tm_dma, tm_dma).transpose(1, 0, 2).reshape(
        t // tm_dma, 1, TOP_K * tm_dma)
    x_rows = _moe_dispatch(dest_tiles, meta[2:3, :N_EXPERTS], h, n_rows, tm=tm_dma, bm=bm)
    y_rows = _moe_experts(meta[0, :n_blocks], meta[1, 0:1], x_rows, w_gate, w_up, w_down, bm=bm)
    return _moe_combine(dest_tiles, x2, gates, y_rows, tm=tm_dma)


def _even_odd_perm():
    half = RET_QK_DIM // 2
    src = np.concatenate([2 * np.arange(half), 2 * np.arange(half) + 1])
    perm = np.zeros((RET_QK_DIM, RET_QK_DIM), np.float32)
    perm[src, np.arange(RET_QK_DIM)] = 1.0
    return jnp.asarray(perm, dtype=BF16)


def kernel(x, ln_mix_even, w_in_even, conv_w_even, a_log_even, dt_bias_even, q_gain_even, k_gain_even, sinks_even, gdn_gain_even, w_out_even, ln_mix_odd, w_in_odd, w_out_odd, ln_ffn, w_router_group, b_router_group, w_router_expert, b_router_expert, w_gate, w_up, w_down):
    bsz, s, d = x.shape
    t = bsz * s
    x2 = x.reshape(t, d).astype(F32)
    depth = ln_ffn.shape[0]
    for layer in range(depth):
        i = layer // 2
        if layer % 2 == 0:
            w_in = w_in_even[i]
            swa_end = SWA_Q_W + 2 * SWA_KV_W
            w_main = jnp.concatenate([w_in[:, :SWA_Q_W], w_in[:, swa_end:EVEN_MAIN], w_in[:, SWA_Q_W:swa_end]],
                                     axis=1).astype(BF16)
            w_gates = jnp.pad(w_in[:, EVEN_MAIN:].astype(F32), ((0, 0), (0, LANES - 2 * GDN_HEADS)))
            p, pg = _norm_proj(x2, ln_mix_even[i], w_main, w_gates, conv_w=conv_w_even[i],
                               conv_cols=(EVEN_COL_GDN, EVEN_COL_Z), seq_len=s)
            p = p.reshape(bsz, s, EVEN_MAIN)
            y_a = _swa(p, sinks_even[i], q_gain_even[i], k_gain_even[i], bsz, s)
            y_b = _gdn(p, pg.reshape(bsz, s, LANES), a_log_even[i], dt_bias_even[i], gdn_gain_even[i], bsz, s)
            w_out = w_out_even[i].astype(BF16)
            x2 = _out_proj(x2, [y_a.reshape(t, SWA_Q_W), y_b.reshape(t, GDN_W)], [w_out[:SWA_Q_W], w_out[SWA_Q_W:]])
        else:
            p = _norm_proj(x2, ln_mix_odd[i], w_in_odd[i].astype(BF16), perm=_even_odd_perm(),
                           n_perm=2 * RET_HEADS).reshape(bsz, s, ODD_IN)
            y = _retention(p, bsz, s)
            x2 = _out_proj(x2, [y.reshape(t, RET_V_W)], [w_out_odd[i].astype(BF16)])
        x2 = _moe(x2, ln_ffn[layer], w_router_group[layer], b_router_group[layer], w_router_expert[layer],
                  b_router_expert[layer], w_gate[layer], w_up[layer], w_down[layer])
    return x2.reshape(bsz, s, d).astype(x.dtype)
```

```python
import functools

import numpy as np
import jax
import jax.numpy as jnp
from jax import lax
from jax.experimental import pallas as pl
from jax.experimental.pallas import tpu as pltpu

F32 = jnp.float32
BF16 = jnp.bfloat16

D_MODEL = 1024
SWA_Q_HEADS = 8
SWA_KV_HEADS = 2
SWA_HEAD_DIM = 64
WINDOW = 128
ROT_DIM = SWA_HEAD_DIM // 4
ROPE_THETA = 500000.0
GDN_HEADS = 4
GDN_HEAD_DIM = 128
CONV_K = 4
GDN_CHUNK = 64
RET_HEADS = 4
RET_QK_DIM = 256
RET_V_DIM = 512
RET_CHUNK = 128
XPOS_THETA = 10000.0
N_GROUPS = 4
EXPERTS_PER_GROUP = 8
N_EXPERTS = N_GROUPS * EXPERTS_PER_GROUP
TOP_K = 2
D_EXPERT = 512
EPS = 1e-6

SWA_Q_W = SWA_Q_HEADS * SWA_HEAD_DIM
SWA_KV_W = SWA_KV_HEADS * SWA_HEAD_DIM
GDN_W = GDN_HEADS * GDN_HEAD_DIM
EVEN_MAIN = SWA_Q_W + 2 * SWA_KV_W + 4 * GDN_W
EVEN_COL_Q = 0
EVEN_COL_GDN = SWA_Q_W
EVEN_COL_Z = EVEN_COL_GDN + 3 * GDN_W
EVEN_COL_K = EVEN_COL_Z + GDN_W
EVEN_COL_V = EVEN_COL_K + SWA_KV_W
RET_QK_W = RET_HEADS * RET_QK_DIM
RET_V_W = RET_HEADS * RET_V_DIM
ODD_IN = 2 * RET_QK_W + 2 * RET_V_W

LANES = 128
VMEM_LIMIT = 56 * 1024 * 1024

NEG_INF = float("-inf")


def _cparams(*sem):
    return pltpu.CompilerParams(dimension_semantics=sem, vmem_limit_bytes=VMEM_LIMIT)


def _silu(x):
    return x * (1.0 / (1.0 + jnp.exp(-x)))


def _norm_proj_kernel(x_ref, g_ref, w_ref, *rest, tn, has_gate, n_perm, perm_w, conv_cols, seq_tiles):
    rest = list(rest)
    wg_ref = rest.pop(0) if has_gate else None
    perm_ref = rest.pop(0) if n_perm else None
    cw_ref = rest.pop(0) if conv_cols else None
    o_ref = rest.pop(0)
    og_ref = rest.pop(0) if has_gate else None
    wp_ref = rest.pop(0) if n_perm else None
    cbuf_ref = rest.pop(0) if conv_cols else None
    tm = x_ref.shape[0]
    pad = 8

    if conv_cols:
        @pl.when(pl.program_id(0) % seq_tiles == 0)
        def _():
            cbuf_ref[0:pad, :] = jnp.zeros((pad, cbuf_ref.shape[1]), F32)

    if n_perm:
        @pl.when(pl.program_id(0) == 0)
        def _():
            for j in range(n_perm):
                cols = slice(j * perm_w, (j + 1) * perm_w)
                wp_ref[:, cols] = jnp.dot(w_ref[:, cols], perm_ref[...], preferred_element_type=F32).astype(BF16)

    x = x_ref[...]
    h = x * lax.rsqrt(jnp.mean(x * x, axis=-1, keepdims=True) + EPS) * g_ref[...]
    hb = h.astype(BF16)
    n = o_ref.shape[-1]
    for lo in range(0, n, tn):
        hi = min(lo + tn, n)
        src = wp_ref if hi <= n_perm * perm_w else w_ref
        res = jnp.dot(hb, src[:, lo:hi], preferred_element_type=F32)
        if conv_cols and conv_cols[0] <= lo and hi <= conv_cols[1]:
            cc = slice(lo - conv_cols[0], hi - conv_cols[0])
            cbuf_ref[pad:pad + tm, cc] = res
            res = None
            for j in range(CONV_K):
                off = pad - (CONV_K - 1) + j
                term = cbuf_ref[off:off + tm, cc] * cw_ref[j:j + 1, cc]
                res = term if res is None else res + term
            res = _silu(res)
            cbuf_ref[0:pad, cc] = cbuf_ref[tm:tm + pad, cc]
        o_ref[:, lo:hi] = res.astype(o_ref.dtype)
    if has_gate:
        h_lo = (h - hb.astype(F32)).astype(BF16)
        r = jnp.dot(hb, wg_ref[...], preferred_element_type=F32)
        og_ref[...] = r[:, :LANES] + r[:, LANES:] + jnp.dot(h_lo, wg_ref[:, :LANES], preferred_element_type=F32)


def _norm_proj(x2, gain, w_bf16, w_gate_f32=None, perm=None, n_perm=0, conv_w=None, conv_cols=None, seq_len=None,
               *, tm=512, tn=512):
    t, d = x2.shape
    n = w_bf16.shape[1]
    has_gate = w_gate_f32 is not None
    perm_w = perm.shape[0] if n_perm else 0
    assert (n_perm * perm_w) % tn == 0
    seq_tiles = seq_len // tm if conv_cols else 0
    assert not conv_cols or (conv_cols[0] % tn == 0 and conv_cols[1] % tn == 0 and seq_len % tm == 0)
    in_specs = [pl.BlockSpec((tm, d), lambda i: (i, 0)),
                pl.BlockSpec((1, d), lambda i: (0, 0)),
                pl.BlockSpec((d, n), lambda i: (0, 0))]
    args = [x2, gain.reshape(1, d), w_bf16]
    out_shape = [jax.ShapeDtypeStruct((t, n), BF16)]
    out_specs = [pl.BlockSpec((tm, n), lambda i: (i, 0))]
    scratch = []
    if has_gate:
        wg_hi = w_gate_f32.astype(BF16)
        wg_lo = (w_gate_f32 - wg_hi.astype(F32)).astype(BF16)
        in_specs.append(pl.BlockSpec((d, 2 * LANES), lambda i: (0, 0)))
        args.append(jnp.concatenate([wg_hi, wg_lo], axis=1))
        out_shape.append(jax.ShapeDtypeStruct((t, LANES), F32))
        out_specs.append(pl.BlockSpec((tm, LANES), lambda i: (i, 0)))
    if n_perm:
        in_specs.append(pl.BlockSpec((perm_w, perm_w), lambda i: (0, 0)))
        args.append(perm)
        scratch.append(pltpu.VMEM((d, n_perm * perm_w), BF16))
    if conv_cols:
        cw = conv_cols[1] - conv_cols[0]
        in_specs.append(pl.BlockSpec((CONV_K, cw), lambda i: (0, 0)))
        args.append(conv_w.astype(F32))
        scratch.append(pltpu.VMEM((tm + 8, cw), F32))
    res = pl.pallas_call(
        functools.partial(_norm_proj_kernel, tn=tn, has_gate=has_gate, n_perm=n_perm, perm_w=perm_w,
                          conv_cols=conv_cols, seq_tiles=seq_tiles),
        grid=(t // tm,), in_specs=in_specs, out_specs=out_specs, out_shape=out_shape, scratch_shapes=scratch,
        compiler_params=_cparams("arbitrary"), name="norm_proj")(*args)
    return res if has_gate else res[0]


def _out_proj_kernel(*refs, n_in):
    res_ref = refs[0]
    a_refs = refs[1:1 + n_in]
    w_refs = refs[1 + n_in:1 + 2 * n_in]
    o_ref = refs[1 + 2 * n_in]
    acc = res_ref[...]
    for a_ref, w_ref in zip(a_refs, w_refs):
        acc = acc + jnp.dot(a_ref[...], w_ref[...], preferred_element_type=F32)
    o_ref[...] = acc


def _out_proj(res, acts, ws, *, tm=512):
    t, d = res.shape
    n_in = len(acts)
    in_specs = [pl.BlockSpec((tm, d), lambda i: (i, 0))]
    in_specs += [pl.BlockSpec((tm, a.shape[1]), lambda i: (i, 0)) for a in acts]
    in_specs += [pl.BlockSpec(w.shape, lambda i: (0, 0)) for w in ws]
    return pl.pallas_call(
        functools.partial(_out_proj_kernel, n_in=n_in),
        grid=(t // tm,), in_specs=in_specs, out_specs=pl.BlockSpec((tm, d), lambda i: (i, 0)),
        out_shape=jax.ShapeDtypeStruct((t, d), F32),
        compiler_params=_cparams("parallel"), name="out_proj")(res, *acts, *ws)


def _retention_kernel(q_ref, k_ref, v_ref, z_ref, cos_ref, sin_ref, decay_ref, xi_ref, zeta_ref, gam_ref,
                      o_ref, state_ref, *, c, n_chunks):
    half = RET_QK_DIM // 2

    @pl.when(pl.program_id(2) == 0)
    def _():
        state_ref[...] = jnp.zeros_like(state_ref)

    cos = cos_ref[...]
    sin = sin_ref[...]

    def rotate(t):
        a = t[:, :half].astype(F32)
        b = t[:, half:].astype(F32)
        return jnp.concatenate([a * cos - b * sin, b * cos + a * sin], axis=-1)

    q = rotate(q_ref[0])
    k = rotate(k_ref[0]) * (RET_QK_DIM ** -0.5)
    decay = decay_ref[0]
    gamma_c = gam_ref[0, 0:1, 0:1]
    xi = xi_ref[0][:, 0:1]
    zeta = zeta_ref[0][:, 0:1]
    chunks = range(n_chunks)
    rows = [slice(ci * c, (ci + 1) * c) for ci in chunks]
    qb = [q[r].astype(BF16) for r in rows]
    kb = [k[r].astype(BF16) for r in rows]
    kz = [(k[r] * zeta).astype(BF16) for r in rows]
    vc = [v_ref[0, r, :] for r in rows]
    inner = [lax.dot_general(qb[i], kb[i], (((1,), (1,)), ((), ())), preferred_element_type=F32) * decay
             for i in chunks]
    kv = [lax.dot_general(kz[i], vc[i], (((0,), (0,)), ((), ())), preferred_element_type=F32) for i in chunks]
    intra = [jnp.dot(inner[i].astype(BF16), vc[i], preferred_element_type=F32) for i in chunks]
    states = [state_ref[...]]
    for i in chunks:
        states.append(states[i] * gamma_c + kv[i])
    state_ref[...] = states[-1]
    cross = [jnp.dot(qb[i], states[i].astype(BF16), preferred_element_type=F32) for i in chunks]
    for i in chunks:
        out = intra[i] + cross[i] * xi
        o = out * lax.rsqrt(jnp.mean(out * out, axis=-1, keepdims=True) + EPS)
        z = z_ref[0, rows[i], :].astype(F32)
        o_ref[0, rows[i], :] = (_silu(z) * o).astype(o_ref.dtype)


RET_KERNEL_CHUNK = 256


def _retention(p, bsz, s, *, blk=1024):
    h = RET_HEADS
    c = RET_KERNEL_CHUNK
    half = RET_QK_DIM // 2
    pos = jnp.arange(s, dtype=F32)
    freq = 1.0 / (XPOS_THETA ** jnp.linspace(0.0, 1.0, half, dtype=F32))
    ang = pos[:, None] * freq[None, :]
    cos, sin = jnp.cos(ang), jnp.sin(ang)
    log_gamma = jnp.log(1.0 - 2.0 ** (-5.0 - jnp.arange(h, dtype=F32)))
    idx = jnp.arange(c, dtype=F32)
    rel = idx[:, None] - idx[None, :]
    decay = jnp.exp(jnp.where(rel >= 0, rel * log_gamma[:, None, None], NEG_INF))
    xi = jnp.broadcast_to(jnp.exp((idx + 1.0) * log_gamma[:, None])[..., None], (h, c, LANES))
    zeta = jnp.broadcast_to(jnp.exp((c - 1.0 - idx) * log_gamma[:, None])[..., None], (h, c, LANES))
    gam = jnp.broadcast_to(jnp.exp(c * log_gamma)[:, None, None], (h, 8, LANES))
    nq = RET_QK_W // RET_QK_DIM
    nv = RET_V_W // RET_V_DIM
    return pl.pallas_call(
        functools.partial(_retention_kernel, c=c, n_chunks=blk // c),
        grid=(bsz, h, s // blk),
        in_specs=[
            pl.BlockSpec((1, blk, RET_QK_DIM), lambda b, hh, i: (b, i, hh)),
            pl.BlockSpec((1, blk, RET_QK_DIM), lambda b, hh, i: (b, i, nq + hh)),
            pl.BlockSpec((1, blk, RET_V_DIM), lambda b, hh, i: (b, i, nv + hh)),
            pl.BlockSpec((1, blk, RET_V_DIM), lambda b, hh, i: (b, i, 2 * nv + hh)),
            pl.BlockSpec((blk, half), lambda b, hh, i: (i, 0)),
            pl.BlockSpec((blk, half), lambda b, hh, i: (i, 0)),
            pl.BlockSpec((1, c, c), lambda b, hh, i: (hh, 0, 0)),
            pl.BlockSpec((1, c, LANES), lambda b, hh, i: (hh, 0, 0)),
            pl.BlockSpec((1, c, LANES), lambda b, hh, i: (hh, 0, 0)),
            pl.BlockSpec((1, 8, LANES), lambda b, hh, i: (hh, 0, 0)),
        ],
        out_specs=pl.BlockSpec((1, blk, RET_V_DIM), lambda b, hh, i: (b, i, hh)),
        out_shape=jax.ShapeDtypeStruct((bsz, s, RET_V_W), BF16),
        scratch_shapes=[pltpu.VMEM((RET_QK_DIM, RET_V_DIM), F32)],
        compiler_params=_cparams("parallel", "parallel", "arbitrary"), name="retention",
    )(p, p, p, p, cos, sin, decay, xi, zeta, gam)


def _swa_kernel(sink_ref, q_ref, kc_ref, kp_ref, vc_ref, vp_ref, cos_ref, sin_ref, cosp_ref, sinp_ref,
                qg_ref, kg_ref, bd_ref, o_ref):
    w = WINDOW
    dh = SWA_HEAD_DIM
    grp = SWA_Q_HEADS // SWA_KV_HEADS
    n = pl.program_id(1)

    def head_rms(x, gain, bd):
        sq = x * x
        hi = sq.astype(BF16)
        lo = (sq - hi.astype(F32)).astype(BF16)
        ssum = jnp.dot(hi, bd, preferred_element_type=F32) + jnp.dot(lo, bd, preferred_element_type=F32)
        return x * lax.rsqrt(ssum * (1.0 / dh) + EPS) * gain

    def rope(x, cos, sin):
        width = x.shape[-1]
        lane = lax.broadcasted_iota(jnp.int32, x.shape, 1) % dh
        partner = jnp.where(lane < ROT_DIM // 2, pltpu.roll(x, width - ROT_DIM // 2, 1),
                            pltpu.roll(x, ROT_DIM // 2, 1))
        return x * cos + partner * sin

    cos, sin = cos_ref[...], sin_ref[...]
    cosq = jnp.concatenate([cos] * (SWA_Q_W // LANES), axis=1)
    sinq = jnp.concatenate([sin] * (SWA_Q_W // LANES), axis=1)
    q = rope(head_rms(q_ref[0].astype(F32), qg_ref[...], bd_ref[...]), cosq, sinq)
    q = (q * (dh ** -0.5)).astype(BF16)
    bdk = bd_ref[0:SWA_KV_W, 0:SWA_KV_W]
    kcur = rope(head_rms(kc_ref[0].astype(F32), kg_ref[...], bdk), cos, sin)
    kprev = rope(head_rms(kp_ref[0].astype(F32), kg_ref[...], bdk), cosp_ref[...], sinp_ref[...])
    kband = jnp.concatenate([kprev, kcur], axis=0).astype(BF16)
    vband = jnp.concatenate([vp_ref[0], vc_ref[0]], axis=0)

    i = lax.broadcasted_iota(jnp.int32, (w, 2 * w), 0)
    j = lax.broadcasted_iota(jnp.int32, (w, 2 * w), 1)
    mask = (j > i) & (j <= i + w) & ((j >= w) | (n > 0))
    heads = range(SWA_Q_HEADS)
    kh = [kband[:, hk * dh:(hk + 1) * dh] for hk in range(SWA_KV_HEADS)]
    vh = [vband[:, hk * dh:(hk + 1) * dh] for hk in range(SWA_KV_HEADS)]
    sc = [lax.dot_general(q[:, hq * dh:(hq + 1) * dh], kh[hq // grp], (((1,), (1,)), ((), ())),
                          preferred_element_type=F32) for hq in heads]
    sc = [jnp.where(mask, x, NEG_INF) for x in sc]
    m = [jnp.maximum(jnp.max(sc[hq], axis=-1, keepdims=True), sink_ref[hq]) for hq in heads]
    p = [jnp.exp(sc[hq] - m[hq]) for hq in heads]
    denom = [jnp.sum(p[hq], axis=-1, keepdims=True) + jnp.exp(sink_ref[hq] - m[hq]) for hq in heads]
    pv = [jnp.dot(p[hq].astype(BF16), vh[hq // grp], preferred_element_type=F32) for hq in heads]
    o_ref[0] = jnp.concatenate([pv[hq] * (1.0 / denom[hq]) for hq in heads], axis=-1).astype(o_ref.dtype)


def _rope_tables(s):
    half = ROT_DIM // 2
    pos = jnp.arange(s, dtype=F32)
    inv_freq = 1.0 / (ROPE_THETA ** (jnp.arange(half, dtype=F32) * 2.0 / ROT_DIM))
    ang = pos[:, None] * inv_freq[None, :]
    cos, sin = jnp.cos(ang), jnp.sin(ang)
    ones = jnp.ones((s, SWA_HEAD_DIM - ROT_DIM), F32)
    cos_h = jnp.concatenate([cos, cos, ones], axis=1)
    sin_h = jnp.concatenate([-sin, sin, 0.0 * ones], axis=1)
    reps = LANES // SWA_HEAD_DIM
    return jnp.tile(cos_h, (1, reps)), jnp.tile(sin_h, (1, reps))


def _swa(p, sinks, q_gain, k_gain, bsz, s):
    w = WINDOW
    cos, sin = _rope_tables(s)
    qg = jnp.tile(q_gain.astype(F32), SWA_Q_HEADS).reshape(1, SWA_Q_W)
    kg = jnp.tile(k_gain.astype(F32), SWA_KV_HEADS).reshape(1, SWA_KV_W)
    seg = np.arange(SWA_Q_W) // SWA_HEAD_DIM
    bd = jnp.asarray(seg[:, None] == seg[None, :], dtype=BF16)
    qblk, kblk, vblk = EVEN_COL_Q // SWA_Q_W, EVEN_COL_K // SWA_KV_W, EVEN_COL_V // SWA_KV_W
    return pl.pallas_call(
        _swa_kernel,
        grid=(bsz, s // w),
        in_specs=[
            pl.BlockSpec(memory_space=pltpu.SMEM),
            pl.BlockSpec((1, w, SWA_Q_W), lambda b, i: (b, i, qblk)),
            pl.BlockSpec((1, w, SWA_KV_W), lambda b, i: (b, i, kblk)),
            pl.BlockSpec((1, w, SWA_KV_W), lambda b, i: (b, jnp.maximum(i - 1, 0), kblk)),
            pl.BlockSpec((1, w, SWA_KV_W), lambda b, i: (b, i, vblk)),
            pl.BlockSpec((1, w, SWA_KV_W), lambda b, i: (b, jnp.maximum(i - 1, 0), vblk)),
            pl.BlockSpec((w, LANES), lambda b, i: (i, 0)),
            pl.BlockSpec((w, LANES), lambda b, i: (i, 0)),
            pl.BlockSpec((w, LANES), lambda b, i: (jnp.maximum(i - 1, 0), 0)),
            pl.BlockSpec((w, LANES), lambda b, i: (jnp.maximum(i - 1, 0), 0)),
            pl.BlockSpec((1, SWA_Q_W), lambda b, i: (0, 0)),
            pl.BlockSpec((1, SWA_KV_W), lambda b, i: (0, 0)),
            pl.BlockSpec((SWA_Q_W, SWA_Q_W), lambda b, i: (0, 0)),
        ],
        out_specs=pl.BlockSpec((1, w, SWA_Q_W), lambda b, i: (b, i, 0)),
        out_shape=jax.ShapeDtypeStruct((bsz, s, SWA_Q_W), BF16),
        compiler_params=_cparams("parallel", "parallel"), name="swa",
    )(sinks.astype(F32), p, p, p, p, p, cos, sin, cos, sin, qg, kg, bd)


def _softplus(x):
    return jnp.maximum(x, 0.0) + jnp.log1p(jnp.exp(-jnp.abs(x)))


def _split3(x):
    x1 = x.astype(BF16)
    r1 = x - x1.astype(F32)
    x2 = r1.astype(BF16)
    return x1, x2, (r1 - x2.astype(F32)).astype(BF16)


def _gdn_kernel(q_ref, k_ref, v_ref, z_ref, gate_ref, alog_ref, dt_ref, gain_ref, o_ref, state_ref, *, blk):
    c = GDN_CHUNK
    dk = GDN_HEAD_DIM
    nh = GDN_HEADS
    sub = LANES
    nt = (((1,), (1,)), ((), ()))

    @pl.when(pl.program_id(1) == 0)
    def _():
        state_ref[...] = jnp.zeros_like(state_ref)

    gates = gate_ref[0]
    g_all = -jnp.exp(alog_ref[...]) * _softplus(gates + dt_ref[...])
    beta_all = 1.0 / (1.0 + jnp.exp(-gates))
    ii = lax.broadcasted_iota(jnp.int32, (sub, sub), 0)
    jj = lax.broadcasted_iota(jnp.int32, (sub, sub), 1)
    same = (ii // c) == (jj // c)
    incl = same & (ii >= jj)
    strict = same & (ii > jj)
    tri = jnp.where(incl, 1.0, 0.0).astype(BF16)
    eye = jnp.where(ii == jj, 1.0, 0.0).astype(F32)
    er = lax.broadcasted_iota(jnp.int32, (8, LANES), 0)
    ec = lax.broadcasted_iota(jnp.int32, (8, LANES), 1)
    eye8 = jnp.where(er == ec, 1.0, 0.0).astype(BF16)
    g_parts = _split3(g_all)
    g_rows = sum(lax.dot_general(eye8, gp, nt, preferred_element_type=F32) for gp in g_parts)
    gr_parts = _split3(g_rows)
    gc_cols, gc_rows = [], []
    for sb in range(blk // sub):
        rows = slice(sb * sub, (sb + 1) * sub)
        gc_cols.append(sum(jnp.dot(tri, gp[rows], preferred_element_type=F32) for gp in g_parts))
        gc_rows.append(sum(lax.dot_general(gp[:, rows], tri, nt, preferred_element_type=F32) for gp in gr_parts))

    n_sub = blk // sub
    units = [(hd, sb) for hd in range(nh) for sb in range(n_sub)]
    dot = functools.partial(jnp.dot, preferred_element_type=F32)

    def dot_nt(x, y):
        return lax.dot_general(x, y, nt, preferred_element_type=F32)

    qs, ks, vs, betas = [], [], [], []
    for hd in range(nh):
        cols = slice(hd * dk, (hd + 1) * dk)
        q = q_ref[0, :, cols].astype(F32)
        k = k_ref[0, :, cols].astype(F32)
        qs.append(q * lax.rsqrt(jnp.sum(q * q, axis=-1, keepdims=True) + EPS) * (dk ** -0.5))
        ks.append(k * lax.rsqrt(jnp.sum(k * k, axis=-1, keepdims=True) + EPS))
        vs.append(v_ref[0, :, cols].astype(F32))
        betas.append(beta_all[:, nh + hd:nh + hd + 1])

    def rows_of(sb):
        return slice(sb * sub, (sb + 1) * sub)

    gcb = [jnp.broadcast_to(gc_cols[sb][:, hd:hd + 1], (sub, LANES)) for hd, sb in units]
    decay = [jnp.exp(jnp.where(incl, gcb[u] - gc_rows[sb][hd:hd + 1, :], NEG_INF)) for u, (hd, sb) in enumerate(units)]
    eg = [jnp.exp(g) for g in gcb]
    q_u = [qs[hd][rows_of(sb)] for hd, sb in units]
    k_u = [ks[hd][rows_of(sb)] for hd, sb in units]
    kb = [k_u[u] * betas[hd][rows_of(sb)] for u, (hd, sb) in enumerate(units)]
    kbf = [x.astype(BF16) for x in k_u]
    a = [dot_nt(kb[u].astype(BF16), kbf[u]) for u in range(len(units))]
    m = [jnp.where(strict, -(a[u] * decay[u]), 0.0) for u in range(len(units))]
    r = [eye + x for x in m]
    mb = [x.astype(BF16) for x in m]
    pw = [dot(x, x) for x in mb]
    for _ in range(int(np.log2(c)) - 2):
        pb = [x.astype(BF16) for x in pw]
        both = [dot(jnp.concatenate([r[u].astype(BF16), pb[u]], axis=0), pb[u]) for u in range(len(units))]
        r = [r[u] + both[u][:sub] for u in range(len(units))]
        pw = [x[sub:] for x in both]
    r = [r[u] + dot(r[u].astype(BF16), pw[u].astype(BF16)) for u in range(len(units))]
    rhs = [jnp.concatenate([kb[u] * eg[u], vs[hd][rows_of(sb)] * betas[hd][rows_of(sb)]], axis=1).astype(BF16)
           for u, (hd, sb) in enumerate(units)]
    wu = [dot(r[u].astype(BF16), rhs[u]).astype(BF16) for u in range(len(units))]
    qk = [dot_nt(q_u[u].astype(BF16), kbf[u]) for u in range(len(units))]
    iwu = [dot((qk[u] * decay[u]).astype(BF16), wu[u]) for u in range(len(units))]
    qe = [(q_u[u] * eg[u] - iwu[u][:, :dk]).astype(BF16) for u in range(len(units))]
    per_chunk = sub // c
    kwu, g_last = {}, {}
    for u, (hd, sb) in enumerate(units):
        for ci in range(per_chunk):
            crow = slice(ci * c, (ci + 1) * c)
            gl = gcb[u][(ci + 1) * c - 1:(ci + 1) * c, :]
            k_dec = (k_u[u][crow] * jnp.exp(gl - gcb[u][crow])).astype(BF16)
            g_last[hd, sb * per_chunk + ci] = jnp.exp(gl)
            kwu[hd, sb * per_chunk + ci] = lax.dot_general(k_dec, wu[u][crow], (((0,), (0,)), ((), ())),
                                                           preferred_element_type=F32)
    outs = [[] for _ in range(nh)]
    for ch in range(blk // c):
        u_sb, ci = ch // per_chunk, ch % per_chunk
        crow = slice(ci * c, (ci + 1) * c)
        states = [state_ref[hd] for hd in range(nh)]
        sbf = [x.astype(BF16) for x in states]
        for hd in range(nh):
            u = hd * n_sub + u_sb
            outs[hd].append(dot(qe[u][crow], sbf[hd]) + iwu[u][crow, dk:])
        corr = [dot(kwu[hd, ch][:, :dk].astype(BF16), sbf[hd]) for hd in range(nh)]
        for hd in range(nh):
            state_ref[hd] = states[hd] * g_last[hd, ch] + (kwu[hd, ch][:, dk:] - corr[hd])
    for hd in range(nh):
        out = jnp.concatenate(outs[hd], axis=0)
        o = out * lax.rsqrt(jnp.mean(out * out, axis=-1, keepdims=True) + EPS) * gain_ref[...]
        z = z_ref[0, :, hd * dk:(hd + 1) * dk].astype(F32)
        o_ref[0, :, hd * dk:(hd + 1) * dk] = (o * _silu(z)).astype(o_ref.dtype)


def _gdn(p, pg, a_log, dt_bias, gdn_gain, bsz, s, *, blk=256):
    dk = GDN_HEAD_DIM
    assert EVEN_COL_GDN % GDN_W == 0 and EVEN_COL_Z % GDN_W == 0
    lane_pad = (0, LANES - GDN_HEADS)
    alog = jnp.pad(a_log.astype(F32), lane_pad).reshape(1, LANES)
    dt = jnp.pad(dt_bias.astype(F32), lane_pad).reshape(1, LANES)
    return pl.pallas_call(
        functools.partial(_gdn_kernel, blk=blk),
        grid=(bsz, s // blk),
        in_specs=[pl.BlockSpec((1, blk, GDN_W), lambda b, i: (b, i, EVEN_COL_GDN // GDN_W)),
                  pl.BlockSpec((1, blk, GDN_W), lambda b, i: (b, i, EVEN_COL_GDN // GDN_W + 1)),
                  pl.BlockSpec((1, blk, GDN_W), lambda b, i: (b, i, EVEN_COL_GDN // GDN_W + 2)),
                  pl.BlockSpec((1, blk, GDN_W), lambda b, i: (b, i, EVEN_COL_Z // GDN_W)),
                  pl.BlockSpec((1, blk, LANES), lambda b, i: (b, i, 0)),
                  pl.BlockSpec((1, LANES), lambda b, i: (0, 0)),
                  pl.BlockSpec((1, LANES), lambda b, i: (0, 0)),
                  pl.BlockSpec((1, dk), lambda b, i: (0, 0))],
        out_specs=pl.BlockSpec((1, blk, GDN_W), lambda b, i: (b, i, 0)),
        out_shape=jax.ShapeDtypeStruct((bsz, s, GDN_W), BF16),
        scratch_shapes=[pltpu.VMEM((GDN_HEADS, dk, dk), F32)],
        compiler_params=_cparams("parallel", "arbitrary"), name="gdn",
    )(p, p, p, p, pg, alog, dt, gdn_gain.astype(F32).reshape(1, dk))


MOE_BM = 512
ROUTER_ROWS = LANES
EXPERT_ROW0 = 8


SLAB = 8
DMA_UNROLL = 4


def _store_slabs(ref, val, rows):
    for sl in range(SLAB):
        ref[pl.ds(sl, rows, stride=SLAB), :] = val[:, sl * LANES:(sl + 1) * LANES]


def _load_slabs(ref, rows):
    return jnp.concatenate([ref[pl.ds(sl, rows, stride=SLAB), :] for sl in range(SLAB)], axis=1)


def _slab(ref, row):
    return ref.at[pl.ds(pl.multiple_of(row * SLAB, SLAB), SLAB), :]


def _ffn_norm(x, gain):
    return x * lax.rsqrt(jnp.mean(x * x, axis=-1, keepdims=True) + EPS) * gain


def _router_kernel(x_ref, g_ref, wr_ref, wrlo_ref, br_ref, eid_ref, gate_ref):
    h = _ffn_norm(x_ref[...], g_ref[...])
    tm = h.shape[0]
    h_hi = h.astype(BF16)
    h_lo = (h - h_hi.astype(F32)).astype(BF16)
    nt = (((1,), (1,)), ((), ()))
    logits = (lax.dot_general(wr_ref[...], h_hi, nt, preferred_element_type=F32)
              + lax.dot_general(wr_ref[...], h_lo, nt, preferred_element_type=F32)
              + lax.dot_general(wrlo_ref[...], h_hi, nt, preferred_element_type=F32)) + br_ref[...]
    row = lax.broadcasted_iota(jnp.int32, (8, tm), 0)
    gl = jnp.where(row < N_GROUPS, logits[0:8, :], NEG_INF)
    gmax = jnp.max(gl, axis=0, keepdims=True)
    grp_p = 1.0 / jnp.sum(jnp.exp(gl - gmax), axis=0, keepdims=True)
    grp_idx = jnp.min(jnp.where(gl == gmax, row, 8), axis=0, keepdims=True)
    el = jnp.zeros((8, tm), F32)
    for g in range(N_GROUPS):
        lo = EXPERT_ROW0 + g * EXPERTS_PER_GROUP
        el = jnp.where(grp_idx == g, logits[lo:lo + EXPERTS_PER_GROUP, :], el)
    m1 = jnp.max(el, axis=0, keepdims=True)
    i1 = jnp.min(jnp.where(el == m1, row, 8), axis=0, keepdims=True)
    el2 = jnp.where(row == i1, NEG_INF, el)
    m2 = jnp.max(el2, axis=0, keepdims=True)
    i2 = jnp.min(jnp.where(el2 == m2, row, 8), axis=0, keepdims=True)
    z = jnp.sum(jnp.exp(el - m1), axis=0, keepdims=True)
    p1 = 1.0 / z
    p2 = jnp.exp(m2 - m1) / z
    scale = grp_p / (p1 + p2)
    e1 = grp_idx * EXPERTS_PER_GROUP + i1
    e2 = grp_idx * EXPERTS_PER_GROUP + i2
    eid_ref[...] = jnp.where(row == 0, e1, jnp.where(row == 1, e2, 0))
    gate_ref[...] = jnp.where(row == 0, p1 * scale, jnp.where(row == 1, p2 * scale, 0.0))


def _moe_router(x2, gain, w_rg, b_rg, w_re, b_re, *, tm=512):
    t, d = x2.shape
    wr = jnp.zeros((ROUTER_ROWS, d), F32)
    wr = wr.at[0:N_GROUPS].set(w_rg.astype(F32).T).at[EXPERT_ROW0:EXPERT_ROW0 + N_EXPERTS].set(w_re.astype(F32).T)
    br = jnp.zeros((ROUTER_ROWS, 1), F32)
    br = br.at[0:N_GROUPS, 0].set(b_rg.astype(F32)).at[EXPERT_ROW0:EXPERT_ROW0 + N_EXPERTS, 0].set(b_re.astype(F32))
    wr_hi = wr.astype(BF16)
    wr_lo = (wr - wr_hi.astype(F32)).astype(BF16)
    return pl.pallas_call(
        _router_kernel, grid=(t // tm,),
        in_specs=[pl.BlockSpec((tm, d), lambda i: (i, 0)),
                  pl.BlockSpec((1, d), lambda i: (0, 0)),
                  pl.BlockSpec((ROUTER_ROWS, d), lambda i: (0, 0)),
                  pl.BlockSpec((ROUTER_ROWS, d), lambda i: (0, 0)),
                  pl.BlockSpec((ROUTER_ROWS, 1), lambda i: (0, 0))],
        out_specs=[pl.BlockSpec((8, tm), lambda i: (0, i)),
                   pl.BlockSpec((8, tm), lambda i: (0, i))],
        out_shape=[jax.ShapeDtypeStruct((8, t), jnp.int32),
                   jax.ShapeDtypeStruct((8, t), F32)],
        compiler_params=_cparams("parallel"), name="moe_router")(x2, gain.reshape(1, d), wr_hi, wr_lo, br)


def _rank_kernel(eid_ref, dest_ref, meta_ref, cnt_ref, off_ref, *, bm):
    ph = pl.program_id(0)
    i = pl.program_id(1)
    tm = eid_ref.shape[1]
    erow = lax.broadcasted_iota(jnp.int32, (N_EXPERTS, tm), 0)
    oh0 = jnp.where(erow == eid_ref[0:1, :], 1.0, 0.0).astype(F32)
    oh1 = jnp.where(erow == eid_ref[1:2, :], 1.0, 0.0).astype(F32)
    oh = oh0 + oh1
    tile_cnt = jnp.sum(oh, axis=1, keepdims=True)

    @pl.when((ph == 0) & (i == 0))
    def _():
        cnt_ref[...] = jnp.zeros_like(cnt_ref)

    @pl.when(ph == 0)
    def _():
        cnt_ref[...] += tile_cnt

    @pl.when((ph == 1) & (i == 0))
    def _():
        cnt = cnt_ref[...]
        padded = jnp.ceil(cnt * (1.0 / bm)) * bm
        er = lax.broadcasted_iota(jnp.int32, (N_EXPERTS, N_EXPERTS), 0)
        ec = lax.broadcasted_iota(jnp.int32, (N_EXPERTS, N_EXPERTS), 1)
        lower = jnp.where(ec < er, 1.0, 0.0).astype(F32)
        start = jnp.dot(lower, padded, preferred_element_type=F32, precision=lax.Precision.HIGHEST)
        off_ref[...] = start
        end = start + padded
        nb = meta_ref.shape[1]
        blk_row = lax.broadcasted_iota(jnp.int32, (N_EXPERTS, nb), 1).astype(F32) * bm
        endb = jnp.concatenate([end] * (nb // LANES), axis=1)
        be = jnp.sum(jnp.where(endb <= blk_row, 1.0, 0.0), axis=0, keepdims=True)
        be = jnp.minimum(be, N_EXPERTS - 1.0)
        used = jnp.concatenate([end[N_EXPERTS - 1:N_EXPERTS, :]] * (nb // LANES), axis=1) * (1.0 / bm)
        elane = lax.broadcasted_iota(jnp.int32, (N_EXPERTS, nb), 1)
        erow2 = lax.broadcasted_iota(jnp.int32, (N_EXPERTS, nb), 0)
        ends = jnp.sum(jnp.where(elane == erow2, endb, 0.0), axis=0, keepdims=True)
        mrow = lax.broadcasted_iota(jnp.int32, (8, nb), 0)
        meta_ref[...] = jnp.where(mrow == 0, be, jnp.where(mrow == 1, used, jnp.where(mrow == 2, ends, 0.0))
                                  ).astype(jnp.int32)
        cnt_ref[...] = jnp.zeros_like(cnt_ref)

    @pl.when(ph == 1)
    def _():
        kr = lax.broadcasted_iota(jnp.int32, (tm, tm), 0)
        kc = lax.broadcasted_iota(jnp.int32, (tm, tm), 1)
        su = jnp.where(kr < kc, 1.0, 0.0).astype(BF16)
        csum = jnp.dot(oh.astype(BF16), su, preferred_element_type=F32)
        pos = csum + (off_ref[...] + cnt_ref[...])[:, 0:1]
        d0 = jnp.sum(oh0 * pos, axis=0, keepdims=True)
        d1 = jnp.sum(oh1 * pos, axis=0, keepdims=True)
        row = lax.broadcasted_iota(jnp.int32, (8, tm), 0)
        dest_ref[...] = jnp.where(row == 0, d0, jnp.where(row == 1, d1, 0.0)).astype(jnp.int32)
        cnt_ref[...] += tile_cnt


def _moe_rank(eid, n_blocks, *, bm, tm=1024):
    t = eid.shape[1]
    nb = -(-n_blocks // LANES) * LANES
    return pl.pallas_call(
        functools.partial(_rank_kernel, bm=bm), grid=(2, t // tm),
        in_specs=[pl.BlockSpec((8, tm), lambda p, i: (0, i))],
        out_specs=[pl.BlockSpec((8, tm), lambda p, i: (0, i * p)),
                   pl.BlockSpec((8, nb), lambda p, i: (0, 0))],
        out_shape=[jax.ShapeDtypeStruct((8, t), jnp.int32), jax.ShapeDtypeStruct((8, nb), jnp.int32)],
        scratch_shapes=[pltpu.VMEM((N_EXPERTS, LANES), F32), pltpu.VMEM((N_EXPERTS, LANES), F32)],
        compiler_params=_cparams("arbitrary", "arbitrary"), name="moe_rank")(eid)


def _dispatch_kernel(dest_ref, ends_ref, x_ref, g_ref, rows_hbm, zero_ref, hbuf_ref, sem, rsem, *, tm, bm):
    @pl.when(pl.program_id(0) == 0)
    def _():
        zero_ref[...] = jnp.zeros_like(zero_ref)

        def tail(e):
            start = ends_ref[0, e] - bm
            return pltpu.make_async_copy(zero_ref, rows_hbm.at[pl.ds(pl.multiple_of(start * SLAB, SLAB), bm * SLAB), :],
                                         sem)

        def nonempty(e):
            return ends_ref[0, e] > jnp.where(e > 0, ends_ref[0, jnp.maximum(e - 1, 0)], 0)

        def fill(e, carry):
            @pl.when(nonempty(e))
            def _():
                tail(e).start()
            return carry

        def drain(e, carry):
            @pl.when(nonempty(e))
            def _():
                tail(e).wait()
            return carry

        lax.fori_loop(0, N_EXPERTS, fill, 0)
        lax.fori_loop(0, N_EXPERTS, drain, 0)

        def unused(b):
            return pltpu.make_async_copy(
                zero_ref, rows_hbm.at[pl.ds(pl.multiple_of(b * (bm * SLAB), SLAB), bm * SLAB), :], sem)

        def fill_unused(b, carry):
            unused(b).start()
            return carry

        def drain_unused(b, carry):
            unused(b).wait()
            return carry

        first_unused = ends_ref[0, N_EXPERTS - 1] // bm
        n_blocks = rows_hbm.shape[0] // (bm * SLAB)
        lax.fori_loop(first_unused, n_blocks, fill_unused, 0)
        lax.fori_loop(first_unused, n_blocks, drain_unused, 0)

    i = pl.program_id(0)
    slot = lax.rem(i, 2)

    def drain_slot(s):
        for _ in range(TOP_K):
            pltpu.make_async_copy(hbuf_ref.at[s], rows_hbm.at[pl.ds(0, tm * SLAB), :], rsem.at[s]).wait()

    for s in range(2):
        @pl.when((slot == s) & (i >= 2))
        def _():
            drain_slot(s)

        @pl.when(slot == s)
        def _():
            _store_slabs(hbuf_ref.at[s], _ffn_norm(x_ref[...], g_ref[...]), tm)

            def issue(g, carry):
                for u in range(DMA_UNROLL):
                    tok = g * DMA_UNROLL + u
                    for k in range(TOP_K):
                        pltpu.make_async_copy(_slab(hbuf_ref.at[s], tok),
                                              _slab(rows_hbm, dest_ref[0, 0, k * tm + tok]),
                                              rsem.at[s]).start(priority=k)
                return carry

            lax.fori_loop(0, tm // DMA_UNROLL, issue, 0)

    last = pl.num_programs(0) - 1

    @pl.when(i == last)
    def _():
        for s in range(2):
            @pl.when((slot == s) | (last >= 1))
            def _():
                drain_slot(s)


def _moe_dispatch(dest_tiles, expert_ends, x2, gain, n_rows, *, tm, bm):
    t, d = x2.shape
    return pl.pallas_call(
        functools.partial(_dispatch_kernel, tm=tm, bm=bm), grid=(t // tm,),
        in_specs=[pl.BlockSpec((1, 1, TOP_K * tm), lambda i: (i, 0, 0), memory_space=pltpu.SMEM),
                  pl.BlockSpec(memory_space=pltpu.SMEM),
                  pl.BlockSpec((tm, d), lambda i: (i, 0)),
                  pl.BlockSpec((1, d), lambda i: (0, 0))],
        out_specs=pl.BlockSpec(memory_space=pl.ANY),
        out_shape=jax.ShapeDtypeStruct((n_rows * SLAB, LANES), F32),
        scratch_shapes=[pltpu.VMEM((bm * SLAB, LANES), F32), pltpu.VMEM((2, tm * SLAB, LANES), F32),
                        pltpu.SemaphoreType.DMA(()), pltpu.SemaphoreType.DMA((2,))],
        compiler_params=pltpu.CompilerParams(dimension_semantics=("arbitrary",), has_side_effects=True,
                                             vmem_limit_bytes=VMEM_LIMIT),
        name="moe_dispatch")(dest_tiles, expert_ends, x2, gain.reshape(1, d))


def _experts_kernel(be_ref, used_ref, x_ref, wg_ref, wu_ref, wd_ref, y_ref, wgb_ref, wub_ref, wdb_ref):
    b = pl.program_id(0)
    prev = be_ref[jnp.maximum(b - 1, 0)]

    @pl.when((b == 0) | (be_ref[b] != prev))
    def _():
        wgb_ref[...] = wg_ref[0, 0].astype(BF16)
        wub_ref[...] = wu_ref[0, 0].astype(BF16)
        wdb_ref[...] = wd_ref[0, 0].astype(BF16)

    bm = x_ref.shape[0] // SLAB

    @pl.when(b < used_ref[0])
    def _():
        x = _load_slabs(x_ref, bm).astype(BF16)
        gate = jnp.dot(x, wgb_ref[...], preferred_element_type=F32)
        up = jnp.dot(x, wub_ref[...], preferred_element_type=F32)
        hid = (_silu(gate) * up).astype(BF16)
        _store_slabs(y_ref, jnp.dot(hid, wdb_ref[...], preferred_element_type=F32), bm)

    @pl.when(b >= used_ref[0])
    def _():
        y_ref[...] = jnp.zeros_like(y_ref)


def _moe_experts(block_expert, n_used, x_rows, w_gate, w_up, w_down, layer, *, bm):
    n_rows = x_rows.shape[0] // SLAB
    d, de = w_gate.shape[2], w_gate.shape[3]
    grid_spec = pltpu.PrefetchScalarGridSpec(
        num_scalar_prefetch=2, grid=(n_rows // bm,),
        in_specs=[pl.BlockSpec((bm * SLAB, LANES),
                               lambda b, be, nu: (jnp.minimum(b, jnp.maximum(nu[0] - 1, 0)), 0)),
                  pl.BlockSpec((1, 1, d, de), lambda b, be, nu: (layer, be[b], 0, 0)),
                  pl.BlockSpec((1, 1, d, de), lambda b, be, nu: (layer, be[b], 0, 0)),
                  pl.BlockSpec((1, 1, de, d), lambda b, be, nu: (layer, be[b], 0, 0))],
        out_specs=pl.BlockSpec((bm * SLAB, LANES), lambda b, be, nu: (b, 0)),
        scratch_shapes=[pltpu.VMEM((d, de), BF16), pltpu.VMEM((d, de), BF16), pltpu.VMEM((de, d), BF16)])
    return pl.pallas_call(
        _experts_kernel, grid_spec=grid_spec, out_shape=jax.ShapeDtypeStruct((n_rows * SLAB, LANES), F32),
        compiler_params=_cparams("arbitrary"), name="moe_experts",
    )(block_expert, n_used, x_rows, w_gate, w_up, w_down)


def _combine_kernel(dest_ref, next_ref, x_ref, gate_ref, y_hbm, o_ref, buf_ref, sem, *, tm):
    i = pl.program_id(0)
    slot = lax.rem(i, 2)

    def gather(idx_ref, s):
        def issue(g, carry):
            for u in range(DMA_UNROLL):
                tok = g * DMA_UNROLL + u
                for k in range(TOP_K):
                    pltpu.make_async_copy(_slab(y_hbm, idx_ref[0, 0, k * tm + tok]), _slab(buf_ref.at[s, k], tok),
                                          sem.at[s]).start(priority=k)
            return carry

        lax.fori_loop(0, tm // DMA_UNROLL, issue, 0)

    for s in range(2):
        @pl.when((i == 0) & (slot == s))
        def _():
            gather(dest_ref, s)

        @pl.when((i + 1 < pl.num_programs(0)) & (slot == 1 - s))
        def _():
            gather(next_ref, s)

    er = lax.broadcasted_iota(jnp.int32, (8, LANES), 0)
    ec = lax.broadcasted_iota(jnp.int32, (8, LANES), 1)
    eye = jnp.where(er == ec, 1.0, 0.0).astype(F32)
    gcol = lax.dot_general(gate_ref[...], eye, (((0,), (0,)), ((), ())), preferred_element_type=F32,
                           precision=lax.Precision.HIGHEST)
    for s in range(2):
        @pl.when(slot == s)
        def _():
            for k in range(TOP_K):
                pltpu.make_async_copy(y_hbm.at[pl.ds(0, tm * SLAB), :], buf_ref.at[s, k], sem.at[s]).wait()
            o_ref[...] = (x_ref[...] + gcol[:, 0:1] * _load_slabs(buf_ref.at[s, 0], tm)
                          + gcol[:, 1:2] * _load_slabs(buf_ref.at[s, 1], tm))


def _moe_combine(dest_tiles, x2, gates, y_rows, *, tm):
    t, d = x2.shape
    n = t // tm
    dest_spec = functools.partial(pl.BlockSpec, (1, 1, TOP_K * tm), memory_space=pltpu.SMEM)
    return pl.pallas_call(
        functools.partial(_combine_kernel, tm=tm), grid=(n,),
        in_specs=[dest_spec(lambda i: (i, 0, 0)),
                  dest_spec(lambda i: (jnp.minimum(i + 1, n - 1), 0, 0)),
                  pl.BlockSpec((tm, d), lambda i: (i, 0)),
                  pl.BlockSpec((8, tm), lambda i: (0, i)),
                  pl.BlockSpec(memory_space=pl.ANY)],
        out_specs=pl.BlockSpec((tm, d), lambda i: (i, 0)),
        out_shape=jax.ShapeDtypeStruct((t, d), F32),
        scratch_shapes=[pltpu.VMEM((2, TOP_K, tm * SLAB, LANES), F32), pltpu.SemaphoreType.DMA((2,))],
        compiler_params=_cparams("arbitrary"), name="moe_combine")(dest_tiles, dest_tiles, x2, gates, y_rows)


def _moe(x2, gain, w_rg, b_rg, w_re, b_re, w_gate, w_up, w_down, layer, *, tm_dma=512):
    t, d = x2.shape
    bm = MOE_BM
    n_rows = t * TOP_K + N_EXPERTS * bm
    n_blocks = n_rows // bm
    eid, gates = _moe_router(x2, gain, w_rg, b_rg, w_re, b_re)
    dest, meta = _moe_rank(eid, n_blocks, bm=bm)
    dest_tiles = dest[0:TOP_K].reshape(TOP_K, t // tm_dma, tm_dma).transpose(1, 0, 2).reshape(
        t // tm_dma, 1, TOP_K * tm_dma)
    x_rows = _moe_dispatch(dest_tiles, meta[2:3, :N_EXPERTS], x2, gain, n_rows, tm=tm_dma, bm=bm)
    y_rows = _moe_experts(meta[0, :n_blocks], meta[1, 0:1], x_rows, w_gate, w_up, w_down, layer, bm=bm)
    return _moe_combine(dest_tiles, x2, gates, y_rows, tm=tm_dma)


def _even_odd_perm():
    half = RET_QK_DIM // 2
    src = np.concatenate([2 * np.arange(half), 2 * np.arange(half) + 1])
    perm = np.zeros((RET_QK_DIM, RET_QK_DIM), np.float32)
    perm[src, np.arange(RET_QK_DIM)] = 1.0
    return jnp.asarray(perm, dtype=BF16)


def kernel(x, ln_mix_even, w_in_even, conv_w_even, a_log_even, dt_bias_even, q_gain_even, k_gain_even, sinks_even, gdn_gain_even, w_out_even, ln_mix_odd, w_in_odd, w_out_odd, ln_ffn, w_router_group, b_router_group, w_router_expert, b_router_expert, w_gate, w_up, w_down):
    bsz, s, d = x.shape
    t = bsz * s
    x2 = x.reshape(t, d).astype(F32)
    depth = ln_ffn.shape[0]
    for layer in range(depth):
        i = layer // 2
        if layer % 2 == 0:
            w_in = w_in_even[i]
            swa_end = SWA_Q_W + 2 * SWA_KV_W
            w_main = jnp.concatenate([w_in[:, :SWA_Q_W], w_in[:, swa_end:EVEN_MAIN], w_in[:, SWA_Q_W:swa_end]],
                                     axis=1).astype(BF16)
            w_gates = jnp.pad(w_in[:, EVEN_MAIN:].astype(F32), ((0, 0), (0, LANES - 2 * GDN_HEADS)))
            p, pg = _norm_proj(x2, ln_mix_even[i], w_main, w_gates, conv_w=conv_w_even[i],
                               conv_cols=(EVEN_COL_GDN, EVEN_COL_Z), seq_len=s)
            p = p.reshape(bsz, s, EVEN_MAIN)
            y_a = _swa(p, sinks_even[i], q_gain_even[i], k_gain_even[i], bsz, s)
            y_b = _gdn(p, pg.reshape(bsz, s, LANES), a_log_even[i], dt_bias_even[i], gdn_gain_even[i], bsz, s)
            w_out = w_out_even[i].astype(BF16)
            x2 = _out_proj(x2, [y_a.reshape(t, SWA_Q_W), y_b.reshape(t, GDN_W)], [w_out[:SWA_Q_W], w_out[SWA_Q_W:]])
        else:
            p = _norm_proj(x2, ln_mix_odd[i], w_in_odd[i].astype(BF16), perm=_even_odd_perm(),
                           n_perm=2 * RET_HEADS).reshape(bsz, s, ODD_IN)
            y = _retention(p, bsz, s)
            x2 = _out_proj(x2, [y.reshape(t, RET_V_W)], [w_out_odd[i].astype(BF16)])
        x2 = _moe(x2, ln_ffn[layer], w_router_group[layer], b_router_group[layer], w_router_expert[layer],
                  b_router_expert[layer], w_gate, w_up, w_down, layer)
    return x2.reshape(bsz, s, d).astype(x.dtype)
```

```python
import functools

import numpy as np
import jax
import jax.numpy as jnp
from jax import lax
from jax.experimental import pallas as pl
from jax.experimental.pallas import tpu as pltpu

F32 = jnp.float32
BF16 = jnp.bfloat16

D_MODEL = 1024
SWA_Q_HEADS = 8
SWA_KV_HEADS = 2
SWA_HEAD_DIM = 64
WINDOW = 128
ROT_DIM = SWA_HEAD_DIM // 4
ROPE_THETA = 500000.0
GDN_HEADS = 4
GDN_HEAD_DIM = 128
CONV_K = 4
GDN_CHUNK = 64
RET_HEADS = 4
RET_QK_DIM = 256
RET_V_DIM = 512
RET_CHUNK = 128
XPOS_THETA = 10000.0
N_GROUPS = 4
EXPERTS_PER_GROUP = 8
N_EXPERTS = N_GROUPS * EXPERTS_PER_GROUP
TOP_K = 2
D_EXPERT = 512
EPS = 1e-6

SWA_Q_W = SWA_Q_HEADS * SWA_HEAD_DIM
SWA_KV_W = SWA_KV_HEADS * SWA_HEAD_DIM
GDN_W = GDN_HEADS * GDN_HEAD_DIM
EVEN_MAIN = SWA_Q_W + 2 * SWA_KV_W + 4 * GDN_W
EVEN_COL_Q = 0
EVEN_COL_GDN = SWA_Q_W
EVEN_COL_Z = EVEN_COL_GDN + 3 * GDN_W
EVEN_COL_K = EVEN_COL_Z + GDN_W
EVEN_COL_V = EVEN_COL_K + SWA_KV_W
RET_QK_W = RET_HEADS * RET_QK_DIM
RET_V_W = RET_HEADS * RET_V_DIM
ODD_IN = 2 * RET_QK_W + 2 * RET_V_W

LANES = 128
VMEM_LIMIT = 56 * 1024 * 1024

NEG_INF = float("-inf")


def _cparams(*sem):
    return pltpu.CompilerParams(dimension_semantics=sem, vmem_limit_bytes=VMEM_LIMIT)


def _silu(x):
    return x * (1.0 / (1.0 + jnp.exp(-x)))


def _norm_proj_kernel(x_ref, g_ref, w_ref, *rest, tn, has_gate, n_perm, perm_w, conv_cols, seq_tiles):
    rest = list(rest)
    wg_ref = rest.pop(0) if has_gate else None
    perm_ref = rest.pop(0) if n_perm else None
    cw_ref = rest.pop(0) if conv_cols else None
    o_ref = rest.pop(0)
    og_ref = rest.pop(0) if has_gate else None
    wp_ref = rest.pop(0) if n_perm else None
    cbuf_ref = rest.pop(0) if conv_cols else None
    tm = x_ref.shape[0]
    pad = 8

    if conv_cols:
        @pl.when(pl.program_id(0) % seq_tiles == 0)
        def _():
            cbuf_ref[0:pad, :] = jnp.zeros((pad, cbuf_ref.shape[1]), F32)

    if n_perm:
        @pl.when(pl.program_id(0) == 0)
        def _():
            for j in range(n_perm):
                cols = slice(j * perm_w, (j + 1) * perm_w)
                wp_ref[:, cols] = jnp.dot(w_ref[:, cols], perm_ref[...], preferred_element_type=F32).astype(BF16)

    x = x_ref[...]
    h = x * lax.rsqrt(jnp.mean(x * x, axis=-1, keepdims=True) + EPS) * g_ref[...]
    hb = h.astype(BF16)
    n = o_ref.shape[-1]
    for lo in range(0, n, tn):
        hi = min(lo + tn, n)
        src = wp_ref if hi <= n_perm * perm_w else w_ref
        res = jnp.dot(hb, src[:, lo:hi], preferred_element_type=F32)
        if conv_cols and conv_cols[0] <= lo and hi <= conv_cols[1]:
            cc = slice(lo - conv_cols[0], hi - conv_cols[0])
            cbuf_ref[pad:pad + tm, cc] = res
            res = None
            for j in range(CONV_K):
                off = pad - (CONV_K - 1) + j
                term = cbuf_ref[off:off + tm, cc] * cw_ref[j:j + 1, cc]
                res = term if res is None else res + term
            res = _silu(res)
            cbuf_ref[0:pad, cc] = cbuf_ref[tm:tm + pad, cc]
        o_ref[:, lo:hi] = res.astype(o_ref.dtype)
    if has_gate:
        h_lo = (h - hb.astype(F32)).astype(BF16)
        r = jnp.dot(hb, wg_ref[...], preferred_element_type=F32)
        og_ref[...] = r[:, :LANES] + r[:, LANES:] + jnp.dot(h_lo, wg_ref[:, :LANES], preferred_element_type=F32)


def _norm_proj(x2, gain, w_bf16, w_gate_f32=None, perm=None, n_perm=0, conv_w=None, conv_cols=None, seq_len=None,
               *, tm=512, tn=512):
    t, d = x2.shape
    n = w_bf16.shape[1]
    has_gate = w_gate_f32 is not None
    perm_w = perm.shape[0] if n_perm else 0
    assert (n_perm * perm_w) % tn == 0
    seq_tiles = seq_len // tm if conv_cols else 0
    assert not conv_cols or (conv_cols[0] % tn == 0 and conv_cols[1] % tn == 0 and seq_len % tm == 0)
    in_specs = [pl.BlockSpec((tm, d), lambda i: (i, 0)),
                pl.BlockSpec((1, d), lambda i: (0, 0)),
                pl.BlockSpec((d, n), lambda i: (0, 0))]
    args = [x2, gain.reshape(1, d), w_bf16]
    out_shape = [jax.ShapeDtypeStruct((t, n), BF16)]
    out_specs = [pl.BlockSpec((tm, n), lambda i: (i, 0))]
    scratch = []
    if has_gate:
        wg_hi = w_gate_f32.astype(BF16)
        wg_lo = (w_gate_f32 - wg_hi.astype(F32)).astype(BF16)
        in_specs.append(pl.BlockSpec((d, 2 * LANES), lambda i: (0, 0)))
        args.append(jnp.concatenate([wg_hi, wg_lo], axis=1))
        out_shape.append(jax.ShapeDtypeStruct((t, LANES), F32))
        out_specs.append(pl.BlockSpec((tm, LANES), lambda i: (i, 0)))
    if n_perm:
        in_specs.append(pl.BlockSpec((perm_w, perm_w), lambda i: (0, 0)))
        args.append(perm)
        scratch.append(pltpu.VMEM((d, n_perm * perm_w), BF16))
    if conv_cols:
        cw = conv_cols[1] - conv_cols[0]
        in_specs.append(pl.BlockSpec((CONV_K, cw), lambda i: (0, 0)))
        args.append(conv_w.astype(F32))
        scratch.append(pltpu.VMEM((tm + 8, cw), F32))
    res = pl.pallas_call(
        functools.partial(_norm_proj_kernel, tn=tn, has_gate=has_gate, n_perm=n_perm, perm_w=perm_w,
                          conv_cols=conv_cols, seq_tiles=seq_tiles),
        grid=(t // tm,), in_specs=in_specs, out_specs=out_specs, out_shape=out_shape, scratch_shapes=scratch,
        compiler_params=_cparams("arbitrary"), name="norm_proj")(*args)
    return res if has_gate else res[0]


def _out_proj_kernel(*refs, n_in, route):
    res_ref = refs[0]
    a_refs = refs[1:1 + n_in]
    w_refs = refs[1 + n_in:1 + 2 * n_in]
    rest = refs[1 + 2 * n_in:]
    acc = res_ref[...]
    for a_ref, w_ref in zip(a_refs, w_refs):
        acc = acc + jnp.dot(a_ref[...], w_ref[...], preferred_element_type=F32)
    if route:
        g_ref, wr_ref, wrlo_ref, br_ref, o_ref, eid_ref, gate_ref = rest
        _route_tile(acc, g_ref, wr_ref, wrlo_ref, br_ref, eid_ref, gate_ref)
    else:
        (o_ref,) = rest
    o_ref[...] = acc


def _out_proj(res, acts, ws, router=None, *, tm=512):
    t, d = res.shape
    n_in = len(acts)
    in_specs = [pl.BlockSpec((tm, d), lambda i: (i, 0))]
    in_specs += [pl.BlockSpec((tm, a.shape[1]), lambda i: (i, 0)) for a in acts]
    in_specs += [pl.BlockSpec(w.shape, lambda i: (0, 0)) for w in ws]
    out_specs = [pl.BlockSpec((tm, d), lambda i: (i, 0))]
    out_shape = [jax.ShapeDtypeStruct((t, d), F32)]
    extra = ()
    if router is not None:
        extra = tuple(router)
        in_specs += [pl.BlockSpec(r.shape, lambda i: (0, 0)) for r in extra]
        out_specs += [pl.BlockSpec((8, tm), lambda i: (0, i))] * 2
        out_shape += [jax.ShapeDtypeStruct((8, t), jnp.int32), jax.ShapeDtypeStruct((8, t), F32)]
    out = pl.pallas_call(
        functools.partial(_out_proj_kernel, n_in=n_in, route=router is not None),
        grid=(t // tm,), in_specs=in_specs, out_specs=out_specs, out_shape=out_shape,
        compiler_params=_cparams("parallel"), name="out_proj")(res, *acts, *ws, *extra)
    return out if router is not None else out[0]


def _retention_kernel(q_ref, k_ref, v_ref, z_ref, cos_ref, sin_ref, decay_ref, xi_ref, zeta_ref, gam_ref,
                      o_ref, state_ref, *, c, n_chunks):
    half = RET_QK_DIM // 2

    @pl.when(pl.program_id(2) == 0)
    def _():
        state_ref[...] = jnp.zeros_like(state_ref)

    cos = cos_ref[...]
    sin = sin_ref[...]

    def rotate(t):
        a = t[:, :half].astype(F32)
        b = t[:, half:].astype(F32)
        return jnp.concatenate([a * cos - b * sin, b * cos + a * sin], axis=-1)

    q = rotate(q_ref[0])
    k = rotate(k_ref[0]) * (RET_QK_DIM ** -0.5)
    decay = decay_ref[0]
    gamma_c = gam_ref[0, 0:1, 0:1]
    xi = xi_ref[0][:, 0:1]
    zeta = zeta_ref[0][:, 0:1]
    chunks = range(n_chunks)
    rows = [slice(ci * c, (ci + 1) * c) for ci in chunks]
    qb = [q[r].astype(BF16) for r in rows]
    kb = [k[r].astype(BF16) for r in rows]
    kz = [(k[r] * zeta).astype(BF16) for r in rows]
    vc = [v_ref[0, r, :] for r in rows]
    inner = [lax.dot_general(qb[i], kb[i], (((1,), (1,)), ((), ())), preferred_element_type=F32) * decay
             for i in chunks]
    kv = [lax.dot_general(kz[i], vc[i], (((0,), (0,)), ((), ())), preferred_element_type=F32) for i in chunks]
    intra = [jnp.dot(inner[i].astype(BF16), vc[i], preferred_element_type=F32) for i in chunks]
    states = [state_ref[...]]
    for i in chunks:
        states.append(states[i] * gamma_c + kv[i])
    state_ref[...] = states[-1]
    cross = [jnp.dot(qb[i], states[i].astype(BF16), preferred_element_type=F32) for i in chunks]
    for i in chunks:
        out = intra[i] + cross[i] * xi
        o = out * lax.rsqrt(jnp.mean(out * out, axis=-1, keepdims=True) + EPS)
        z = z_ref[0, rows[i], :].astype(F32)
        o_ref[0, rows[i], :] = (_silu(z) * o).astype(o_ref.dtype)


RET_KERNEL_CHUNK = 256


def _retention(p, bsz, s, *, blk=1024):
    h = RET_HEADS
    c = RET_KERNEL_CHUNK
    half = RET_QK_DIM // 2
    pos = jnp.arange(s, dtype=F32)
    freq = 1.0 / (XPOS_THETA ** jnp.linspace(0.0, 1.0, half, dtype=F32))
    ang = pos[:, None] * freq[None, :]
    cos, sin = jnp.cos(ang), jnp.sin(ang)
    log_gamma = jnp.log(1.0 - 2.0 ** (-5.0 - jnp.arange(h, dtype=F32)))
    idx = jnp.arange(c, dtype=F32)
    rel = idx[:, None] - idx[None, :]
    decay = jnp.exp(jnp.where(rel >= 0, rel * log_gamma[:, None, None], NEG_INF))
    xi = jnp.broadcast_to(jnp.exp((idx + 1.0) * log_gamma[:, None])[..., None], (h, c, LANES))
    zeta = jnp.broadcast_to(jnp.exp((c - 1.0 - idx) * log_gamma[:, None])[..., None], (h, c, LANES))
    gam = jnp.broadcast_to(jnp.exp(c * log_gamma)[:, None, None], (h, 8, LANES))
    nq = RET_QK_W // RET_QK_DIM
    nv = RET_V_W // RET_V_DIM
    return pl.pallas_call(
        functools.partial(_retention_kernel, c=c, n_chunks=blk // c),
        grid=(bsz, h, s // blk),
        in_specs=[
            pl.BlockSpec((1, blk, RET_QK_DIM), lambda b, hh, i: (b, i, hh)),
            pl.BlockSpec((1, blk, RET_QK_DIM), lambda b, hh, i: (b, i, nq + hh)),
            pl.BlockSpec((1, blk, RET_V_DIM), lambda b, hh, i: (b, i, nv + hh)),
            pl.BlockSpec((1, blk, RET_V_DIM), lambda b, hh, i: (b, i, 2 * nv + hh)),
            pl.BlockSpec((blk, half), lambda b, hh, i: (i, 0)),
            pl.BlockSpec((blk, half), lambda b, hh, i: (i, 0)),
            pl.BlockSpec((1, c, c), lambda b, hh, i: (hh, 0, 0)),
            pl.BlockSpec((1, c, LANES), lambda b, hh, i: (hh, 0, 0)),
            pl.BlockSpec((1, c, LANES), lambda b, hh, i: (hh, 0, 0)),
            pl.BlockSpec((1, 8, LANES), lambda b, hh, i: (hh, 0, 0)),
        ],
        out_specs=pl.BlockSpec((1, blk, RET_V_DIM), lambda b, hh, i: (b, i, hh)),
        out_shape=jax.ShapeDtypeStruct((bsz, s, RET_V_W), BF16),
        scratch_shapes=[pltpu.VMEM((RET_QK_DIM, RET_V_DIM), F32)],
        compiler_params=_cparams("parallel", "parallel", "arbitrary"), name="retention",
    )(p, p, p, p, cos, sin, decay, xi, zeta, gam)


def _swa_kernel(sink_ref, q_ref, kc_ref, kp_ref, vc_ref, vp_ref, cos_ref, sin_ref, cosp_ref, sinp_ref,
                qg_ref, kg_ref, bd_ref, rp_ref, o_ref, *, nq):
    w = WINDOW
    dh = SWA_HEAD_DIM
    grp = SWA_Q_HEADS // SWA_KV_HEADS
    n = pl.program_id(1)

    def head_rms(x, gain, bd):
        sq = x * x
        hi = sq.astype(BF16)
        lo = (sq - hi.astype(F32)).astype(BF16)
        ssum = jnp.dot(hi, bd, preferred_element_type=F32) + jnp.dot(lo, bd, preferred_element_type=F32)
        return x * lax.rsqrt(ssum * (1.0 / dh) + EPS) * gain

    def rope(x, cos, sin, rp):
        partner = jnp.dot(x.astype(BF16), rp, preferred_element_type=F32)
        return x * cos + partner * sin

    cos, sin = cos_ref[...], sin_ref[...]
    cosq = jnp.concatenate([cos] * (SWA_Q_W // LANES), axis=1)
    sinq = jnp.concatenate([sin] * (SWA_Q_W // LANES), axis=1)
    q = rope(head_rms(q_ref[0].astype(F32), qg_ref[...], bd_ref[...]), cosq, sinq, rp_ref[...])
    q = (q * (dh ** -0.5)).astype(BF16)
    bdk = bd_ref[0:SWA_KV_W, 0:SWA_KV_W]
    rpk = rp_ref[0:SWA_KV_W, 0:SWA_KV_W]
    kcur = rope(head_rms(kc_ref[0].astype(F32), kg_ref[...], bdk), cos, sin, rpk)
    kprev = rope(head_rms(kp_ref[0].astype(F32), kg_ref[...], bdk), cosp_ref[...], sinp_ref[...], rpk)
    kall = jnp.concatenate([kprev, kcur], axis=0).astype(BF16)
    vall = jnp.concatenate([vp_ref[0], vc_ref[0]], axis=0)

    i = lax.broadcasted_iota(jnp.int32, (w, 2 * w), 0)
    j = lax.broadcasted_iota(jnp.int32, (w, 2 * w), 1)
    band = (j > i) & (j <= i + w)
    masks = [band & ((j >= w) | (n > 0))] + [band] * (nq - 1)
    units = [(qb, hq) for qb in range(nq) for hq in range(SWA_Q_HEADS)]
    kh = {(qb, hk): kall[qb * w:(qb + 2) * w, hk * dh:(hk + 1) * dh] for qb in range(nq) for hk in range(SWA_KV_HEADS)}
    vh = {(qb, hk): vall[qb * w:(qb + 2) * w, hk * dh:(hk + 1) * dh] for qb in range(nq) for hk in range(SWA_KV_HEADS)}
    sc = [lax.dot_general(q[qb * w:(qb + 1) * w, hq * dh:(hq + 1) * dh], kh[qb, hq // grp], (((1,), (1,)), ((), ())),
                          preferred_element_type=F32) for qb, hq in units]
    sc = [jnp.where(masks[qb], sc[u], NEG_INF) for u, (qb, hq) in enumerate(units)]
    m = [jnp.maximum(jnp.max(sc[u], axis=-1, keepdims=True), sink_ref[hq]) for u, (qb, hq) in enumerate(units)]
    p = [jnp.exp(sc[u] - m[u]) for u in range(len(units))]
    denom = [jnp.sum(p[u], axis=-1, keepdims=True) + jnp.exp(sink_ref[hq] - m[u]) for u, (qb, hq) in enumerate(units)]
    pv = [jnp.dot(p[u].astype(BF16), vh[qb, hq // grp], preferred_element_type=F32) for u, (qb, hq) in enumerate(units)]
    for qb in range(nq):
        row = [pv[qb * SWA_Q_HEADS + hq] * (1.0 / denom[qb * SWA_Q_HEADS + hq]) for hq in range(SWA_Q_HEADS)]
        o_ref[0, qb * w:(qb + 1) * w, :] = jnp.concatenate(row, axis=-1).astype(o_ref.dtype)


def _rope_tables(s):
    half = ROT_DIM // 2
    pos = jnp.arange(s, dtype=F32)
    inv_freq = 1.0 / (ROPE_THETA ** (jnp.arange(half, dtype=F32) * 2.0 / ROT_DIM))
    ang = pos[:, None] * inv_freq[None, :]
    cos, sin = jnp.cos(ang), jnp.sin(ang)
    ones = jnp.ones((s, SWA_HEAD_DIM - ROT_DIM), F32)
    cos_h = jnp.concatenate([cos, cos, ones], axis=1)
    sin_h = jnp.concatenate([-sin, sin, 0.0 * ones], axis=1)
    reps = LANES // SWA_HEAD_DIM
    return jnp.tile(cos_h, (1, reps)), jnp.tile(sin_h, (1, reps))


def _swa(p, sinks, q_gain, k_gain, bsz, s, *, nq=4):
    w = WINDOW
    blk = nq * w
    cos, sin = _rope_tables(s)
    qg = jnp.tile(q_gain.astype(F32), SWA_Q_HEADS).reshape(1, SWA_Q_W)
    kg = jnp.tile(k_gain.astype(F32), SWA_KV_HEADS).reshape(1, SWA_KV_W)
    lanes = np.arange(SWA_Q_W)
    seg = lanes // SWA_HEAD_DIM
    bd = jnp.asarray(seg[:, None] == seg[None, :], dtype=BF16)
    half = ROT_DIM // 2
    in_head = lanes % SWA_HEAD_DIM
    src = np.where(in_head < half, lanes + half, lanes - half)
    rp = np.zeros((SWA_Q_W, SWA_Q_W), np.float32)
    rot = in_head < ROT_DIM
    rp[src[rot], lanes[rot]] = 1.0
    rp = jnp.asarray(rp, dtype=BF16)
    qblk, kblk, vblk = EVEN_COL_Q // SWA_Q_W, EVEN_COL_K // SWA_KV_W, EVEN_COL_V // SWA_KV_W
    prev = lambda i: jnp.maximum(i * nq - 1, 0)
    return pl.pallas_call(
        functools.partial(_swa_kernel, nq=nq),
        grid=(bsz, s // blk),
        in_specs=[
            pl.BlockSpec(memory_space=pltpu.SMEM),
            pl.BlockSpec((1, blk, SWA_Q_W), lambda b, i: (b, i, qblk)),
            pl.BlockSpec((1, blk, SWA_KV_W), lambda b, i: (b, i, kblk)),
            pl.BlockSpec((1, w, SWA_KV_W), lambda b, i: (b, prev(i), kblk)),
            pl.BlockSpec((1, blk, SWA_KV_W), lambda b, i: (b, i, vblk)),
            pl.BlockSpec((1, w, SWA_KV_W), lambda b, i: (b, prev(i), vblk)),
            pl.BlockSpec((blk, LANES), lambda b, i: (i, 0)),
            pl.BlockSpec((blk, LANES), lambda b, i: (i, 0)),
            pl.BlockSpec((w, LANES), lambda b, i: (prev(i), 0)),
            pl.BlockSpec((w, LANES), lambda b, i: (prev(i), 0)),
            pl.BlockSpec((1, SWA_Q_W), lambda b, i: (0, 0)),
            pl.BlockSpec((1, SWA_KV_W), lambda b, i: (0, 0)),
            pl.BlockSpec((SWA_Q_W, SWA_Q_W), lambda b, i: (0, 0)),
            pl.BlockSpec((SWA_Q_W, SWA_Q_W), lambda b, i: (0, 0)),
        ],
        out_specs=pl.BlockSpec((1, blk, SWA_Q_W), lambda b, i: (b, i, 0)),
        out_shape=jax.ShapeDtypeStruct((bsz, s, SWA_Q_W), BF16),
        compiler_params=_cparams("parallel", "parallel"), name="swa",
    )(sinks.astype(F32), p, p, p, p, p, cos, sin, cos, sin, qg, kg, bd, rp)


def _softplus(x):
    return jnp.maximum(x, 0.0) + jnp.log1p(jnp.exp(-jnp.abs(x)))


def _split3(x):
    x1 = x.astype(BF16)
    r1 = x - x1.astype(F32)
    x2 = r1.astype(BF16)
    return x1, x2, (r1 - x2.astype(F32)).astype(BF16)


def _gdn_kernel(q_ref, k_ref, v_ref, z_ref, gate_ref, alog_ref, dt_ref, gain_ref, o_ref, state_ref, *, blk):
    c = GDN_CHUNK
    dk = GDN_HEAD_DIM
    nh = GDN_HEADS
    sub = LANES
    nt = (((1,), (1,)), ((), ()))

    @pl.when(pl.program_id(1) == 0)
    def _():
        state_ref[...] = jnp.zeros_like(state_ref)

    gates = gate_ref[0]
    g_all = -jnp.exp(alog_ref[...]) * _softplus(gates + dt_ref[...])
    beta_all = 1.0 / (1.0 + jnp.exp(-gates))
    ii = lax.broadcasted_iota(jnp.int32, (sub, sub), 0)
    jj = lax.broadcasted_iota(jnp.int32, (sub, sub), 1)
    same = (ii // c) == (jj // c)
    incl = same & (ii >= jj)
    strict = same & (ii > jj)
    tri = jnp.where(incl, 1.0, 0.0).astype(BF16)
    eye = jnp.where(ii == jj, 1.0, 0.0).astype(F32)
    er = lax.broadcasted_iota(jnp.int32, (8, LANES), 0)
    ec = lax.broadcasted_iota(jnp.int32, (8, LANES), 1)
    eye8 = jnp.where(er == ec, 1.0, 0.0).astype(BF16)
    g_parts = _split3(g_all)
    g_rows = sum(lax.dot_general(eye8, gp, nt, preferred_element_type=F32) for gp in g_parts)
    gr_parts = _split3(g_rows)
    gc_cols, gc_rows = [], []
    for sb in range(blk // sub):
        rows = slice(sb * sub, (sb + 1) * sub)
        gc_cols.append(sum(jnp.dot(tri, gp[rows], preferred_element_type=F32) for gp in g_parts))
        gc_rows.append(sum(lax.dot_general(gp[:, rows], tri, nt, preferred_element_type=F32) for gp in gr_parts))

    n_sub = blk // sub
    units = [(hd, sb) for hd in range(nh) for sb in range(n_sub)]
    dot = functools.partial(jnp.dot, preferred_element_type=F32)

    def dot_nt(x, y):
        return lax.dot_general(x, y, nt, preferred_element_type=F32)

    qs, ks, vs, betas = [], [], [], []
    for hd in range(nh):
        cols = slice(hd * dk, (hd + 1) * dk)
        q = q_ref[0, :, cols].astype(F32)
        k = k_ref[0, :, cols].astype(F32)
        qs.append(q * lax.rsqrt(jnp.sum(q * q, axis=-1, keepdims=True) + EPS) * (dk ** -0.5))
        ks.append(k * lax.rsqrt(jnp.sum(k * k, axis=-1, keepdims=True) + EPS))
        vs.append(v_ref[0, :, cols].astype(F32))
        betas.append(beta_all[:, nh + hd:nh + hd + 1])

    def rows_of(sb):
        return slice(sb * sub, (sb + 1) * sub)

    gcb = [jnp.broadcast_to(gc_cols[sb][:, hd:hd + 1], (sub, LANES)) for hd, sb in units]
    decay = [jnp.exp(jnp.where(incl, gcb[u] - gc_rows[sb][hd:hd + 1, :], NEG_INF)) for u, (hd, sb) in enumerate(units)]
    eg = [jnp.exp(g) for g in gcb]
    q_u = [qs[hd][rows_of(sb)] for hd, sb in units]
    k_u = [ks[hd][rows_of(sb)] for hd, sb in units]
    kb = [k_u[u] * betas[hd][rows_of(sb)] for u, (hd, sb) in enumerate(units)]
    kbf = [x.astype(BF16) for x in k_u]
    a = [dot_nt(kb[u].astype(BF16), kbf[u]) for u in range(len(units))]
    m = [jnp.where(strict, -(a[u] * decay[u]), 0.0) for u in range(len(units))]
    r = [eye + x for x in m]
    mb = [x.astype(BF16) for x in m]
    pw = [dot(x, x) for x in mb]
    for _ in range(int(np.log2(c)) - 2):
        pb = [x.astype(BF16) for x in pw]
        both = [dot(jnp.concatenate([r[u].astype(BF16), pb[u]], axis=0), pb[u]) for u in range(len(units))]
        r = [r[u] + both[u][:sub] for u in range(len(units))]
        pw = [x[sub:] for x in both]
    r = [r[u] + dot(r[u].astype(BF16), pw[u].astype(BF16)) for u in range(len(units))]
    rhs = [jnp.concatenate([kb[u] * eg[u], vs[hd][rows_of(sb)] * betas[hd][rows_of(sb)]], axis=1).astype(BF16)
           for u, (hd, sb) in enumerate(units)]
    wu = [dot(r[u].astype(BF16), rhs[u]).astype(BF16) for u in range(len(units))]
    qk = [dot_nt(q_u[u].astype(BF16), kbf[u]) for u in range(len(units))]
    iwu = [dot((qk[u] * decay[u]).astype(BF16), wu[u]) for u in range(len(units))]
    qe = [(q_u[u] * eg[u] - iwu[u][:, :dk]).astype(BF16) for u in range(len(units))]
    per_chunk = sub // c
    kwu, g_last = {}, {}
    for u, (hd, sb) in enumerate(units):
        for ci in range(per_chunk):
            crow = slice(ci * c, (ci + 1) * c)
            gl = gcb[u][(ci + 1) * c - 1:(ci + 1) * c, :]
            k_dec = (k_u[u][crow] * jnp.exp(gl - gcb[u][crow])).astype(BF16)
            g_last[hd, sb * per_chunk + ci] = jnp.exp(gl)
            kwu[hd, sb * per_chunk + ci] = lax.dot_general(k_dec, wu[u][crow], (((0,), (0,)), ((), ())),
                                                           preferred_element_type=F32)
    outs = [[] for _ in range(nh)]
    for ch in range(blk // c):
        u_sb, ci = ch // per_chunk, ch % per_chunk
        crow = slice(ci * c, (ci + 1) * c)
        states = [state_ref[hd] for hd in range(nh)]
        sbf = [x.astype(BF16) for x in states]
        for hd in range(nh):
            u = hd * n_sub + u_sb
            outs[hd].append(dot(qe[u][crow], sbf[hd]) + iwu[u][crow, dk:])
        corr = [dot(kwu[hd, ch][:, :dk].astype(BF16), sbf[hd]) for hd in range(nh)]
        for hd in range(nh):
            state_ref[hd] = states[hd] * g_last[hd, ch] + (kwu[hd, ch][:, dk:] - corr[hd])
    for hd in range(nh):
        out = jnp.concatenate(outs[hd], axis=0)
        o = out * lax.rsqrt(jnp.mean(out * out, axis=-1, keepdims=True) + EPS) * gain_ref[...]
        z = z_ref[0, :, hd * dk:(hd + 1) * dk].astype(F32)
        o_ref[0, :, hd * dk:(hd + 1) * dk] = (o * _silu(z)).astype(o_ref.dtype)


def _gdn(p, pg, a_log, dt_bias, gdn_gain, bsz, s, *, blk=512):
    dk = GDN_HEAD_DIM
    assert EVEN_COL_GDN % GDN_W == 0 and EVEN_COL_Z % GDN_W == 0
    lane_pad = (0, LANES - GDN_HEADS)
    alog = jnp.pad(a_log.astype(F32), lane_pad).reshape(1, LANES)
    dt = jnp.pad(dt_bias.astype(F32), lane_pad).reshape(1, LANES)
    return pl.pallas_call(
        functools.partial(_gdn_kernel, blk=blk),
        grid=(bsz, s // blk),
        in_specs=[pl.BlockSpec((1, blk, GDN_W), lambda b, i: (b, i, EVEN_COL_GDN // GDN_W)),
                  pl.BlockSpec((1, blk, GDN_W), lambda b, i: (b, i, EVEN_COL_GDN // GDN_W + 1)),
                  pl.BlockSpec((1, blk, GDN_W), lambda b, i: (b, i, EVEN_COL_GDN // GDN_W + 2)),
                  pl.BlockSpec((1, blk, GDN_W), lambda b, i: (b, i, EVEN_COL_Z // GDN_W)),
                  pl.BlockSpec((1, blk, LANES), lambda b, i: (b, i, 0)),
                  pl.BlockSpec((1, LANES), lambda b, i: (0, 0)),
                  pl.BlockSpec((1, LANES), lambda b, i: (0, 0)),
                  pl.BlockSpec((1, dk), lambda b, i: (0, 0))],
        out_specs=pl.BlockSpec((1, blk, GDN_W), lambda b, i: (b, i, 0)),
        out_shape=jax.ShapeDtypeStruct((bsz, s, GDN_W), BF16),
        scratch_shapes=[pltpu.VMEM((GDN_HEADS, dk, dk), F32)],
        compiler_params=_cparams("parallel", "arbitrary"), name="gdn",
    )(p, p, p, p, pg, alog, dt, gdn_gain.astype(F32).reshape(1, dk))


MOE_BM = 512
ROUTER_ROWS = LANES
EXPERT_ROW0 = 8


SLAB = 8
DMA_UNROLL = 4


def _store_slabs(ref, val, rows):
    for sl in range(SLAB):
        ref[pl.ds(sl, rows, stride=SLAB), :] = val[:, sl * LANES:(sl + 1) * LANES]


def _load_slabs(ref, rows):
    return jnp.concatenate([ref[pl.ds(sl, rows, stride=SLAB), :] for sl in range(SLAB)], axis=1)


def _slab(ref, row):
    return ref.at[pl.ds(pl.multiple_of(row * SLAB, SLAB), SLAB), :]


def _ffn_norm(x, gain):
    return x * lax.rsqrt(jnp.mean(x * x, axis=-1, keepdims=True) + EPS) * gain


def _route_tile(x, g_ref, wr_ref, wrlo_ref, br_ref, eid_ref, gate_ref):
    h = _ffn_norm(x, g_ref[...])
    tm = h.shape[0]
    h_hi = h.astype(BF16)
    h_lo = (h - h_hi.astype(F32)).astype(BF16)
    nt = (((1,), (1,)), ((), ()))
    logits = (lax.dot_general(wr_ref[...], h_hi, nt, preferred_element_type=F32)
              + lax.dot_general(wr_ref[...], h_lo, nt, preferred_element_type=F32)
              + lax.dot_general(wrlo_ref[...], h_hi, nt, preferred_element_type=F32)) + br_ref[...]
    row = lax.broadcasted_iota(jnp.int32, (8, tm), 0)
    gl = jnp.where(row < N_GROUPS, logits[0:8, :], NEG_INF)
    gmax = jnp.max(gl, axis=0, keepdims=True)
    grp_p = 1.0 / jnp.sum(jnp.exp(gl - gmax), axis=0, keepdims=True)
    grp_idx = jnp.min(jnp.where(gl == gmax, row, 8), axis=0, keepdims=True)
    el = jnp.zeros((8, tm), F32)
    for g in range(N_GROUPS):
        lo = EXPERT_ROW0 + g * EXPERTS_PER_GROUP
        el = jnp.where(grp_idx == g, logits[lo:lo + EXPERTS_PER_GROUP, :], el)
    m1 = jnp.max(el, axis=0, keepdims=True)
    i1 = jnp.min(jnp.where(el == m1, row, 8), axis=0, keepdims=True)
    el2 = jnp.where(row == i1, NEG_INF, el)
    m2 = jnp.max(el2, axis=0, keepdims=True)
    i2 = jnp.min(jnp.where(el2 == m2, row, 8), axis=0, keepdims=True)
    z = jnp.sum(jnp.exp(el - m1), axis=0, keepdims=True)
    p1 = 1.0 / z
    p2 = jnp.exp(m2 - m1) / z
    scale = grp_p / (p1 + p2)
    e1 = grp_idx * EXPERTS_PER_GROUP + i1
    e2 = grp_idx * EXPERTS_PER_GROUP + i2
    eid_ref[...] = jnp.where(row == 0, e1, jnp.where(row == 1, e2, 0))
    gate_ref[...] = jnp.where(row == 0, p1 * scale, jnp.where(row == 1, p2 * scale, 0.0))


def _router_operands(gain, w_rg, b_rg, w_re, b_re):
    d = w_rg.shape[0]
    wr = jnp.zeros((ROUTER_ROWS, d), F32)
    wr = wr.at[0:N_GROUPS].set(w_rg.astype(F32).T).at[EXPERT_ROW0:EXPERT_ROW0 + N_EXPERTS].set(w_re.astype(F32).T)
    br = jnp.zeros((ROUTER_ROWS, 1), F32)
    br = br.at[0:N_GROUPS, 0].set(b_rg.astype(F32)).at[EXPERT_ROW0:EXPERT_ROW0 + N_EXPERTS, 0].set(b_re.astype(F32))
    wr_hi = wr.astype(BF16)
    wr_lo = (wr - wr_hi.astype(F32)).astype(BF16)
    return gain.astype(F32).reshape(1, d), wr_hi, wr_lo, br


def _rank_kernel(eid_ref, dest_ref, meta_ref, cnt_ref, off_ref, *, bm):
    ph = pl.program_id(0)
    i = pl.program_id(1)
    tm = eid_ref.shape[1]
    erow = lax.broadcasted_iota(jnp.int32, (N_EXPERTS, tm), 0)
    oh0 = jnp.where(erow == eid_ref[0:1, :], 1.0, 0.0).astype(F32)
    oh1 = jnp.where(erow == eid_ref[1:2, :], 1.0, 0.0).astype(F32)
    oh = oh0 + oh1
    tile_cnt = jnp.sum(oh, axis=1, keepdims=True)

    @pl.when((ph == 0) & (i == 0))
    def _():
        cnt_ref[...] = jnp.zeros_like(cnt_ref)

    @pl.when(ph == 0)
    def _():
        cnt_ref[...] += tile_cnt

    @pl.when((ph == 1) & (i == 0))
    def _():
        cnt = cnt_ref[...]
        padded = jnp.ceil(cnt * (1.0 / bm)) * bm
        er = lax.broadcasted_iota(jnp.int32, (N_EXPERTS, N_EXPERTS), 0)
        ec = lax.broadcasted_iota(jnp.int32, (N_EXPERTS, N_EXPERTS), 1)
        lower = jnp.where(ec < er, 1.0, 0.0).astype(F32)
        start = jnp.dot(lower, padded, preferred_element_type=F32, precision=lax.Precision.HIGHEST)
        off_ref[...] = start
        end = start + padded
        nb = meta_ref.shape[1]
        blk_row = lax.broadcasted_iota(jnp.int32, (N_EXPERTS, nb), 1).astype(F32) * bm
        endb = jnp.concatenate([end] * (nb // LANES), axis=1)
        be = jnp.sum(jnp.where(endb <= blk_row, 1.0, 0.0), axis=0, keepdims=True)
        be = jnp.minimum(be, N_EXPERTS - 1.0)
        used = jnp.concatenate([end[N_EXPERTS - 1:N_EXPERTS, :]] * (nb // LANES), axis=1) * (1.0 / bm)
        elane = lax.broadcasted_iota(jnp.int32, (N_EXPERTS, nb), 1)
        erow2 = lax.broadcasted_iota(jnp.int32, (N_EXPERTS, nb), 0)
        ends = jnp.sum(jnp.where(elane == erow2, endb, 0.0), axis=0, keepdims=True)
        mrow = lax.broadcasted_iota(jnp.int32, (8, nb), 0)
        meta_ref[...] = jnp.where(mrow == 0, be, jnp.where(mrow == 1, used, jnp.where(mrow == 2, ends, 0.0))
                                  ).astype(jnp.int32)
        cnt_ref[...] = jnp.zeros_like(cnt_ref)

    @pl.when(ph == 1)
    def _():
        kr = lax.broadcasted_iota(jnp.int32, (tm, tm), 0)
        kc = lax.broadcasted_iota(jnp.int32, (tm, tm), 1)
        su = jnp.where(kr < kc, 1.0, 0.0).astype(BF16)
        csum = jnp.dot(oh.astype(BF16), su, preferred_element_type=F32)
        pos = csum + (off_ref[...] + cnt_ref[...])[:, 0:1]
        d0 = jnp.sum(oh0 * pos, axis=0, keepdims=True)
        d1 = jnp.sum(oh1 * pos, axis=0, keepdims=True)
        row = lax.broadcasted_iota(jnp.int32, (8, tm), 0)
        dest_ref[...] = jnp.where(row == 0, d0, jnp.where(row == 1, d1, 0.0)).astype(jnp.int32)
        cnt_ref[...] += tile_cnt


def _moe_rank(eid, n_blocks, *, bm, tm=1024):
    t = eid.shape[1]
    nb = -(-n_blocks // LANES) * LANES
    return pl.pallas_call(
        functools.partial(_rank_kernel, bm=bm), grid=(2, t // tm),
        in_specs=[pl.BlockSpec((8, tm), lambda p, i: (0, i))],
        out_specs=[pl.BlockSpec((8, tm), lambda p, i: (0, i * p)),
                   pl.BlockSpec((8, nb), lambda p, i: (0, 0))],
        out_shape=[jax.ShapeDtypeStruct((8, t), jnp.int32), jax.ShapeDtypeStruct((8, nb), jnp.int32)],
        scratch_shapes=[pltpu.VMEM((N_EXPERTS, LANES), F32), pltpu.VMEM((N_EXPERTS, LANES), F32)],
        compiler_params=_cparams("arbitrary", "arbitrary"), name="moe_rank")(eid)


def _dispatch_kernel(dest_ref, ends_ref, x_ref, g_ref, rows_hbm, zero_ref, hbuf_ref, sem, rsem, *, tm, bm):
    @pl.when(pl.program_id(0) == 0)
    def _():
        zero_ref[...] = jnp.zeros_like(zero_ref)

        def tail(e):
            start = ends_ref[0, e] - bm
            return pltpu.make_async_copy(zero_ref, rows_hbm.at[pl.ds(pl.multiple_of(start * SLAB, SLAB), bm * SLAB), :],
                                         sem)

        def nonempty(e):
            return ends_ref[0, e] > jnp.where(e > 0, ends_ref[0, jnp.maximum(e - 1, 0)], 0)

        def fill(e, carry):
            @pl.when(nonempty(e))
            def _():
                tail(e).start()
            return carry

        def drain(e, carry):
            @pl.when(nonempty(e))
            def _():
                tail(e).wait()
            return carry

        lax.fori_loop(0, N_EXPERTS, fill, 0)
        lax.fori_loop(0, N_EXPERTS, drain, 0)

        def unused(b):
            return pltpu.make_async_copy(
                zero_ref, rows_hbm.at[pl.ds(pl.multiple_of(b * (bm * SLAB), SLAB), bm * SLAB), :], sem)

        def fill_unused(b, carry):
            unused(b).start()
            return carry

        def drain_unused(b, carry):
            unused(b).wait()
            return carry

        first_unused = ends_ref[0, N_EXPERTS - 1] // bm
        n_blocks = rows_hbm.shape[0] // (bm * SLAB)
        lax.fori_loop(first_unused, n_blocks, fill_unused, 0)
        lax.fori_loop(first_unused, n_blocks, drain_unused, 0)

    i = pl.program_id(0)
    slot = lax.rem(i, 2)

    def drain_slot(s):
        for _ in range(TOP_K):
            pltpu.make_async_copy(hbuf_ref.at[s], rows_hbm.at[pl.ds(0, tm * SLAB), :], rsem.at[s]).wait()

    for s in range(2):
        @pl.when((slot == s) & (i >= 2))
        def _():
            drain_slot(s)

        @pl.when(slot == s)
        def _():
            _store_slabs(hbuf_ref.at[s], _ffn_norm(x_ref[...], g_ref[...]), tm)

            def issue(g, carry):
                for u in range(DMA_UNROLL):
                    tok = g * DMA_UNROLL + u
                    for k in range(TOP_K):
                        pltpu.make_async_copy(_slab(hbuf_ref.at[s], tok),
                                              _slab(rows_hbm, dest_ref[0, 0, k * tm + tok]),
                                              rsem.at[s]).start(priority=k)
                return carry

            lax.fori_loop(0, tm // DMA_UNROLL, issue, 0)

    last = pl.num_programs(0) - 1

    @pl.when(i == last)
    def _():
        for s in range(2):
            @pl.when((slot == s) | (last >= 1))
            def _():
                drain_slot(s)


def _moe_dispatch(dest_tiles, expert_ends, x2, gain, n_rows, *, tm, bm):
    t, d = x2.shape
    return pl.pallas_call(
        functools.partial(_dispatch_kernel, tm=tm, bm=bm), grid=(t // tm,),
        in_specs=[pl.BlockSpec((1, 1, TOP_K * tm), lambda i: (i, 0, 0), memory_space=pltpu.SMEM),
                  pl.BlockSpec(memory_space=pltpu.SMEM),
                  pl.BlockSpec((tm, d), lambda i: (i, 0)),
                  pl.BlockSpec((1, d), lambda i: (0, 0))],
        out_specs=pl.BlockSpec(memory_space=pl.ANY),
        out_shape=jax.ShapeDtypeStruct((n_rows * SLAB, LANES), F32),
        scratch_shapes=[pltpu.VMEM((bm * SLAB, LANES), F32), pltpu.VMEM((2, tm * SLAB, LANES), F32),
                        pltpu.SemaphoreType.DMA(()), pltpu.SemaphoreType.DMA((2,))],
        compiler_params=pltpu.CompilerParams(dimension_semantics=("arbitrary",), has_side_effects=True,
                                             vmem_limit_bytes=VMEM_LIMIT),
        name="moe_dispatch")(dest_tiles, expert_ends, x2, gain.reshape(1, d))


def _experts_kernel(be_ref, used_ref, x_ref, wg_ref, wu_ref, wd_ref, y_ref, wgb_ref, wub_ref, wdb_ref):
    b = pl.program_id(0)
    prev = be_ref[jnp.maximum(b - 1, 0)]

    @pl.when((b == 0) | (be_ref[b] != prev))
    def _():
        wgb_ref[...] = wg_ref[0, 0].astype(BF16)
        wub_ref[...] = wu_ref[0, 0].astype(BF16)
        wdb_ref[...] = wd_ref[0, 0].astype(BF16)

    bm = x_ref.shape[0] // SLAB

    @pl.when(b < used_ref[0])
    def _():
        x = _load_slabs(x_ref, bm).astype(BF16)
        gate = jnp.dot(x, wgb_ref[...], preferred_element_type=F32)
        up = jnp.dot(x, wub_ref[...], preferred_element_type=F32)
        hid = (_silu(gate) * up).astype(BF16)
        _store_slabs(y_ref, jnp.dot(hid, wdb_ref[...], preferred_element_type=F32), bm)

    @pl.when(b >= used_ref[0])
    def _():
        y_ref[...] = jnp.zeros_like(y_ref)


def _moe_experts(block_expert, n_used, x_rows, w_gate, w_up, w_down, layer, *, bm):
    n_rows = x_rows.shape[0] // SLAB
    d, de = w_gate.shape[2], w_gate.shape[3]
    grid_spec = pltpu.PrefetchScalarGridSpec(
        num_scalar_prefetch=2, grid=(n_rows // bm,),
        in_specs=[pl.BlockSpec((bm * SLAB, LANES),
                               lambda b, be, nu: (jnp.minimum(b, jnp.maximum(nu[0] - 1, 0)), 0)),
                  pl.BlockSpec((1, 1, d, de), lambda b, be, nu: (layer, be[b], 0, 0)),
                  pl.BlockSpec((1, 1, d, de), lambda b, be, nu: (layer, be[b], 0, 0)),
                  pl.BlockSpec((1, 1, de, d), lambda b, be, nu: (layer, be[b], 0, 0))],
        out_specs=pl.BlockSpec((bm * SLAB, LANES), lambda b, be, nu: (b, 0)),
        scratch_shapes=[pltpu.VMEM((d, de), BF16), pltpu.VMEM((d, de), BF16), pltpu.VMEM((de, d), BF16)])
    return pl.pallas_call(
        _experts_kernel, grid_spec=grid_spec, out_shape=jax.ShapeDtypeStruct((n_rows * SLAB, LANES), F32),
        compiler_params=_cparams("arbitrary"), name="moe_experts",
    )(block_expert, n_used, x_rows, w_gate, w_up, w_down)


def _combine_kernel(dest_ref, next_ref, x_ref, gate_ref, y_hbm, o_ref, buf_ref, sem, *, tm):
    i = pl.program_id(0)
    slot = lax.rem(i, 2)

    def gather(idx_ref, s):
        def issue(g, carry):
            for u in range(DMA_UNROLL):
                tok = g * DMA_UNROLL + u
                for k in range(TOP_K):
                    pltpu.make_async_copy(_slab(y_hbm, idx_ref[0, 0, k * tm + tok]), _slab(buf_ref.at[s, k], tok),
                                          sem.at[s]).start(priority=k)
            return carry

        lax.fori_loop(0, tm // DMA_UNROLL, issue, 0)

    for s in range(2):
        @pl.when((i == 0) & (slot == s))
        def _():
            gather(dest_ref, s)

        @pl.when((i + 1 < pl.num_programs(0)) & (slot == 1 - s))
        def _():
            gather(next_ref, s)

    er = lax.broadcasted_iota(jnp.int32, (8, LANES), 0)
    ec = lax.broadcasted_iota(jnp.int32, (8, LANES), 1)
    eye = jnp.where(er == ec, 1.0, 0.0).astype(F32)
    gcol = lax.dot_general(gate_ref[...], eye, (((0,), (0,)), ((), ())), preferred_element_type=F32,
                           precision=lax.Precision.HIGHEST)
    for s in range(2):
        @pl.when(slot == s)
        def _():
            for k in range(TOP_K):
                pltpu.make_async_copy(y_hbm.at[pl.ds(0, tm * SLAB), :], buf_ref.at[s, k], sem.at[s]).wait()
            o_ref[...] = (x_ref[...] + gcol[:, 0:1] * _load_slabs(buf_ref.at[s, 0], tm)
                          + gcol[:, 1:2] * _load_slabs(buf_ref.at[s, 1], tm))


def _moe_combine(dest_tiles, x2, gates, y_rows, *, tm):
    t, d = x2.shape
    n = t // tm
    dest_spec = functools.partial(pl.BlockSpec, (1, 1, TOP_K * tm), memory_space=pltpu.SMEM)
    return pl.pallas_call(
        functools.partial(_combine_kernel, tm=tm), grid=(n,),
        in_specs=[dest_spec(lambda i: (i, 0, 0)),
                  dest_spec(lambda i: (jnp.minimum(i + 1, n - 1), 0, 0)),
                  pl.BlockSpec((tm, d), lambda i: (i, 0)),
                  pl.BlockSpec((8, tm), lambda i: (0, i)),
                  pl.BlockSpec(memory_space=pl.ANY)],
        out_specs=pl.BlockSpec((tm, d), lambda i: (i, 0)),
        out_shape=jax.ShapeDtypeStruct((t, d), F32),
        scratch_shapes=[pltpu.VMEM((2, TOP_K, tm * SLAB, LANES), F32), pltpu.SemaphoreType.DMA((2,))],
        compiler_params=_cparams("arbitrary"), name="moe_combine")(dest_tiles, dest_tiles, x2, gates, y_rows)


def _moe(x2, eid, gates, gain, w_gate, w_up, w_down, layer, *, tm_dma=512):
    t, d = x2.shape
    bm = MOE_BM
    n_rows = t * TOP_K + N_EXPERTS * bm
    n_blocks = n_rows // bm
    dest, meta = _moe_rank(eid, n_blocks, bm=bm)
    dest_tiles = dest[0:TOP_K].reshape(TOP_K, t // tm_dma, tm_dma).transpose(1, 0, 2).reshape(
        t // tm_dma, 1, TOP_K * tm_dma)
    x_rows = _moe_dispatch(dest_tiles, meta[2:3, :N_EXPERTS], x2, gain, n_rows, tm=tm_dma, bm=bm)
    y_rows = _moe_experts(meta[0, :n_blocks], meta[1, 0:1], x_rows, w_gate, w_up, w_down, layer, bm=bm)
    return _moe_combine(dest_tiles, x2, gates, y_rows, tm=tm_dma)


def _even_odd_perm():
    half = RET_QK_DIM // 2
    src = np.concatenate([2 * np.arange(half), 2 * np.arange(half) + 1])
    perm = np.zeros((RET_QK_DIM, RET_QK_DIM), np.float32)
    perm[src, np.arange(RET_QK_DIM)] = 1.0
    return jnp.asarray(perm, dtype=BF16)


def kernel(x, ln_mix_even, w_in_even, conv_w_even, a_log_even, dt_bias_even, q_gain_even, k_gain_even, sinks_even, gdn_gain_even, w_out_even, ln_mix_odd, w_in_odd, w_out_odd, ln_ffn, w_router_group, b_router_group, w_router_expert, b_router_expert, w_gate, w_up, w_down):
    bsz, s, d = x.shape
    t = bsz * s
    x2 = x.reshape(t, d).astype(F32)
    depth = ln_ffn.shape[0]
    for layer in range(depth):
        i = layer // 2
        router = _router_operands(ln_ffn[layer], w_router_group[layer], b_router_group[layer],
                                  w_router_expert[layer], b_router_expert[layer])
        if layer % 2 == 0:
            w_in = w_in_even[i]
            swa_end = SWA_Q_W + 2 * SWA_KV_W
            w_main = jnp.concatenate([w_in[:, :SWA_Q_W], w_in[:, swa_end:EVEN_MAIN], w_in[:, SWA_Q_W:swa_end]],
                                     axis=1).astype(BF16)
            w_gates = jnp.pad(w_in[:, EVEN_MAIN:].astype(F32), ((0, 0), (0, LANES - 2 * GDN_HEADS)))
            p, pg = _norm_proj(x2, ln_mix_even[i], w_main, w_gates, conv_w=conv_w_even[i],
                               conv_cols=(EVEN_COL_GDN, EVEN_COL_Z), seq_len=s)
            p = p.reshape(bsz, s, EVEN_MAIN)
            y_a = _swa(p, sinks_even[i], q_gain_even[i], k_gain_even[i], bsz, s)
            y_b = _gdn(p, pg.reshape(bsz, s, LANES), a_log_even[i], dt_bias_even[i], gdn_gain_even[i], bsz, s)
            w_out = w_out_even[i].astype(BF16)
            x2, eid, gates = _out_proj(x2, [y_a.reshape(t, SWA_Q_W), y_b.reshape(t, GDN_W)],
                                       [w_out[:SWA_Q_W], w_out[SWA_Q_W:]], router)
        else:
            p = _norm_proj(x2, ln_mix_odd[i], w_in_odd[i].astype(BF16), perm=_even_odd_perm(),
                           n_perm=2 * RET_HEADS).reshape(bsz, s, ODD_IN)
            y = _retention(p, bsz, s)
            x2, eid, gates = _out_proj(x2, [y.reshape(t, RET_V_W)], [w_out_odd[i].astype(BF16)], router)
        x2 = _moe(x2, eid, gates, ln_ffn[layer], w_gate, w_up, w_down, layer)
    return x2.reshape(bsz, s, d).astype(x.dtype)
```

```python
import functools

import numpy as np
import jax
import jax.numpy as jnp
from jax import lax
from jax.experimental import pallas as pl
from jax.experimental.pallas import tpu as pltpu

F32 = jnp.float32
BF16 = jnp.bfloat16

D_MODEL = 1024
SWA_Q_HEADS = 8
SWA_KV_HEADS = 2
SWA_HEAD_DIM = 64
WINDOW = 128
ROT_DIM = SWA_HEAD_DIM // 4
ROPE_THETA = 500000.0
GDN_HEADS = 4
GDN_HEAD_DIM = 128
CONV_K = 4
GDN_CHUNK = 64
RET_HEADS = 4
RET_QK_DIM = 256
RET_V_DIM = 512
RET_CHUNK = 128
XPOS_THETA = 10000.0
N_GROUPS = 4
EXPERTS_PER_GROUP = 8
N_EXPERTS = N_GROUPS * EXPERTS_PER_GROUP
TOP_K = 2
D_EXPERT = 512
EPS = 1e-6

SWA_Q_W = SWA_Q_HEADS * SWA_HEAD_DIM
SWA_KV_W = SWA_KV_HEADS * SWA_HEAD_DIM
GDN_W = GDN_HEADS * GDN_HEAD_DIM
EVEN_MAIN = SWA_Q_W + 2 * SWA_KV_W + 4 * GDN_W
EVEN_COL_Q = 0
EVEN_COL_GDN = SWA_Q_W
EVEN_COL_Z = EVEN_COL_GDN + 3 * GDN_W
EVEN_COL_K = EVEN_COL_Z + GDN_W
EVEN_COL_V = EVEN_COL_K + SWA_KV_W
RET_QK_W = RET_HEADS * RET_QK_DIM
RET_V_W = RET_HEADS * RET_V_DIM
ODD_IN = 2 * RET_QK_W + 2 * RET_V_W

LANES = 128
VMEM_LIMIT = 56 * 1024 * 1024

NEG_INF = float("-inf")


def _cparams(*sem):
    return pltpu.CompilerParams(dimension_semantics=sem, vmem_limit_bytes=VMEM_LIMIT)


def _resident(shape):
    return pl.BlockSpec(shape, lambda i: (0,) * len(shape), pipeline_mode=pl.Buffered(1))


def _silu(x):
    return x * (1.0 / (1.0 + jnp.exp(-x)))


def _norm_proj_kernel(x_ref, g_ref, w_ref, *rest, tn, has_gate, n_perm, perm_w, conv_cols, seq_tiles):
    rest = list(rest)
    wg_ref = rest.pop(0) if has_gate else None
    perm_ref = rest.pop(0) if n_perm else None
    cw_ref = rest.pop(0) if conv_cols else None
    o_ref = rest.pop(0)
    og_ref = rest.pop(0) if has_gate else None
    wp_ref = rest.pop(0) if n_perm else None
    cbuf_ref = rest.pop(0) if conv_cols else None
    tm = x_ref.shape[0]
    pad = 8

    if conv_cols:
        @pl.when(pl.program_id(0) % seq_tiles == 0)
        def _():
            cbuf_ref[0:pad, :] = jnp.zeros((pad, cbuf_ref.shape[1]), F32)

    if n_perm:
        @pl.when(pl.program_id(0) == 0)
        def _():
            for j in range(n_perm):
                cols = slice(j * perm_w, (j + 1) * perm_w)
                wp_ref[:, cols] = jnp.dot(w_ref[:, cols], perm_ref[...], preferred_element_type=F32).astype(BF16)

    x = x_ref[...]
    h = x * lax.rsqrt(jnp.mean(x * x, axis=-1, keepdims=True) + EPS) * g_ref[...]
    hb = h.astype(BF16)
    n = o_ref.shape[-1]
    for lo in range(0, n, tn):
        hi = min(lo + tn, n)
        src = wp_ref if hi <= n_perm * perm_w else w_ref
        res = jnp.dot(hb, src[:, lo:hi], preferred_element_type=F32)
        if conv_cols and conv_cols[0] <= lo and hi <= conv_cols[1]:
            cc = slice(lo - conv_cols[0], hi - conv_cols[0])
            cbuf_ref[pad:pad + tm, cc] = res
            res = None
            for j in range(CONV_K):
                off = pad - (CONV_K - 1) + j
                term = cbuf_ref[off:off + tm, cc] * cw_ref[j:j + 1, cc]
                res = term if res is None else res + term
            res = _silu(res)
            cbuf_ref[0:pad, cc] = cbuf_ref[tm:tm + pad, cc]
        o_ref[:, lo:hi] = res.astype(o_ref.dtype)
    if has_gate:
        h_lo = (h - hb.astype(F32)).astype(BF16)
        r = jnp.dot(hb, wg_ref[...], preferred_element_type=F32)
        og_ref[...] = r[:, :LANES] + r[:, LANES:] + jnp.dot(h_lo, wg_ref[:, :LANES], preferred_element_type=F32)


def _norm_proj(x2, gain, w_bf16, w_gate_f32=None, perm=None, n_perm=0, conv_w=None, conv_cols=None, seq_len=None,
               *, tm=512, tn=512):
    t, d = x2.shape
    n = w_bf16.shape[1]
    has_gate = w_gate_f32 is not None
    perm_w = perm.shape[0] if n_perm else 0
    assert (n_perm * perm_w) % tn == 0
    seq_tiles = seq_len // tm if conv_cols else 0
    assert not conv_cols or (conv_cols[0] % tn == 0 and conv_cols[1] % tn == 0 and seq_len % tm == 0)
    in_specs = [pl.BlockSpec((tm, d), lambda i: (i, 0)),
                _resident((1, d)),
                _resident((d, n))]
    args = [x2, gain.reshape(1, d), w_bf16]
    out_shape = [jax.ShapeDtypeStruct((t, n), BF16)]
    out_specs = [pl.BlockSpec((tm, n), lambda i: (i, 0))]
    scratch = []
    if has_gate:
        wg_hi = w_gate_f32.astype(BF16)
        wg_lo = (w_gate_f32 - wg_hi.astype(F32)).astype(BF16)
        in_specs.append(_resident((d, 2 * LANES)))
        args.append(jnp.concatenate([wg_hi, wg_lo], axis=1))
        out_shape.append(jax.ShapeDtypeStruct((t, LANES), F32))
        out_specs.append(pl.BlockSpec((tm, LANES), lambda i: (i, 0)))
    if n_perm:
        in_specs.append(_resident((perm_w, perm_w)))
        args.append(perm)
        scratch.append(pltpu.VMEM((d, n_perm * perm_w), BF16))
    if conv_cols:
        cw = conv_cols[1] - conv_cols[0]
        in_specs.append(_resident((CONV_K, cw)))
        args.append(conv_w.astype(F32))
        scratch.append(pltpu.VMEM((tm + 8, cw), F32))
    res = pl.pallas_call(
        functools.partial(_norm_proj_kernel, tn=tn, has_gate=has_gate, n_perm=n_perm, perm_w=perm_w,
                          conv_cols=conv_cols, seq_tiles=seq_tiles),
        grid=(t // tm,), in_specs=in_specs, out_specs=out_specs, out_shape=out_shape, scratch_shapes=scratch,
        compiler_params=_cparams("arbitrary"), name="norm_proj")(*args)
    return res if has_gate else res[0]


def _out_proj_kernel(*refs, n_in, route):
    res_ref = refs[0]
    a_refs = refs[1:1 + n_in]
    w_refs = refs[1 + n_in:1 + 2 * n_in]
    rest = refs[1 + 2 * n_in:]
    acc = res_ref[...]
    for a_ref, w_ref in zip(a_refs, w_refs):
        acc = acc + jnp.dot(a_ref[...], w_ref[...], preferred_element_type=F32)
    if route:
        g_ref, wr_ref, wrlo_ref, br_ref, o_ref, eid_ref, gate_ref = rest
        _route_tile(acc, g_ref, wr_ref, wrlo_ref, br_ref, eid_ref, gate_ref)
    else:
        (o_ref,) = rest
    o_ref[...] = acc


def _out_proj(res, acts, ws, router=None, *, tm=1024):
    t, d = res.shape
    n_in = len(acts)
    in_specs = [pl.BlockSpec((tm, d), lambda i: (i, 0))]
    in_specs += [pl.BlockSpec((tm, a.shape[1]), lambda i: (i, 0)) for a in acts]
    in_specs += [_resident(w.shape) for w in ws]
    out_specs = [pl.BlockSpec((tm, d), lambda i: (i, 0))]
    out_shape = [jax.ShapeDtypeStruct((t, d), F32)]
    extra = ()
    if router is not None:
        extra = tuple(router)
        in_specs += [_resident(r.shape) for r in extra]
        out_specs += [pl.BlockSpec((8, tm), lambda i: (0, i))] * 2
        out_shape += [jax.ShapeDtypeStruct((8, t), jnp.int32), jax.ShapeDtypeStruct((8, t), F32)]
    out = pl.pallas_call(
        functools.partial(_out_proj_kernel, n_in=n_in, route=router is not None),
        grid=(t // tm,), in_specs=in_specs, out_specs=out_specs, out_shape=out_shape,
        compiler_params=_cparams("parallel"), name="out_proj")(res, *acts, *ws, *extra)
    return out if router is not None else out[0]


def _retention_kernel(q_ref, k_ref, v_ref, z_ref, cos_ref, sin_ref, decay_ref, xi_ref, zeta_ref, gam_ref,
                      o_ref, state_ref, *, c, n_chunks):
    half = RET_QK_DIM // 2

    @pl.when(pl.program_id(2) == 0)
    def _():
        state_ref[...] = jnp.zeros_like(state_ref)

    cos = cos_ref[...]
    sin = sin_ref[...]

    def rotate(t):
        a = t[:, :half].astype(F32)
        b = t[:, half:].astype(F32)
        return jnp.concatenate([a * cos - b * sin, b * cos + a * sin], axis=-1)

    q = rotate(q_ref[0])
    k = rotate(k_ref[0]) * (RET_QK_DIM ** -0.5)
    decay = decay_ref[0]
    gamma_c = gam_ref[0, 0:1, 0:1]
    xi = xi_ref[0][:, 0:1]
    zeta = zeta_ref[0][:, 0:1]
    chunks = range(n_chunks)
    rows = [slice(ci * c, (ci + 1) * c) for ci in chunks]
    qb = [q[r].astype(BF16) for r in rows]
    kb = [k[r].astype(BF16) for r in rows]
    kz = [(k[r] * zeta).astype(BF16) for r in rows]
    vc = [v_ref[0, r, :] for r in rows]
    inner = [lax.dot_general(qb[i], kb[i], (((1,), (1,)), ((), ())), preferred_element_type=F32) * decay
             for i in chunks]
    kv = [lax.dot_general(kz[i], vc[i], (((0,), (0,)), ((), ())), preferred_element_type=F32) for i in chunks]
    intra = [jnp.dot(inner[i].astype(BF16), vc[i], preferred_element_type=F32) for i in chunks]
    states = [state_ref[...]]
    for i in chunks:
        states.append(states[i] * gamma_c + kv[i])
    state_ref[...] = states[-1]
    cross = [jnp.dot(qb[i], states[i].astype(BF16), preferred_element_type=F32) for i in chunks]
    for i in chunks:
        out = intra[i] + cross[i] * xi
        o = out * lax.rsqrt(jnp.mean(out * out, axis=-1, keepdims=True) + EPS)
        z = z_ref[0, rows[i], :].astype(F32)
        o_ref[0, rows[i], :] = (_silu(z) * o).astype(o_ref.dtype)


RET_KERNEL_CHUNK = 256


def _retention(p, bsz, s, *, blk=1024):
    h = RET_HEADS
    c = RET_KERNEL_CHUNK
    half = RET_QK_DIM // 2
    pos = jnp.arange(s, dtype=F32)
    freq = 1.0 / (XPOS_THETA ** jnp.linspace(0.0, 1.0, half, dtype=F32))
    ang = pos[:, None] * freq[None, :]
    cos, sin = jnp.cos(ang), jnp.sin(ang)
    log_gamma = jnp.log(1.0 - 2.0 ** (-5.0 - jnp.arange(h, dtype=F32)))
    idx = jnp.arange(c, dtype=F32)
    rel = idx[:, None] - idx[None, :]
    decay = jnp.exp(jnp.where(rel >= 0, rel * log_gamma[:, None, None], NEG_INF))
    xi = jnp.broadcast_to(jnp.exp((idx + 1.0) * log_gamma[:, None])[..., None], (h, c, LANES))
    zeta = jnp.broadcast_to(jnp.exp((c - 1.0 - idx) * log_gamma[:, None])[..., None], (h, c, LANES))
    gam = jnp.broadcast_to(jnp.exp(c * log_gamma)[:, None, None], (h, 8, LANES))
    nq = RET_QK_W // RET_QK_DIM
    nv = RET_V_W // RET_V_DIM
    return pl.pallas_call(
        functools.partial(_retention_kernel, c=c, n_chunks=blk // c),
        grid=(bsz, h, s // blk),
        in_specs=[
            pl.BlockSpec((1, blk, RET_QK_DIM), lambda b, hh, i: (b, i, hh)),
            pl.BlockSpec((1, blk, RET_QK_DIM), lambda b, hh, i: (b, i, nq + hh)),
            pl.BlockSpec((1, blk, RET_V_DIM), lambda b, hh, i: (b, i, nv + hh)),
            pl.BlockSpec((1, blk, RET_V_DIM), lambda b, hh, i: (b, i, 2 * nv + hh)),
            pl.BlockSpec((blk, half), lambda b, hh, i: (i, 0)),
            pl.BlockSpec((blk, half), lambda b, hh, i: (i, 0)),
            pl.BlockSpec((1, c, c), lambda b, hh, i: (hh, 0, 0)),
            pl.BlockSpec((1, c, LANES), lambda b, hh, i: (hh, 0, 0)),
            pl.BlockSpec((1, c, LANES), lambda b, hh, i: (hh, 0, 0)),
            pl.BlockSpec((1, 8, LANES), lambda b, hh, i: (hh, 0, 0)),
        ],
        out_specs=pl.BlockSpec((1, blk, RET_V_DIM), lambda b, hh, i: (b, i, hh)),
        out_shape=jax.ShapeDtypeStruct((bsz, s, RET_V_W), BF16),
        scratch_shapes=[pltpu.VMEM((RET_QK_DIM, RET_V_DIM), F32)],
        compiler_params=_cparams("parallel", "parallel", "arbitrary"), name="retention",
    )(p, p, p, p, cos, sin, decay, xi, zeta, gam)


def _swa_kernel(sink_ref, q_ref, kc_ref, kp_ref, vc_ref, vp_ref, cos_ref, sin_ref, cosp_ref, sinp_ref,
                qg_ref, kg_ref, bd_ref, rp_ref, o_ref, *, nq):
    w = WINDOW
    dh = SWA_HEAD_DIM
    grp = SWA_Q_HEADS // SWA_KV_HEADS
    n = pl.program_id(1)

    def head_rms(x, gain, bd):
        sq = x * x
        hi = sq.astype(BF16)
        lo = (sq - hi.astype(F32)).astype(BF16)
        ssum = jnp.dot(hi, bd, preferred_element_type=F32) + jnp.dot(lo, bd, preferred_element_type=F32)
        return x * lax.rsqrt(ssum * (1.0 / dh) + EPS) * gain

    def rope(x, cos, sin, rp):
        partner = jnp.dot(x.astype(BF16), rp, preferred_element_type=F32)
        return x * cos + partner * sin

    cos, sin = cos_ref[...], sin_ref[...]
    cosq = jnp.concatenate([cos] * (SWA_Q_W // LANES), axis=1)
    sinq = jnp.concatenate([sin] * (SWA_Q_W // LANES), axis=1)
    q = rope(head_rms(q_ref[0].astype(F32), qg_ref[...], bd_ref[...]), cosq, sinq, rp_ref[...])
    q = (q * (dh ** -0.5)).astype(BF16)
    bdk = bd_ref[0:SWA_KV_W, 0:SWA_KV_W]
    rpk = rp_ref[0:SWA_KV_W, 0:SWA_KV_W]
    kcur = rope(head_rms(kc_ref[0].astype(F32), kg_ref[...], bdk), cos, sin, rpk)
    kprev = rope(head_rms(kp_ref[0].astype(F32), kg_ref[...], bdk), cosp_ref[...], sinp_ref[...], rpk)
    kall = jnp.concatenate([kprev, kcur], axis=0).astype(BF16)
    vall = jnp.concatenate([vp_ref[0], vc_ref[0]], axis=0)

    i = lax.broadcasted_iota(jnp.int32, (w, 2 * w), 0)
    j = lax.broadcasted_iota(jnp.int32, (w, 2 * w), 1)
    band = (j > i) & (j <= i + w)
    masks = [band & ((j >= w) | (n > 0))] + [band] * (nq - 1)
    units = [(qb, hq) for qb in range(nq) for hq in range(SWA_Q_HEADS)]
    kh = {(qb, hk): kall[qb * w:(qb + 2) * w, hk * dh:(hk + 1) * dh] for qb in range(nq) for hk in range(SWA_KV_HEADS)}
    vh = {(qb, hk): vall[qb * w:(qb + 2) * w, hk * dh:(hk + 1) * dh] for qb in range(nq) for hk in range(SWA_KV_HEADS)}
    sc = [lax.dot_general(q[qb * w:(qb + 1) * w, hq * dh:(hq + 1) * dh], kh[qb, hq // grp], (((1,), (1,)), ((), ())),
                          preferred_element_type=F32) for qb, hq in units]
    sc = [jnp.where(masks[qb], sc[u], NEG_INF) for u, (qb, hq) in enumerate(units)]
    m = [jnp.maximum(jnp.max(sc[u], axis=-1, keepdims=True), sink_ref[hq]) for u, (qb, hq) in enumerate(units)]
    p = [jnp.exp(sc[u] - m[u]) for u in range(len(units))]
    denom = [jnp.sum(p[u], axis=-1, keepdims=True) + jnp.exp(sink_ref[hq] - m[u]) for u, (qb, hq) in enumerate(units)]
    pv = [jnp.dot(p[u].astype(BF16), vh[qb, hq // grp], preferred_element_type=F32) for u, (qb, hq) in enumerate(units)]
    for qb in range(nq):
        row = [pv[qb * SWA_Q_HEADS + hq] * (1.0 / denom[qb * SWA_Q_HEADS + hq]) for hq in range(SWA_Q_HEADS)]
        o_ref[0, qb * w:(qb + 1) * w, :] = jnp.concatenate(row, axis=-1).astype(o_ref.dtype)


def _rope_tables(s):
    half = ROT_DIM // 2
    pos = jnp.arange(s, dtype=F32)
    inv_freq = 1.0 / (ROPE_THETA ** (jnp.arange(half, dtype=F32) * 2.0 / ROT_DIM))
    ang = pos[:, None] * inv_freq[None, :]
    cos, sin = jnp.cos(ang), jnp.sin(ang)
    ones = jnp.ones((s, SWA_HEAD_DIM - ROT_DIM), F32)
    cos_h = jnp.concatenate([cos, cos, ones], axis=1)
    sin_h = jnp.concatenate([-sin, sin, 0.0 * ones], axis=1)
    reps = LANES // SWA_HEAD_DIM
    return jnp.tile(cos_h, (1, reps)), jnp.tile(sin_h, (1, reps))


def _swa(p, sinks, q_gain, k_gain, bsz, s, *, nq=4):
    w = WINDOW
    blk = nq * w
    cos, sin = _rope_tables(s)
    qg = jnp.tile(q_gain.astype(F32), SWA_Q_HEADS).reshape(1, SWA_Q_W)
    kg = jnp.tile(k_gain.astype(F32), SWA_KV_HEADS).reshape(1, SWA_KV_W)
    lanes = np.arange(SWA_Q_W)
    seg = lanes // SWA_HEAD_DIM
    bd = jnp.asarray(seg[:, None] == seg[None, :], dtype=BF16)
    half = ROT_DIM // 2
    in_head = lanes % SWA_HEAD_DIM
    src = np.where(in_head < half, lanes + half, lanes - half)
    rp = np.zeros((SWA_Q_W, SWA_Q_W), np.float32)
    rot = in_head < ROT_DIM
    rp[src[rot], lanes[rot]] = 1.0
    rp = jnp.asarray(rp, dtype=BF16)
    qblk, kblk, vblk = EVEN_COL_Q // SWA_Q_W, EVEN_COL_K // SWA_KV_W, EVEN_COL_V // SWA_KV_W
    prev = lambda i: jnp.maximum(i * nq - 1, 0)
    return pl.pallas_call(
        functools.partial(_swa_kernel, nq=nq),
        grid=(bsz, s // blk),
        in_specs=[
            pl.BlockSpec(memory_space=pltpu.SMEM),
            pl.BlockSpec((1, blk, SWA_Q_W), lambda b, i: (b, i, qblk)),
            pl.BlockSpec((1, blk, SWA_KV_W), lambda b, i: (b, i, kblk)),
            pl.BlockSpec((1, w, SWA_KV_W), lambda b, i: (b, prev(i), kblk)),
            pl.BlockSpec((1, blk, SWA_KV_W), lambda b, i: (b, i, vblk)),
            pl.BlockSpec((1, w, SWA_KV_W), lambda b, i: (b, prev(i), vblk)),
            pl.BlockSpec((blk, LANES), lambda b, i: (i, 0)),
            pl.BlockSpec((blk, LANES), lambda b, i: (i, 0)),
            pl.BlockSpec((w, LANES), lambda b, i: (prev(i), 0)),
            pl.BlockSpec((w, LANES), lambda b, i: (prev(i), 0)),
            pl.BlockSpec((1, SWA_Q_W), lambda b, i: (0, 0)),
            pl.BlockSpec((1, SWA_KV_W), lambda b, i: (0, 0)),
            pl.BlockSpec((SWA_Q_W, SWA_Q_W), lambda b, i: (0, 0)),
            pl.BlockSpec((SWA_Q_W, SWA_Q_W), lambda b, i: (0, 0)),
        ],
        out_specs=pl.BlockSpec((1, blk, SWA_Q_W), lambda b, i: (b, i, 0)),
        out_shape=jax.ShapeDtypeStruct((bsz, s, SWA_Q_W), BF16),
        compiler_params=_cparams("parallel", "parallel"), name="swa",
    )(sinks.astype(F32), p, p, p, p, p, cos, sin, cos, sin, qg, kg, bd, rp)


def _softplus(x):
    return jnp.maximum(x, 0.0) + jnp.log1p(jnp.exp(-jnp.abs(x)))


def _split3(x):
    x1 = x.astype(BF16)
    r1 = x - x1.astype(F32)
    x2 = r1.astype(BF16)
    return x1, x2, (r1 - x2.astype(F32)).astype(BF16)


def _gdn_kernel(q_ref, k_ref, v_ref, z_ref, gate_ref, alog_ref, dt_ref, gain_ref, o_ref, state_ref, *, blk):
    c = GDN_CHUNK
    dk = GDN_HEAD_DIM
    nh = GDN_HEADS
    sub = LANES
    nt = (((1,), (1,)), ((), ()))

    @pl.when(pl.program_id(1) == 0)
    def _():
        state_ref[...] = jnp.zeros_like(state_ref)

    gates = gate_ref[0]
    g_all = -jnp.exp(alog_ref[...]) * _softplus(gates + dt_ref[...])
    beta_all = 1.0 / (1.0 + jnp.exp(-gates))
    ii = lax.broadcasted_iota(jnp.int32, (sub, sub), 0)
    jj = lax.broadcasted_iota(jnp.int32, (sub, sub), 1)
    same = (ii // c) == (jj // c)
    incl = same & (ii >= jj)
    strict = same & (ii > jj)
    tri = jnp.where(incl, 1.0, 0.0).astype(BF16)
    eye = jnp.where(ii == jj, 1.0, 0.0).astype(F32)
    er = lax.broadcasted_iota(jnp.int32, (8, LANES), 0)
    ec = lax.broadcasted_iota(jnp.int32, (8, LANES), 1)
    eye8 = jnp.where(er == ec, 1.0, 0.0).astype(BF16)
    g_parts = _split3(g_all)
    g_rows = sum(lax.dot_general(eye8, gp, nt, preferred_element_type=F32) for gp in g_parts)
    gr_parts = _split3(g_rows)
    gc_cols, gc_rows = [], []
    for sb in range(blk // sub):
        rows = slice(sb * sub, (sb + 1) * sub)
        gc_cols.append(sum(jnp.dot(tri, gp[rows], preferred_element_type=F32) for gp in g_parts))
        gc_rows.append(sum(lax.dot_general(gp[:, rows], tri, nt, preferred_element_type=F32) for gp in gr_parts))

    n_sub = blk // sub
    units = [(hd, sb) for hd in range(nh) for sb in range(n_sub)]
    dot = functools.partial(jnp.dot, preferred_element_type=F32)

    def dot_nt(x, y):
        return lax.dot_general(x, y, nt, preferred_element_type=F32)

    qs, ks, vs, betas = [], [], [], []
    for hd in range(nh):
        cols = slice(hd * dk, (hd + 1) * dk)
        q = q_ref[0, :, cols].astype(F32)
        k = k_ref[0, :, cols].astype(F32)
        qs.append(q * lax.rsqrt(jnp.sum(q * q, axis=-1, keepdims=True) + EPS) * (dk ** -0.5))
        ks.append(k * lax.rsqrt(jnp.sum(k * k, axis=-1, keepdims=True) + EPS))
        vs.append(v_ref[0, :, cols].astype(F32))
        betas.append(beta_all[:, nh + hd:nh + hd + 1])

    def rows_of(sb):
        return slice(sb * sub, (sb + 1) * sub)

    gcb = [jnp.broadcast_to(gc_cols[sb][:, hd:hd + 1], (sub, LANES)) for hd, sb in units]
    decay = [jnp.exp(jnp.where(incl, gcb[u] - gc_rows[sb][hd:hd + 1, :], NEG_INF)) for u, (hd, sb) in enumerate(units)]
    eg = [jnp.exp(g) for g in gcb]
    q_u = [qs[hd][rows_of(sb)] for hd, sb in units]
    k_u = [ks[hd][rows_of(sb)] for hd, sb in units]
    kb = [k_u[u] * betas[hd][rows_of(sb)] for u, (hd, sb) in enumerate(units)]
    kbf = [x.astype(BF16) for x in k_u]
    a = [dot_nt(kb[u].astype(BF16), kbf[u]) for u in range(len(units))]
    m = [jnp.where(strict, -(a[u] * decay[u]), 0.0) for u in range(len(units))]
    r = [eye + x for x in m]
    mb = [x.astype(BF16) for x in m]
    pw = [dot(x, x) for x in mb]
    for _ in range(int(np.log2(c)) - 2):
        pb = [x.astype(BF16) for x in pw]
        both = [dot(jnp.concatenate([r[u].astype(BF16), pb[u]], axis=0), pb[u]) for u in range(len(units))]
        r = [r[u] + both[u][:sub] for u in range(len(units))]
        pw = [x[sub:] for x in both]
    r = [r[u] + dot(r[u].astype(BF16), pw[u].astype(BF16)) for u in range(len(units))]
    rhs = [jnp.concatenate([kb[u] * eg[u], vs[hd][rows_of(sb)] * betas[hd][rows_of(sb)]], axis=1).astype(BF16)
           for u, (hd, sb) in enumerate(units)]
    wu = [dot(r[u].astype(BF16), rhs[u]).astype(BF16) for u in range(len(units))]
    qk = [dot_nt(q_u[u].astype(BF16), kbf[u]) for u in range(len(units))]
    iwu = [dot((qk[u] * decay[u]).astype(BF16), wu[u]) for u in range(len(units))]
    qe = [(q_u[u] * eg[u] - iwu[u][:, :dk]).astype(BF16) for u in range(len(units))]
    per_chunk = sub // c
    kwu, g_last = {}, {}
    for u, (hd, sb) in enumerate(units):
        for ci in range(per_chunk):
            crow = slice(ci * c, (ci + 1) * c)
            gl = gcb[u][(ci + 1) * c - 1:(ci + 1) * c, :]
            k_dec = (k_u[u][crow] * jnp.exp(gl - gcb[u][crow])).astype(BF16)
            g_last[hd, sb * per_chunk + ci] = jnp.exp(gl)
            kwu[hd, sb * per_chunk + ci] = lax.dot_general(k_dec, wu[u][crow], (((0,), (0,)), ((), ())),
                                                           preferred_element_type=F32)
    outs = [[] for _ in range(nh)]
    for ch in range(blk // c):
        u_sb, ci = ch // per_chunk, ch % per_chunk
        crow = slice(ci * c, (ci + 1) * c)
        states = [state_ref[hd] for hd in range(nh)]
        sbf = [x.astype(BF16) for x in states]
        for hd in range(nh):
            u = hd * n_sub + u_sb
            outs[hd].append(dot(qe[u][crow], sbf[hd]) + iwu[u][crow, dk:])
        corr = [dot(kwu[hd, ch][:, :dk].astype(BF16), sbf[hd]) for hd in range(nh)]
        for hd in range(nh):
            state_ref[hd] = states[hd] * g_last[hd, ch] + (kwu[hd, ch][:, dk:] - corr[hd])
    for hd in range(nh):
        out = jnp.concatenate(outs[hd], axis=0)
        o = out * lax.rsqrt(jnp.mean(out * out, axis=-1, keepdims=True) + EPS) * gain_ref[...]
        z = z_ref[0, :, hd * dk:(hd + 1) * dk].astype(F32)
        o_ref[0, :, hd * dk:(hd + 1) * dk] = (o * _silu(z)).astype(o_ref.dtype)


def _gdn(p, pg, a_log, dt_bias, gdn_gain, bsz, s, *, blk=512):
    dk = GDN_HEAD_DIM
    assert EVEN_COL_GDN % GDN_W == 0 and EVEN_COL_Z % GDN_W == 0
    lane_pad = (0, LANES - GDN_HEADS)
    alog = jnp.pad(a_log.astype(F32), lane_pad).reshape(1, LANES)
    dt = jnp.pad(dt_bias.astype(F32), lane_pad).reshape(1, LANES)
    return pl.pallas_call(
        functools.partial(_gdn_kernel, blk=blk),
        grid=(bsz, s // blk),
        in_specs=[pl.BlockSpec((1, blk, GDN_W), lambda b, i: (b, i, EVEN_COL_GDN // GDN_W)),
                  pl.BlockSpec((1, blk, GDN_W), lambda b, i: (b, i, EVEN_COL_GDN // GDN_W + 1)),
                  pl.BlockSpec((1, blk, GDN_W), lambda b, i: (b, i, EVEN_COL_GDN // GDN_W + 2)),
                  pl.BlockSpec((1, blk, GDN_W), lambda b, i: (b, i, EVEN_COL_Z // GDN_W)),
                  pl.BlockSpec((1, blk, LANES), lambda b, i: (b, i, 0)),
                  pl.BlockSpec((1, LANES), lambda b, i: (0, 0)),
                  pl.BlockSpec((1, LANES), lambda b, i: (0, 0)),
                  pl.BlockSpec((1, dk), lambda b, i: (0, 0))],
        out_specs=pl.BlockSpec((1, blk, GDN_W), lambda b, i: (b, i, 0)),
        out_shape=jax.ShapeDtypeStruct((bsz, s, GDN_W), BF16),
        scratch_shapes=[pltpu.VMEM((GDN_HEADS, dk, dk), F32)],
        compiler_params=_cparams("parallel", "arbitrary"), name="gdn",
    )(p, p, p, p, pg, alog, dt, gdn_gain.astype(F32).reshape(1, dk))


MOE_BM = 512
ROUTER_ROWS = LANES
EXPERT_ROW0 = 8


SLAB = 8
DMA_UNROLL = 4


def _store_slabs(ref, val, rows):
    for sl in range(SLAB):
        ref[pl.ds(sl, rows, stride=SLAB), :] = val[:, sl * LANES:(sl + 1) * LANES]


def _load_slabs(ref, rows):
    return jnp.concatenate([ref[pl.ds(sl, rows, stride=SLAB), :] for sl in range(SLAB)], axis=1)


def _slab(ref, row):
    return ref.at[pl.ds(pl.multiple_of(row * SLAB, SLAB), SLAB), :]


def _ffn_norm(x, gain):
    return x * lax.rsqrt(jnp.mean(x * x, axis=-1, keepdims=True) + EPS) * gain


def _route_tile(x, g_ref, wr_ref, wrlo_ref, br_ref, eid_ref, gate_ref):
    h = _ffn_norm(x, g_ref[...])
    tm = h.shape[0]
    h_hi = h.astype(BF16)
    h_lo = (h - h_hi.astype(F32)).astype(BF16)
    nt = (((1,), (1,)), ((), ()))
    logits = (lax.dot_general(wr_ref[...], h_hi, nt, preferred_element_type=F32)
              + lax.dot_general(wr_ref[...], h_lo, nt, preferred_element_type=F32)
              + lax.dot_general(wrlo_ref[...], h_hi, nt, preferred_element_type=F32)) + br_ref[...]
    row = lax.broadcasted_iota(jnp.int32, (8, tm), 0)
    gl = jnp.where(row < N_GROUPS, logits[0:8, :], NEG_INF)
    gmax = jnp.max(gl, axis=0, keepdims=True)
    grp_p = 1.0 / jnp.sum(jnp.exp(gl - gmax), axis=0, keepdims=True)
    grp_idx = jnp.min(jnp.where(gl == gmax, row, 8), axis=0, keepdims=True)
    el = jnp.zeros((8, tm), F32)
    for g in range(N_GROUPS):
        lo = EXPERT_ROW0 + g * EXPERTS_PER_GROUP
        el = jnp.where(grp_idx == g, logits[lo:lo + EXPERTS_PER_GROUP, :], el)
    m1 = jnp.max(el, axis=0, keepdims=True)
    i1 = jnp.min(jnp.where(el == m1, row, 8), axis=0, keepdims=True)
    el2 = jnp.where(row == i1, NEG_INF, el)
    m2 = jnp.max(el2, axis=0, keepdims=True)
    i2 = jnp.min(jnp.where(el2 == m2, row, 8), axis=0, keepdims=True)
    z = jnp.sum(jnp.exp(el - m1), axis=0, keepdims=True)
    p1 = 1.0 / z
    p2 = jnp.exp(m2 - m1) / z
    scale = grp_p / (p1 + p2)
    e1 = grp_idx * EXPERTS_PER_GROUP + i1
    e2 = grp_idx * EXPERTS_PER_GROUP + i2
    eid_ref[...] = jnp.where(row == 0, e1, jnp.where(row == 1, e2, 0))
    gate_ref[...] = jnp.where(row == 0, p1 * scale, jnp.where(row == 1, p2 * scale, 0.0))


def _router_operands(gain, w_rg, b_rg, w_re, b_re):
    d = w_rg.shape[0]
    wr = jnp.zeros((ROUTER_ROWS, d), F32)
    wr = wr.at[0:N_GROUPS].set(w_rg.astype(F32).T).at[EXPERT_ROW0:EXPERT_ROW0 + N_EXPERTS].set(w_re.astype(F32).T)
    br = jnp.zeros((ROUTER_ROWS, 1), F32)
    br = br.at[0:N_GROUPS, 0].set(b_rg.astype(F32)).at[EXPERT_ROW0:EXPERT_ROW0 + N_EXPERTS, 0].set(b_re.astype(F32))
    wr_hi = wr.astype(BF16)
    wr_lo = (wr - wr_hi.astype(F32)).astype(BF16)
    return gain.astype(F32).reshape(1, d), wr_hi, wr_lo, br


def _rank_kernel(eid_ref, dest_ref, meta_ref, cnt_ref, off_ref, *, bm):
    ph = pl.program_id(0)
    i = pl.program_id(1)
    tm = eid_ref.shape[1]
    erow = lax.broadcasted_iota(jnp.int32, (N_EXPERTS, tm), 0)
    oh0 = jnp.where(erow == eid_ref[0:1, :], 1.0, 0.0).astype(F32)
    oh1 = jnp.where(erow == eid_ref[1:2, :], 1.0, 0.0).astype(F32)
    oh = oh0 + oh1
    tile_cnt = jnp.sum(oh, axis=1, keepdims=True)

    @pl.when((ph == 0) & (i == 0))
    def _():
        cnt_ref[...] = jnp.zeros_like(cnt_ref)

    @pl.when(ph == 0)
    def _():
        cnt_ref[...] += tile_cnt

    @pl.when((ph == 1) & (i == 0))
    def _():
        cnt = cnt_ref[...]
        padded = jnp.ceil(cnt * (1.0 / bm)) * bm
        er = lax.broadcasted_iota(jnp.int32, (N_EXPERTS, N_EXPERTS), 0)
        ec = lax.broadcasted_iota(jnp.int32, (N_EXPERTS, N_EXPERTS), 1)
        lower = jnp.where(ec < er, 1.0, 0.0).astype(F32)
        start = jnp.dot(lower, padded, preferred_element_type=F32, precision=lax.Precision.HIGHEST)
        off_ref[...] = start
        end = start + padded
        nb = meta_ref.shape[1]
        blk_row = lax.broadcasted_iota(jnp.int32, (N_EXPERTS, nb), 1).astype(F32) * bm
        endb = jnp.concatenate([end] * (nb // LANES), axis=1)
        be = jnp.sum(jnp.where(endb <= blk_row, 1.0, 0.0), axis=0, keepdims=True)
        be = jnp.minimum(be, N_EXPERTS - 1.0)
        used = jnp.concatenate([end[N_EXPERTS - 1:N_EXPERTS, :]] * (nb // LANES), axis=1) * (1.0 / bm)
        elane = lax.broadcasted_iota(jnp.int32, (N_EXPERTS, nb), 1)
        erow2 = lax.broadcasted_iota(jnp.int32, (N_EXPERTS, nb), 0)
        ends = jnp.sum(jnp.where(elane == erow2, endb, 0.0), axis=0, keepdims=True)
        mrow = lax.broadcasted_iota(jnp.int32, (8, nb), 0)
        meta_ref[...] = jnp.where(mrow == 0, be, jnp.where(mrow == 1, used, jnp.where(mrow == 2, ends, 0.0))
                                  ).astype(jnp.int32)
        cnt_ref[...] = jnp.zeros_like(cnt_ref)

    @pl.when(ph == 1)
    def _():
        kr = lax.broadcasted_iota(jnp.int32, (tm, tm), 0)
        kc = lax.broadcasted_iota(jnp.int32, (tm, tm), 1)
        su = jnp.where(kr < kc, 1.0, 0.0).astype(BF16)
        csum = jnp.dot(oh.astype(BF16), su, preferred_element_type=F32)
        pos = csum + (off_ref[...] + cnt_ref[...])[:, 0:1]
        d0 = jnp.sum(oh0 * pos, axis=0, keepdims=True)
        d1 = jnp.sum(oh1 * pos, axis=0, keepdims=True)
        row = lax.broadcasted_iota(jnp.int32, (8, tm), 0)
        dest_ref[...] = jnp.where(row == 0, d0, jnp.where(row == 1, d1, 0.0)).astype(jnp.int32)
        cnt_ref[...] += tile_cnt


def _moe_rank(eid, n_blocks, *, bm, tm=1024):
    t = eid.shape[1]
    nb = -(-n_blocks // LANES) * LANES
    return pl.pallas_call(
        functools.partial(_rank_kernel, bm=bm), grid=(2, t // tm),
        in_specs=[pl.BlockSpec((8, tm), lambda p, i: (0, i))],
        out_specs=[pl.BlockSpec((8, tm), lambda p, i: (0, i * p)),
                   pl.BlockSpec((8, nb), lambda p, i: (0, 0))],
        out_shape=[jax.ShapeDtypeStruct((8, t), jnp.int32), jax.ShapeDtypeStruct((8, nb), jnp.int32)],
        scratch_shapes=[pltpu.VMEM((N_EXPERTS, LANES), F32), pltpu.VMEM((N_EXPERTS, LANES), F32)],
        compiler_params=_cparams("arbitrary", "arbitrary"), name="moe_rank")(eid)


def _dispatch_kernel(dest_ref, ends_ref, x_ref, g_ref, rows_hbm, zero_ref, hbuf_ref, sem, rsem, *, tm, bm):
    @pl.when(pl.program_id(0) == 0)
    def _():
        zero_ref[...] = jnp.zeros_like(zero_ref)

        def tail(e):
            start = ends_ref[0, e] - bm
            return pltpu.make_async_copy(zero_ref, rows_hbm.at[pl.ds(pl.multiple_of(start * SLAB, SLAB), bm * SLAB), :],
                                         sem)

        def nonempty(e):
            return ends_ref[0, e] > jnp.where(e > 0, ends_ref[0, jnp.maximum(e - 1, 0)], 0)

        def fill(e, carry):
            @pl.when(nonempty(e))
            def _():
                tail(e).start()
            return carry

        def drain(e, carry):
            @pl.when(nonempty(e))
            def _():
                tail(e).wait()
            return carry

        lax.fori_loop(0, N_EXPERTS, fill, 0)
        lax.fori_loop(0, N_EXPERTS, drain, 0)

        def unused(b):
            return pltpu.make_async_copy(
                zero_ref, rows_hbm.at[pl.ds(pl.multiple_of(b * (bm * SLAB), SLAB), bm * SLAB), :], sem)

        def fill_unused(b, carry):
            unused(b).start()
            return carry

        def drain_unused(b, carry):
            unused(b).wait()
            return carry

        first_unused = ends_ref[0, N_EXPERTS - 1] // bm
        n_blocks = rows_hbm.shape[0] // (bm * SLAB)
        lax.fori_loop(first_unused, n_blocks, fill_unused, 0)
        lax.fori_loop(first_unused, n_blocks, drain_unused, 0)

    i = pl.program_id(0)
    slot = lax.rem(i, 2)

    def drain_slot(s):
        for _ in range(TOP_K):
            pltpu.make_async_copy(hbuf_ref.at[s], rows_hbm.at[pl.ds(0, tm * SLAB), :], rsem.at[s]).wait()

    for s in range(2):
        @pl.when((slot == s) & (i >= 2))
        def _():
            drain_slot(s)

        @pl.when(slot == s)
        def _():
            _store_slabs(hbuf_ref.at[s], _ffn_norm(x_ref[...], g_ref[...]), tm)

            def issue(g, carry):
                for u in range(DMA_UNROLL):
                    tok = g * DMA_UNROLL + u
                    for k in range(TOP_K):
                        pltpu.make_async_copy(_slab(hbuf_ref.at[s], tok),
                                              _slab(rows_hbm, dest_ref[0, 0, k * tm + tok]),
                                              rsem.at[s]).start(priority=k)
                return carry

            lax.fori_loop(0, tm // DMA_UNROLL, issue, 0)

    last = pl.num_programs(0) - 1

    @pl.when(i == last)
    def _():
        for s in range(2):
            @pl.when((slot == s) | (last >= 1))
            def _():
                drain_slot(s)


def _moe_dispatch(dest_tiles, expert_ends, x2, gain, n_rows, *, tm, bm):
    t, d = x2.shape
    return pl.pallas_call(
        functools.partial(_dispatch_kernel, tm=tm, bm=bm), grid=(t // tm,),
        in_specs=[pl.BlockSpec((1, 1, TOP_K * tm), lambda i: (i, 0, 0), memory_space=pltpu.SMEM),
                  pl.BlockSpec(memory_space=pltpu.SMEM),
                  pl.BlockSpec((tm, d), lambda i: (i, 0)),
                  pl.BlockSpec((1, d), lambda i: (0, 0))],
        out_specs=pl.BlockSpec(memory_space=pl.ANY),
        out_shape=jax.ShapeDtypeStruct((n_rows * SLAB, LANES), F32),
        scratch_shapes=[pltpu.VMEM((bm * SLAB, LANES), F32), pltpu.VMEM((2, tm * SLAB, LANES), F32),
                        pltpu.SemaphoreType.DMA(()), pltpu.SemaphoreType.DMA((2,))],
        compiler_params=pltpu.CompilerParams(dimension_semantics=("arbitrary",), has_side_effects=True,
                                             vmem_limit_bytes=VMEM_LIMIT),
        name="moe_dispatch")(dest_tiles, expert_ends, x2, gain.reshape(1, d))


EXPERT_RING = 3


def _experts_kernel(be_ref, used_ref, x_hbm, wg_ref, wu_ref, wd_ref, y_ref, wgb_ref, wub_ref, wdb_ref, xbuf_ref,
                    xsem):
    b = pl.program_id(0)
    used = used_ref[0]
    bm = y_ref.shape[0] // SLAB

    def fetch(blk):
        slot = lax.rem(blk, EXPERT_RING)
        rows = pl.ds(pl.multiple_of(blk * (bm * SLAB), SLAB), bm * SLAB)
        return pltpu.make_async_copy(x_hbm.at[rows, :], xbuf_ref.at[slot], xsem.at[slot])

    @pl.when(b == 0)
    def _():
        for first in range(EXPERT_RING - 1):
            @pl.when(first < used)
            def _():
                fetch(first).start()

    @pl.when(b + (EXPERT_RING - 1) < used)
    def _():
        fetch(b + (EXPERT_RING - 1)).start()

    prev = be_ref[jnp.maximum(b - 1, 0)]

    @pl.when((b == 0) | (be_ref[b] != prev))
    def _():
        wgb_ref[...] = wg_ref[0, 0].astype(BF16)
        wub_ref[...] = wu_ref[0, 0].astype(BF16)
        wdb_ref[...] = wd_ref[0, 0].astype(BF16)

    @pl.when(b < used)
    def _():
        fetch(b).wait()
        x = _load_slabs(xbuf_ref.at[lax.rem(b, EXPERT_RING)], bm).astype(BF16)
        gate = jnp.dot(x, wgb_ref[...], preferred_element_type=F32)
        up = jnp.dot(x, wub_ref[...], preferred_element_type=F32)
        hid = (_silu(gate) * up).astype(BF16)
        _store_slabs(y_ref, jnp.dot(hid, wdb_ref[...], preferred_element_type=F32), bm)

    @pl.when(b >= used_ref[0])
    def _():
        y_ref[...] = jnp.zeros_like(y_ref)


def _moe_experts(block_expert, n_used, x_rows, w_gate, w_up, w_down, layer, *, bm):
    n_rows = x_rows.shape[0] // SLAB
    d, de = w_gate.shape[2], w_gate.shape[3]
    grid_spec = pltpu.PrefetchScalarGridSpec(
        num_scalar_prefetch=2, grid=(n_rows // bm,),
        in_specs=[pl.BlockSpec(memory_space=pl.ANY),
                  pl.BlockSpec((1, 1, d, de), lambda b, be, nu: (layer, be[b], 0, 0)),
                  pl.BlockSpec((1, 1, d, de), lambda b, be, nu: (layer, be[b], 0, 0)),
                  pl.BlockSpec((1, 1, de, d), lambda b, be, nu: (layer, be[b], 0, 0))],
        out_specs=pl.BlockSpec((bm * SLAB, LANES), lambda b, be, nu: (b, 0)),
        scratch_shapes=[pltpu.VMEM((d, de), BF16), pltpu.VMEM((d, de), BF16), pltpu.VMEM((de, d), BF16),
                        pltpu.VMEM((EXPERT_RING, bm * SLAB, LANES), F32), pltpu.SemaphoreType.DMA((EXPERT_RING,))])
    return pl.pallas_call(
        _experts_kernel, grid_spec=grid_spec, out_shape=jax.ShapeDtypeStruct((n_rows * SLAB, LANES), F32),
        compiler_params=_cparams("arbitrary"), name="moe_experts",
    )(block_expert, n_used, x_rows, w_gate, w_up, w_down)


def _combine_kernel(dest_ref, next_ref, x_ref, gate_ref, y_hbm, o_ref, buf_ref, sem, *, tm):
    i = pl.program_id(0)
    slot = lax.rem(i, 2)

    def gather(idx_ref, s):
        def issue(g, carry):
            for u in range(DMA_UNROLL):
                tok = g * DMA_UNROLL + u
                for k in range(TOP_K):
                    pltpu.make_async_copy(_slab(y_hbm, idx_ref[0, 0, k * tm + tok]), _slab(buf_ref.at[s, k], tok),
                                          sem.at[s]).start(priority=k)
            return carry

        lax.fori_loop(0, tm // DMA_UNROLL, issue, 0)

    for s in range(2):
        @pl.when((i == 0) & (slot == s))
        def _():
            gather(dest_ref, s)

        @pl.when((i + 1 < pl.num_programs(0)) & (slot == 1 - s))
        def _():
            gather(next_ref, s)

    er = lax.broadcasted_iota(jnp.int32, (8, LANES), 0)
    ec = lax.broadcasted_iota(jnp.int32, (8, LANES), 1)
    eye = jnp.where(er == ec, 1.0, 0.0).astype(F32)
    gcol = lax.dot_general(gate_ref[...], eye, (((0,), (0,)), ((), ())), preferred_element_type=F32,
                           precision=lax.Precision.HIGHEST)
    for s in range(2):
        @pl.when(slot == s)
        def _():
            for k in range(TOP_K):
                pltpu.make_async_copy(y_hbm.at[pl.ds(0, tm * SLAB), :], buf_ref.at[s, k], sem.at[s]).wait()
            o_ref[...] = (x_ref[...] + gcol[:, 0:1] * _load_slabs(buf_ref.at[s, 0], tm)
                          + gcol[:, 1:2] * _load_slabs(buf_ref.at[s, 1], tm))


def _moe_combine(dest_tiles, x2, gates, y_rows, *, tm):
    t, d = x2.shape
    n = t // tm
    dest_spec = functools.partial(pl.BlockSpec, (1, 1, TOP_K * tm), memory_space=pltpu.SMEM)
    return pl.pallas_call(
        functools.partial(_combine_kernel, tm=tm), grid=(n,),
        in_specs=[dest_spec(lambda i: (i, 0, 0)),
                  dest_spec(lambda i: (jnp.minimum(i + 1, n - 1), 0, 0)),
                  pl.BlockSpec((tm, d), lambda i: (i, 0)),
                  pl.BlockSpec((8, tm), lambda i: (0, i)),
                  pl.BlockSpec(memory_space=pl.ANY)],
        out_specs=pl.BlockSpec((tm, d), lambda i: (i, 0)),
        out_shape=jax.ShapeDtypeStruct((t, d), F32),
        scratch_shapes=[pltpu.VMEM((2, TOP_K, tm * SLAB, LANES), F32), pltpu.SemaphoreType.DMA((2,))],
        compiler_params=_cparams("arbitrary"), name="moe_combine")(dest_tiles, dest_tiles, x2, gates, y_rows)


def _moe(x2, eid, gates, gain, w_gate, w_up, w_down, layer, *, tm_dma=1024):
    t, d = x2.shape
    bm = MOE_BM
    n_rows = t * TOP_K + N_EXPERTS * bm
    n_blocks = n_rows // bm
    dest, meta = _moe_rank(eid, n_blocks, bm=bm)
    dest_tiles = dest[0:TOP_K].reshape(TOP_K, t // tm_dma, tm_dma).transpose(1, 0, 2).reshape(
        t // tm_dma, 1, TOP_K * tm_dma)
    x_rows = _moe_dispatch(dest_tiles, meta[2:3, :N_EXPERTS], x2, gain, n_rows, tm=tm_dma, bm=bm)
    y_rows = _moe_experts(meta[0, :n_blocks], meta[1, 0:1], x_rows, w_gate, w_up, w_down, layer, bm=bm)
    return _moe_combine(dest_tiles, x2, gates, y_rows, tm=tm_dma)


def _even_odd_perm():
    half = RET_QK_DIM // 2
    src = np.concatenate([2 * np.arange(half), 2 * np.arange(half) + 1])
    perm = np.zeros((RET_QK_DIM, RET_QK_DIM), np.float32)
    perm[src, np.arange(RET_QK_DIM)] = 1.0
    return jnp.asarray(perm, dtype=BF16)


def kernel(x, ln_mix_even, w_in_even, conv_w_even, a_log_even, dt_bias_even, q_gain_even, k_gain_even, sinks_even, gdn_gain_even, w_out_even, ln_mix_odd, w_in_odd, w_out_odd, ln_ffn, w_router_group, b_router_group, w_router_expert, b_router_expert, w_gate, w_up, w_down):
    bsz, s, d = x.shape
    t = bsz * s
    x2 = x.reshape(t, d).astype(F32)
    depth = ln_ffn.shape[0]
    for layer in range(depth):
        i = layer // 2
        router = _router_operands(ln_ffn[layer], w_router_group[layer], b_router_group[layer],
                                  w_router_expert[layer], b_router_expert[layer])
        if layer % 2 == 0:
            w_in = w_in_even[i]
            swa_end = SWA_Q_W + 2 * SWA_KV_W
            w_main = jnp.concatenate([w_in[:, :SWA_Q_W], w_in[:, swa_end:EVEN_MAIN], w_in[:, SWA_Q_W:swa_end]],
                                     axis=1).astype(BF16)
            w_gates = jnp.pad(w_in[:, EVEN_MAIN:].astype(F32), ((0, 0), (0, LANES - 2 * GDN_HEADS)))
            p, pg = _norm_proj(x2, ln_mix_even[i], w_main, w_gates, conv_w=conv_w_even[i],
                               conv_cols=(EVEN_COL_GDN, EVEN_COL_Z), seq_len=s, tm=1024)
            p = p.reshape(bsz, s, EVEN_MAIN)
            y_a = _swa(p, sinks_even[i], q_gain_even[i], k_gain_even[i], bsz, s)
            y_b = _gdn(p, pg.reshape(bsz, s, LANES), a_log_even[i], dt_bias_even[i], gdn_gain_even[i], bsz, s)
            w_out = w_out_even[i].astype(BF16)
            x2, eid, gates = _out_proj(x2, [y_a.reshape(t, SWA_Q_W), y_b.reshape(t, GDN_W)],
                                       [w_out[:SWA_Q_W], w_out[SWA_Q_W:]], router)
        else:
            p = _norm_proj(x2, ln_mix_odd[i], w_in_odd[i].astype(BF16), perm=_even_odd_perm(),
                           n_perm=2 * RET_HEADS).reshape(bsz, s, ODD_IN)
            y = _retention(p, bsz, s)
            x2, eid, gates = _out_proj(x2, [y.reshape(t, RET_V_W)], [w_out_odd[i].astype(BF16)], router)
        x2 = _moe(x2, eid, gates, ln_ffn[layer], w_gate, w_up, w_down, layer)
    return x2.reshape(bsz, s, d).astype(x.dtype)
```

```python
import functools

import numpy as np
import jax
import jax.numpy as jnp
from jax import lax
from jax.experimental import pallas as pl
from jax.experimental.pallas import tpu as pltpu

F32 = jnp.float32
BF16 = jnp.bfloat16

D_MODEL = 1024
SWA_Q_HEADS = 8
SWA_KV_HEADS = 2
SWA_HEAD_DIM = 64
WINDOW = 128
ROT_DIM = SWA_HEAD_DIM // 4
ROPE_THETA = 500000.0
GDN_HEADS = 4
GDN_HEAD_DIM = 128
CONV_K = 4
GDN_CHUNK = 64
RET_HEADS = 4
RET_QK_DIM = 256
RET_V_DIM = 512
RET_CHUNK = 128
XPOS_THETA = 10000.0
N_GROUPS = 4
EXPERTS_PER_GROUP = 8
N_EXPERTS = N_GROUPS * EXPERTS_PER_GROUP
TOP_K = 2
D_EXPERT = 512
EPS = 1e-6

SWA_Q_W = SWA_Q_HEADS * SWA_HEAD_DIM
SWA_KV_W = SWA_KV_HEADS * SWA_HEAD_DIM
GDN_W = GDN_HEADS * GDN_HEAD_DIM
EVEN_MAIN = SWA_Q_W + 2 * SWA_KV_W + 4 * GDN_W
EVEN_COL_Q = 0
EVEN_COL_GDN = SWA_Q_W
EVEN_COL_Z = EVEN_COL_GDN + 3 * GDN_W
EVEN_COL_K = EVEN_COL_Z + GDN_W
EVEN_COL_V = EVEN_COL_K + SWA_KV_W
RET_QK_W = RET_HEADS * RET_QK_DIM
RET_V_W = RET_HEADS * RET_V_DIM
ODD_IN = 2 * RET_QK_W + 2 * RET_V_W

LANES = 128
VMEM_LIMIT = 56 * 1024 * 1024

NEG_INF = float("-inf")


def _cparams(*sem):
    return pltpu.CompilerParams(dimension_semantics=sem, vmem_limit_bytes=VMEM_LIMIT)


def _resident(shape):
    return pl.BlockSpec(shape, lambda i: (0,) * len(shape), pipeline_mode=pl.Buffered(1))


def _silu(x):
    return x * (1.0 / (1.0 + jnp.exp(-x)))


def _norm_proj_kernel(x_ref, g_ref, w_ref, *rest, tn, has_gate, n_perm, perm_w, conv_cols, seq_tiles, combine):
    rest = list(rest)
    wg_ref = rest.pop(0) if has_gate else None
    perm_ref = rest.pop(0) if n_perm else None
    cw_ref = rest.pop(0) if conv_cols else None
    comb_in = [rest.pop(0) for _ in range(4)] if combine else None
    o_ref = rest.pop(0)
    og_ref = rest.pop(0) if has_gate else None
    xnew_ref = rest.pop(0) if combine else None
    wp_ref = rest.pop(0) if n_perm else None
    cbuf_ref = rest.pop(0) if conv_cols else None
    comb_scratch = [rest.pop(0) for _ in range(2)] if combine else None
    tm = x_ref.shape[0]
    pad = 8

    if combine:
        dest_ref, next_ref, gate_ref, y_hbm = comb_in
        _combine_tile(dest_ref, next_ref, x_ref, gate_ref, y_hbm, xnew_ref, *comb_scratch, tm)
        x_ref = xnew_ref

    if conv_cols:
        @pl.when(pl.program_id(0) % seq_tiles == 0)
        def _():
            cbuf_ref[0:pad, :] = jnp.zeros((pad, cbuf_ref.shape[1]), F32)

    if n_perm:
        @pl.when(pl.program_id(0) == 0)
        def _():
            for j in range(n_perm):
                cols = slice(j * perm_w, (j + 1) * perm_w)
                wp_ref[:, cols] = jnp.dot(w_ref[:, cols], perm_ref[...], preferred_element_type=F32).astype(BF16)

    x = x_ref[...]
    h = x * lax.rsqrt(jnp.mean(x * x, axis=-1, keepdims=True) + EPS) * g_ref[...]
    hb = h.astype(BF16)
    n = o_ref.shape[-1]
    for lo in range(0, n, tn):
        hi = min(lo + tn, n)
        src = wp_ref if hi <= n_perm * perm_w else w_ref
        res = jnp.dot(hb, src[:, lo:hi], preferred_element_type=F32)
        if conv_cols and conv_cols[0] <= lo and hi <= conv_cols[1]:
            cc = slice(lo - conv_cols[0], hi - conv_cols[0])
            cbuf_ref[pad:pad + tm, cc] = res
            res = None
            for j in range(CONV_K):
                off = pad - (CONV_K - 1) + j
                term = cbuf_ref[off:off + tm, cc] * cw_ref[j:j + 1, cc]
                res = term if res is None else res + term
            res = _silu(res)
            cbuf_ref[0:pad, cc] = cbuf_ref[tm:tm + pad, cc]
        o_ref[:, lo:hi] = res.astype(o_ref.dtype)
    if has_gate:
        h_lo = (h - hb.astype(F32)).astype(BF16)
        r = jnp.dot(hb, wg_ref[...], preferred_element_type=F32)
        og_ref[...] = r[:, :LANES] + r[:, LANES:] + jnp.dot(h_lo, wg_ref[:, :LANES], preferred_element_type=F32)


def _norm_proj(x2, gain, w_bf16, w_gate_f32=None, perm=None, n_perm=0, conv_w=None, conv_cols=None, seq_len=None,
               moe_out=None, *, tm=512, tn=512):
    t, d = x2.shape
    n = w_bf16.shape[1]
    has_gate = w_gate_f32 is not None
    perm_w = perm.shape[0] if n_perm else 0
    assert (n_perm * perm_w) % tn == 0
    seq_tiles = seq_len // tm if conv_cols else 0
    assert not conv_cols or (conv_cols[0] % tn == 0 and conv_cols[1] % tn == 0 and seq_len % tm == 0)
    in_specs = [pl.BlockSpec((tm, d), lambda i: (i, 0)),
                _resident((1, d)),
                _resident((d, n))]
    args = [x2, gain.reshape(1, d), w_bf16]
    out_shape = [jax.ShapeDtypeStruct((t, n), BF16)]
    out_specs = [pl.BlockSpec((tm, n), lambda i: (i, 0))]
    scratch = []
    if has_gate:
        wg_hi = w_gate_f32.astype(BF16)
        wg_lo = (w_gate_f32 - wg_hi.astype(F32)).astype(BF16)
        in_specs.append(_resident((d, 2 * LANES)))
        args.append(jnp.concatenate([wg_hi, wg_lo], axis=1))
        out_shape.append(jax.ShapeDtypeStruct((t, LANES), F32))
        out_specs.append(pl.BlockSpec((tm, LANES), lambda i: (i, 0)))
    if n_perm:
        in_specs.append(_resident((perm_w, perm_w)))
        args.append(perm)
        scratch.append(pltpu.VMEM((d, n_perm * perm_w), BF16))
    if conv_cols:
        cw = conv_cols[1] - conv_cols[0]
        in_specs.append(_resident((CONV_K, cw)))
        args.append(conv_w.astype(F32))
        scratch.append(pltpu.VMEM((tm + 8, cw), F32))
    if moe_out is not None:
        dest, gates, y_rows = moe_out
        dest_tiles = _dest_tiles(dest, tm)
        n_tiles = t // tm
        dest_spec = functools.partial(pl.BlockSpec, (1, 1, TOP_K * tm), memory_space=pltpu.SMEM)
        in_specs += [dest_spec(lambda i: (i, 0, 0)), dest_spec(lambda i: (jnp.minimum(i + 1, n_tiles - 1), 0, 0)),
                     pl.BlockSpec((8, tm), lambda i: (0, i)), pl.BlockSpec(memory_space=pl.ANY)]
        args += [dest_tiles, dest_tiles, gates, y_rows]
        out_shape.append(jax.ShapeDtypeStruct((t, d), F32))
        out_specs.append(pl.BlockSpec((tm, d), lambda i: (i, 0)))
        scratch += [pltpu.VMEM((2, TOP_K, tm * SLAB, LANES), F32), pltpu.SemaphoreType.DMA((2,))]
    res = pl.pallas_call(
        functools.partial(_norm_proj_kernel, tn=tn, has_gate=has_gate, n_perm=n_perm, perm_w=perm_w,
                          conv_cols=conv_cols, seq_tiles=seq_tiles, combine=moe_out is not None),
        grid=(t // tm,), in_specs=in_specs, out_specs=out_specs, out_shape=out_shape, scratch_shapes=scratch,
        compiler_params=_cparams("arbitrary"), name="norm_proj")(*args)
    return res if len(res) > 1 else res[0]


def _out_proj_kernel(*refs, n_in, route):
    res_ref = refs[0]
    a_refs = refs[1:1 + n_in]
    w_refs = refs[1 + n_in:1 + 2 * n_in]
    rest = refs[1 + 2 * n_in:]
    acc = res_ref[...]
    for a_ref, w_ref in zip(a_refs, w_refs):
        acc = acc + jnp.dot(a_ref[...], w_ref[...], preferred_element_type=F32)
    if route:
        g_ref, wr_ref, wrlo_ref, br_ref, o_ref, eid_ref, gate_ref = rest
        _route_tile(acc, g_ref, wr_ref, wrlo_ref, br_ref, eid_ref, gate_ref)
    else:
        (o_ref,) = rest
    o_ref[...] = acc


def _out_proj(res, acts, ws, router=None, *, tm=1024):
    t, d = res.shape
    n_in = len(acts)
    in_specs = [pl.BlockSpec((tm, d), lambda i: (i, 0))]
    in_specs += [pl.BlockSpec((tm, a.shape[1]), lambda i: (i, 0)) for a in acts]
    in_specs += [_resident(w.shape) for w in ws]
    out_specs = [pl.BlockSpec((tm, d), lambda i: (i, 0))]
    out_shape = [jax.ShapeDtypeStruct((t, d), F32)]
    extra = ()
    if router is not None:
        extra = tuple(router)
        in_specs += [_resident(r.shape) for r in extra]
        out_specs += [pl.BlockSpec((8, tm), lambda i: (0, i))] * 2
        out_shape += [jax.ShapeDtypeStruct((8, t), jnp.int32), jax.ShapeDtypeStruct((8, t), F32)]
    out = pl.pallas_call(
        functools.partial(_out_proj_kernel, n_in=n_in, route=router is not None),
        grid=(t // tm,), in_specs=in_specs, out_specs=out_specs, out_shape=out_shape,
        compiler_params=_cparams("parallel"), name="out_proj")(res, *acts, *ws, *extra)
    return out if router is not None else out[0]


def _retention_kernel(q_ref, k_ref, v_ref, z_ref, cos_ref, sin_ref, decay_ref, xi_ref, zeta_ref, gam_ref,
                      o_ref, state_ref, *, c, n_chunks):
    half = RET_QK_DIM // 2

    @pl.when(pl.program_id(2) == 0)
    def _():
        state_ref[...] = jnp.zeros_like(state_ref)

    cos = cos_ref[...]
    sin = sin_ref[...]

    def rotate(t):
        a = t[:, :half].astype(F32)
        b = t[:, half:].astype(F32)
        return jnp.concatenate([a * cos - b * sin, b * cos + a * sin], axis=-1)

    q = rotate(q_ref[0])
    k = rotate(k_ref[0]) * (RET_QK_DIM ** -0.5)
    decay = decay_ref[0]
    gamma_c = gam_ref[0, 0:1, 0:1]
    xi = xi_ref[0][:, 0:1]
    zeta = zeta_ref[0][:, 0:1]
    chunks = range(n_chunks)
    rows = [slice(ci * c, (ci + 1) * c) for ci in chunks]
    qb = [q[r].astype(BF16) for r in rows]
    kb = [k[r].astype(BF16) for r in rows]
    kz = [(k[r] * zeta).astype(BF16) for r in rows]
    vc = [v_ref[0, r, :] for r in rows]
    inner = [lax.dot_general(qb[i], kb[i], (((1,), (1,)), ((), ())), preferred_element_type=F32) * decay
             for i in chunks]
    kv = [lax.dot_general(kz[i], vc[i], (((0,), (0,)), ((), ())), preferred_element_type=F32) for i in chunks]
    intra = [jnp.dot(inner[i].astype(BF16), vc[i], preferred_element_type=F32) for i in chunks]
    states = [state_ref[...]]
    for i in chunks:
        states.append(states[i] * gamma_c + kv[i])
    state_ref[...] = states[-1]
    cross = [jnp.dot(qb[i], states[i].astype(BF16), preferred_element_type=F32) for i in chunks]
    for i in chunks:
        out = intra[i] + cross[i] * xi
        o = out * lax.rsqrt(jnp.mean(out * out, axis=-1, keepdims=True) + EPS)
        z = z_ref[0, rows[i], :].astype(F32)
        o_ref[0, rows[i], :] = (_silu(z) * o).astype(o_ref.dtype)


RET_KERNEL_CHUNK = 256


def _retention(p, bsz, s, *, blk=1024):
    h = RET_HEADS
    c = RET_KERNEL_CHUNK
    half = RET_QK_DIM // 2
    pos = jnp.arange(s, dtype=F32)
    freq = 1.0 / (XPOS_THETA ** jnp.linspace(0.0, 1.0, half, dtype=F32))
    ang = pos[:, None] * freq[None, :]
    cos, sin = jnp.cos(ang), jnp.sin(ang)
    log_gamma = jnp.log(1.0 - 2.0 ** (-5.0 - jnp.arange(h, dtype=F32)))
    idx = jnp.arange(c, dtype=F32)
    rel = idx[:, None] - idx[None, :]
    decay = jnp.exp(jnp.where(rel >= 0, rel * log_gamma[:, None, None], NEG_INF))
    xi = jnp.broadcast_to(jnp.exp((idx + 1.0) * log_gamma[:, None])[..., None], (h, c, LANES))
    zeta = jnp.broadcast_to(jnp.exp((c - 1.0 - idx) * log_gamma[:, None])[..., None], (h, c, LANES))
    gam = jnp.broadcast_to(jnp.exp(c * log_gamma)[:, None, None], (h, 8, LANES))
    nq = RET_QK_W // RET_QK_DIM
    nv = RET_V_W // RET_V_DIM
    return pl.pallas_call(
        functools.partial(_retention_kernel, c=c, n_chunks=blk // c),
        grid=(bsz, h, s // blk),
        in_specs=[
            pl.BlockSpec((1, blk, RET_QK_DIM), lambda b, hh, i: (b, i, hh)),
            pl.BlockSpec((1, blk, RET_QK_DIM), lambda b, hh, i: (b, i, nq + hh)),
            pl.BlockSpec((1, blk, RET_V_DIM), lambda b, hh, i: (b, i, nv + hh)),
            pl.BlockSpec((1, blk, RET_V_DIM), lambda b, hh, i: (b, i, 2 * nv + hh)),
            pl.BlockSpec((blk, half), lambda b, hh, i: (i, 0)),
            pl.BlockSpec((blk, half), lambda b, hh, i: (i, 0)),
            pl.BlockSpec((1, c, c), lambda b, hh, i: (hh, 0, 0)),
            pl.BlockSpec((1, c, LANES), lambda b, hh, i: (hh, 0, 0)),
            pl.BlockSpec((1, c, LANES), lambda b, hh, i: (hh, 0, 0)),
            pl.BlockSpec((1, 8, LANES), lambda b, hh, i: (hh, 0, 0)),
        ],
        out_specs=pl.BlockSpec((1, blk, RET_V_DIM), lambda b, hh, i: (b, i, hh)),
        out_shape=jax.ShapeDtypeStruct((bsz, s, RET_V_W), BF16),
        scratch_shapes=[pltpu.VMEM((RET_QK_DIM, RET_V_DIM), F32)],
        compiler_params=_cparams("parallel", "parallel", "arbitrary"), name="retention",
    )(p, p, p, p, cos, sin, decay, xi, zeta, gam)


def _swa_kernel(sink_ref, q_ref, kc_ref, kp_ref, vc_ref, vp_ref, cos_ref, sin_ref, cosp_ref, sinp_ref,
                qg_ref, kg_ref, bd_ref, rp_ref, o_ref, *, nq):
    w = WINDOW
    dh = SWA_HEAD_DIM
    grp = SWA_Q_HEADS // SWA_KV_HEADS
    n = pl.program_id(1)

    def head_rms(x, gain, bd):
        sq = x * x
        hi = sq.astype(BF16)
        lo = (sq - hi.astype(F32)).astype(BF16)
        ssum = jnp.dot(hi, bd, preferred_element_type=F32) + jnp.dot(lo, bd, preferred_element_type=F32)
        return x * lax.rsqrt(ssum * (1.0 / dh) + EPS) * gain

    def rope(x, cos, sin, rp):
        partner = jnp.dot(x.astype(BF16), rp, preferred_element_type=F32)
        return x * cos + partner * sin

    cos, sin = cos_ref[...], sin_ref[...]
    cosq = jnp.concatenate([cos] * (SWA_Q_W // LANES), axis=1)
    sinq = jnp.concatenate([sin] * (SWA_Q_W // LANES), axis=1)
    q = rope(head_rms(q_ref[0].astype(F32), qg_ref[...], bd_ref[...]), cosq, sinq, rp_ref[...])
    q = (q * (dh ** -0.5)).astype(BF16)
    bdk = bd_ref[0:SWA_KV_W, 0:SWA_KV_W]
    rpk = rp_ref[0:SWA_KV_W, 0:SWA_KV_W]
    kcur = rope(head_rms(kc_ref[0].astype(F32), kg_ref[...], bdk), cos, sin, rpk)
    kprev = rope(head_rms(kp_ref[0].astype(F32), kg_ref[...], bdk), cosp_ref[...], sinp_ref[...], rpk)
    kall = jnp.concatenate([kprev, kcur], axis=0).astype(BF16)
    vall = jnp.concatenate([vp_ref[0], vc_ref[0]], axis=0)

    i = lax.broadcasted_iota(jnp.int32, (w, 2 * w), 0)
    j = lax.broadcasted_iota(jnp.int32, (w, 2 * w), 1)
    band = (j > i) & (j <= i + w)
    masks = [band & ((j >= w) | (n > 0))] + [band] * (nq - 1)
    units = [(qb, hq) for qb in range(nq) for hq in range(SWA_Q_HEADS)]
    kh = {(qb, hk): kall[qb * w:(qb + 2) * w, hk * dh:(hk + 1) * dh] for qb in range(nq) for hk in range(SWA_KV_HEADS)}
    vh = {(qb, hk): vall[qb * w:(qb + 2) * w, hk * dh:(hk + 1) * dh] for qb in range(nq) for hk in range(SWA_KV_HEADS)}
    sc = [lax.dot_general(q[qb * w:(qb + 1) * w, hq * dh:(hq + 1) * dh], kh[qb, hq // grp], (((1,), (1,)), ((), ())),
                          preferred_element_type=F32) for qb, hq in units]
    sc = [jnp.where(masks[qb], sc[u], NEG_INF) for u, (qb, hq) in enumerate(units)]
    m = [jnp.maximum(jnp.max(sc[u], axis=-1, keepdims=True), sink_ref[hq]) for u, (qb, hq) in enumerate(units)]
    p = [jnp.exp(sc[u] - m[u]) for u in range(len(units))]
    denom = [jnp.sum(p[u], axis=-1, keepdims=True) + jnp.exp(sink_ref[hq] - m[u]) for u, (qb, hq) in enumerate(units)]
    pv = [jnp.dot(p[u].astype(BF16), vh[qb, hq // grp], preferred_element_type=F32) for u, (qb, hq) in enumerate(units)]
    for qb in range(nq):
        row = [pv[qb * SWA_Q_HEADS + hq] * (1.0 / denom[qb * SWA_Q_HEADS + hq]) for hq in range(SWA_Q_HEADS)]
        o_ref[0, qb * w:(qb + 1) * w, :] = jnp.concatenate(row, axis=-1).astype(o_ref.dtype)


def _rope_tables(s):
    half = ROT_DIM // 2
    pos = jnp.arange(s, dtype=F32)
    inv_freq = 1.0 / (ROPE_THETA ** (jnp.arange(half, dtype=F32) * 2.0 / ROT_DIM))
    ang = pos[:, None] * inv_freq[None, :]
    cos, sin = jnp.cos(ang), jnp.sin(ang)
    ones = jnp.ones((s, SWA_HEAD_DIM - ROT_DIM), F32)
    cos_h = jnp.concatenate([cos, cos, ones], axis=1)
    sin_h = jnp.concatenate([-sin, sin, 0.0 * ones], axis=1)
    reps = LANES // SWA_HEAD_DIM
    return jnp.tile(cos_h, (1, reps)), jnp.tile(sin_h, (1, reps))


def _swa(p, sinks, q_gain, k_gain, bsz, s, *, nq=4):
    w = WINDOW
    blk = nq * w
    cos, sin = _rope_tables(s)
    qg = jnp.tile(q_gain.astype(F32), SWA_Q_HEADS).reshape(1, SWA_Q_W)
    kg = jnp.tile(k_gain.astype(F32), SWA_KV_HEADS).reshape(1, SWA_KV_W)
    lanes = np.arange(SWA_Q_W)
    seg = lanes // SWA_HEAD_DIM
    bd = jnp.asarray(seg[:, None] == seg[None, :], dtype=BF16)
    half = ROT_DIM // 2
    in_head = lanes % SWA_HEAD_DIM
    src = np.where(in_head < half, lanes + half, lanes - half)
    rp = np.zeros((SWA_Q_W, SWA_Q_W), np.float32)
    rot = in_head < ROT_DIM
    rp[src[rot], lanes[rot]] = 1.0
    rp = jnp.asarray(rp, dtype=BF16)
    qblk, kblk, vblk = EVEN_COL_Q // SWA_Q_W, EVEN_COL_K // SWA_KV_W, EVEN_COL_V // SWA_KV_W
    prev = lambda i: jnp.maximum(i * nq - 1, 0)
    return pl.pallas_call(
        functools.partial(_swa_kernel, nq=nq),
        grid=(bsz, s // blk),
        in_specs=[
            pl.BlockSpec(memory_space=pltpu.SMEM),
            pl.BlockSpec((1, blk, SWA_Q_W), lambda b, i: (b, i, qblk)),
            pl.BlockSpec((1, blk, SWA_KV_W), lambda b, i: (b, i, kblk)),
            pl.BlockSpec((1, w, SWA_KV_W), lambda b, i: (b, prev(i), kblk)),
            pl.BlockSpec((1, blk, SWA_KV_W), lambda b, i: (b, i, vblk)),
            pl.BlockSpec((1, w, SWA_KV_W), lambda b, i: (b, prev(i), vblk)),
            pl.BlockSpec((blk, LANES), lambda b, i: (i, 0)),
            pl.BlockSpec((blk, LANES), lambda b, i: (i, 0)),
            pl.BlockSpec((w, LANES), lambda b, i: (prev(i), 0)),
            pl.BlockSpec((w, LANES), lambda b, i: (prev(i), 0)),
            pl.BlockSpec((1, SWA_Q_W), lambda b, i: (0, 0)),
            pl.BlockSpec((1, SWA_KV_W), lambda b, i: (0, 0)),
            pl.BlockSpec((SWA_Q_W, SWA_Q_W), lambda b, i: (0, 0)),
            pl.BlockSpec((SWA_Q_W, SWA_Q_W), lambda b, i: (0, 0)),
        ],
        out_specs=pl.BlockSpec((1, blk, SWA_Q_W), lambda b, i: (b, i, 0)),
        out_shape=jax.ShapeDtypeStruct((bsz, s, SWA_Q_W), BF16),
        compiler_params=_cparams("parallel", "parallel"), name="swa",
    )(sinks.astype(F32), p, p, p, p, p, cos, sin, cos, sin, qg, kg, bd, rp)


def _softplus(x):
    return jnp.maximum(x, 0.0) + jnp.log1p(jnp.exp(-jnp.abs(x)))


def _split3(x):
    x1 = x.astype(BF16)
    r1 = x - x1.astype(F32)
    x2 = r1.astype(BF16)
    return x1, x2, (r1 - x2.astype(F32)).astype(BF16)


def _gdn_kernel(q_ref, k_ref, v_ref, z_ref, gate_ref, alog_ref, dt_ref, gain_ref, o_ref, state_ref, *, blk):
    c = GDN_CHUNK
    dk = GDN_HEAD_DIM
    nh = GDN_HEADS
    sub = LANES
    nt = (((1,), (1,)), ((), ()))

    @pl.when(pl.program_id(1) == 0)
    def _():
        state_ref[...] = jnp.zeros_like(state_ref)

    gates = gate_ref[0]
    g_all = -jnp.exp(alog_ref[...]) * _softplus(gates + dt_ref[...])
    beta_all = 1.0 / (1.0 + jnp.exp(-gates))
    ii = lax.broadcasted_iota(jnp.int32, (sub, sub), 0)
    jj = lax.broadcasted_iota(jnp.int32, (sub, sub), 1)
    same = (ii // c) == (jj // c)
    incl = same & (ii >= jj)
    strict = same & (ii > jj)
    tri = jnp.where(incl, 1.0, 0.0).astype(BF16)
    eye = jnp.where(ii == jj, 1.0, 0.0).astype(F32)
    er = lax.broadcasted_iota(jnp.int32, (8, LANES), 0)
    ec = lax.broadcasted_iota(jnp.int32, (8, LANES), 1)
    eye8 = jnp.where(er == ec, 1.0, 0.0).astype(BF16)
    g_parts = _split3(g_all)
    g_rows = sum(lax.dot_general(eye8, gp, nt, preferred_element_type=F32) for gp in g_parts)
    gr_parts = _split3(g_rows)
    gc_cols, gc_rows = [], []
    for sb in range(blk // sub):
        rows = slice(sb * sub, (sb + 1) * sub)
        gc_cols.append(sum(jnp.dot(tri, gp[rows], preferred_element_type=F32) for gp in g_parts))
        gc_rows.append(sum(lax.dot_general(gp[:, rows], tri, nt, preferred_element_type=F32) for gp in gr_parts))

    n_sub = blk // sub
    units = [(hd, sb) for hd in range(nh) for sb in range(n_sub)]
    dot = functools.partial(jnp.dot, preferred_element_type=F32)

    def dot_nt(x, y):
        return lax.dot_general(x, y, nt, preferred_element_type=F32)

    qs, ks, vs, betas = [], [], [], []
    for hd in range(nh):
        cols = slice(hd * dk, (hd + 1) * dk)
        q = q_ref[0, :, cols].astype(F32)
        k = k_ref[0, :, cols].astype(F32)
        qs.append(q * lax.rsqrt(jnp.sum(q * q, axis=-1, keepdims=True) + EPS) * (dk ** -0.5))
        ks.append(k * lax.rsqrt(jnp.sum(k * k, axis=-1, keepdims=True) + EPS))
        vs.append(v_ref[0, :, cols].astype(F32))
        betas.append(beta_all[:, nh + hd:nh + hd + 1])

    def rows_of(sb):
        return slice(sb * sub, (sb + 1) * sub)

    gcb = [jnp.broadcast_to(gc_cols[sb][:, hd:hd + 1], (sub, LANES)) for hd, sb in units]
    decay = [jnp.exp(jnp.where(incl, gcb[u] - gc_rows[sb][hd:hd + 1, :], NEG_INF)) for u, (hd, sb) in enumerate(units)]
    eg = [jnp.exp(g) for g in gcb]
    q_u = [qs[hd][rows_of(sb)] for hd, sb in units]
    k_u = [ks[hd][rows_of(sb)] for hd, sb in units]
    kb = [k_u[u] * betas[hd][rows_of(sb)] for u, (hd, sb) in enumerate(units)]
    kbf = [x.astype(BF16) for x in k_u]
    a = [dot_nt(kb[u].astype(BF16), kbf[u]) for u in range(len(units))]
    m = [jnp.where(strict, -(a[u] * decay[u]), 0.0) for u in range(len(units))]
    r = [eye + x for x in m]
    mb = [x.astype(BF16) for x in m]
    pw = [dot(x, x) for x in mb]
    for _ in range(int(np.log2(c)) - 2):
        pb = [x.astype(BF16) for x in pw]
        both = [dot(jnp.concatenate([r[u].astype(BF16), pb[u]], axis=0), pb[u]) for u in range(len(units))]
        r = [r[u] + both[u][:sub] for u in range(len(units))]
        pw = [x[sub:] for x in both]
    r = [r[u] + dot(r[u].astype(BF16), pw[u].astype(BF16)) for u in range(len(units))]
    rhs = [jnp.concatenate([kb[u] * eg[u], vs[hd][rows_of(sb)] * betas[hd][rows_of(sb)]], axis=1).astype(BF16)
           for u, (hd, sb) in enumerate(units)]
    wu = [dot(r[u].astype(BF16), rhs[u]).astype(BF16) for u in range(len(units))]
    qk = [dot_nt(q_u[u].astype(BF16), kbf[u]) for u in range(len(units))]
    iwu = [dot((qk[u] * decay[u]).astype(BF16), wu[u]) for u in range(len(units))]
    qe = [(q_u[u] * eg[u] - iwu[u][:, :dk]).astype(BF16) for u in range(len(units))]
    per_chunk = sub // c
    kwu, g_last = {}, {}
    for u, (hd, sb) in enumerate(units):
        for ci in range(per_chunk):
            crow = slice(ci * c, (ci + 1) * c)
            gl = gcb[u][(ci + 1) * c - 1:(ci + 1) * c, :]
            k_dec = (k_u[u][crow] * jnp.exp(gl - gcb[u][crow])).astype(BF16)
            g_last[hd, sb * per_chunk + ci] = jnp.exp(gl)
            kwu[hd, sb * per_chunk + ci] = lax.dot_general(k_dec, wu[u][crow], (((0,), (0,)), ((), ())),
                                                           preferred_element_type=F32)
    outs = [[] for _ in range(nh)]
    for ch in range(blk // c):
        u_sb, ci = ch // per_chunk, ch % per_chunk
        crow = slice(ci * c, (ci + 1) * c)
        states = [state_ref[hd] for hd in range(nh)]
        sbf = [x.astype(BF16) for x in states]
        for hd in range(nh):
            u = hd * n_sub + u_sb
            outs[hd].append(dot(qe[u][crow], sbf[hd]) + iwu[u][crow, dk:])
        corr = [dot(kwu[hd, ch][:, :dk].astype(BF16), sbf[hd]) for hd in range(nh)]
        for hd in range(nh):
            state_ref[hd] = states[hd] * g_last[hd, ch] + (kwu[hd, ch][:, dk:] - corr[hd])
    for hd in range(nh):
        out = jnp.concatenate(outs[hd], axis=0)
        o = out * lax.rsqrt(jnp.mean(out * out, axis=-1, keepdims=True) + EPS) * gain_ref[...]
        z = z_ref[0, :, hd * dk:(hd + 1) * dk].astype(F32)
        o_ref[0, :, hd * dk:(hd + 1) * dk] = (o * _silu(z)).astype(o_ref.dtype)


def _gdn(p, pg, a_log, dt_bias, gdn_gain, bsz, s, *, blk=512):
    dk = GDN_HEAD_DIM
    assert EVEN_COL_GDN % GDN_W == 0 and EVEN_COL_Z % GDN_W == 0
    lane_pad = (0, LANES - GDN_HEADS)
    alog = jnp.pad(a_log.astype(F32), lane_pad).reshape(1, LANES)
    dt = jnp.pad(dt_bias.astype(F32), lane_pad).reshape(1, LANES)
    return pl.pallas_call(
        functools.partial(_gdn_kernel, blk=blk),
        grid=(bsz, s // blk),
        in_specs=[pl.BlockSpec((1, blk, GDN_W), lambda b, i: (b, i, EVEN_COL_GDN // GDN_W)),
                  pl.BlockSpec((1, blk, GDN_W), lambda b, i: (b, i, EVEN_COL_GDN // GDN_W + 1)),
                  pl.BlockSpec((1, blk, GDN_W), lambda b, i: (b, i, EVEN_COL_GDN // GDN_W + 2)),
                  pl.BlockSpec((1, blk, GDN_W), lambda b, i: (b, i, EVEN_COL_Z // GDN_W)),
                  pl.BlockSpec((1, blk, LANES), lambda b, i: (b, i, 0)),
                  pl.BlockSpec((1, LANES), lambda b, i: (0, 0)),
                  pl.BlockSpec((1, LANES), lambda b, i: (0, 0)),
                  pl.BlockSpec((1, dk), lambda b, i: (0, 0))],
        out_specs=pl.BlockSpec((1, blk, GDN_W), lambda b, i: (b, i, 0)),
        out_shape=jax.ShapeDtypeStruct((bsz, s, GDN_W), BF16),
        scratch_shapes=[pltpu.VMEM((GDN_HEADS, dk, dk), F32)],
        compiler_params=_cparams("parallel", "arbitrary"), name="gdn",
    )(p, p, p, p, pg, alog, dt, gdn_gain.astype(F32).reshape(1, dk))


MOE_BM = 512
ROUTER_ROWS = LANES
EXPERT_ROW0 = 8


SLAB = 8
DMA_UNROLL = 4


def _store_slabs(ref, val, rows):
    for sl in range(SLAB):
        ref[pl.ds(sl, rows, stride=SLAB), :] = val[:, sl * LANES:(sl + 1) * LANES]


def _load_slabs(ref, rows):
    return jnp.concatenate([ref[pl.ds(sl, rows, stride=SLAB), :] for sl in range(SLAB)], axis=1)


def _slab(ref, row):
    return ref.at[pl.ds(pl.multiple_of(row * SLAB, SLAB), SLAB), :]


def _ffn_norm(x, gain):
    return x * lax.rsqrt(jnp.mean(x * x, axis=-1, keepdims=True) + EPS) * gain


def _route_tile(x, g_ref, wr_ref, wrlo_ref, br_ref, eid_ref, gate_ref):
    h = _ffn_norm(x, g_ref[...])
    tm = h.shape[0]
    h_hi = h.astype(BF16)
    h_lo = (h - h_hi.astype(F32)).astype(BF16)
    nt = (((1,), (1,)), ((), ()))
    logits = (lax.dot_general(wr_ref[...], h_hi, nt, preferred_element_type=F32)
              + lax.dot_general(wr_ref[...], h_lo, nt, preferred_element_type=F32)
              + lax.dot_general(wrlo_ref[...], h_hi, nt, preferred_element_type=F32)) + br_ref[...]
    row = lax.broadcasted_iota(jnp.int32, (8, tm), 0)
    gl = jnp.where(row < N_GROUPS, logits[0:8, :], NEG_INF)
    gmax = jnp.max(gl, axis=0, keepdims=True)
    grp_p = 1.0 / jnp.sum(jnp.exp(gl - gmax), axis=0, keepdims=True)
    grp_idx = jnp.min(jnp.where(gl == gmax, row, 8), axis=0, keepdims=True)
    el = jnp.zeros((8, tm), F32)
    for g in range(N_GROUPS):
        lo = EXPERT_ROW0 + g * EXPERTS_PER_GROUP
        el = jnp.where(grp_idx == g, logits[lo:lo + EXPERTS_PER_GROUP, :], el)
    m1 = jnp.max(el, axis=0, keepdims=True)
    i1 = jnp.min(jnp.where(el == m1, row, 8), axis=0, keepdims=True)
    el2 = jnp.where(row == i1, NEG_INF, el)
    m2 = jnp.max(el2, axis=0, keepdims=True)
    i2 = jnp.min(jnp.where(el2 == m2, row, 8), axis=0, keepdims=True)
    z = jnp.sum(jnp.exp(el - m1), axis=0, keepdims=True)
    p1 = 1.0 / z
    p2 = jnp.exp(m2 - m1) / z
    scale = grp_p / (p1 + p2)
    e1 = grp_idx * EXPERTS_PER_GROUP + i1
    e2 = grp_idx * EXPERTS_PER_GROUP + i2
    eid_ref[...] = jnp.where(row == 0, e1, jnp.where(row == 1, e2, 0))
    gate_ref[...] = jnp.where(row == 0, p1 * scale, jnp.where(row == 1, p2 * scale, 0.0))


def _router_operands(gain, w_rg, b_rg, w_re, b_re):
    d = w_rg.shape[0]
    wr = jnp.zeros((ROUTER_ROWS, d), F32)
    wr = wr.at[0:N_GROUPS].set(w_rg.astype(F32).T).at[EXPERT_ROW0:EXPERT_ROW0 + N_EXPERTS].set(w_re.astype(F32).T)
    br = jnp.zeros((ROUTER_ROWS, 1), F32)
    br = br.at[0:N_GROUPS, 0].set(b_rg.astype(F32)).at[EXPERT_ROW0:EXPERT_ROW0 + N_EXPERTS, 0].set(b_re.astype(F32))
    wr_hi = wr.astype(BF16)
    wr_lo = (wr - wr_hi.astype(F32)).astype(BF16)
    return gain.astype(F32).reshape(1, d), wr_hi, wr_lo, br


def _rank_kernel(eid_ref, dest_ref, meta_ref, cnt_ref, off_ref, *, bm):
    ph = pl.program_id(0)
    i = pl.program_id(1)
    tm = eid_ref.shape[1]
    erow = lax.broadcasted_iota(jnp.int32, (N_EXPERTS, tm), 0)
    oh0 = jnp.where(erow == eid_ref[0:1, :], 1.0, 0.0).astype(F32)
    oh1 = jnp.where(erow == eid_ref[1:2, :], 1.0, 0.0).astype(F32)
    oh = oh0 + oh1
    tile_cnt = jnp.sum(oh, axis=1, keepdims=True)

    @pl.when((ph == 0) & (i == 0))
    def _():
        cnt_ref[...] = jnp.zeros_like(cnt_ref)

    @pl.when(ph == 0)
    def _():
        cnt_ref[...] += tile_cnt

    @pl.when((ph == 1) & (i == 0))
    def _():
        cnt = cnt_ref[...]
        padded = jnp.ceil(cnt * (1.0 / bm)) * bm
        er = lax.broadcasted_iota(jnp.int32, (N_EXPERTS, N_EXPERTS), 0)
        ec = lax.broadcasted_iota(jnp.int32, (N_EXPERTS, N_EXPERTS), 1)
        lower = jnp.where(ec < er, 1.0, 0.0).astype(F32)
        start = jnp.dot(lower, padded, preferred_element_type=F32, precision=lax.Precision.HIGHEST)
        off_ref[...] = start
        end = start + padded
        nb = meta_ref.shape[1]
        blk_row = lax.broadcasted_iota(jnp.int32, (N_EXPERTS, nb), 1).astype(F32) * bm
        endb = jnp.concatenate([end] * (nb // LANES), axis=1)
        be = jnp.sum(jnp.where(endb <= blk_row, 1.0, 0.0), axis=0, keepdims=True)
        be = jnp.minimum(be, N_EXPERTS - 1.0)
        used = jnp.concatenate([end[N_EXPERTS - 1:N_EXPERTS, :]] * (nb // LANES), axis=1) * (1.0 / bm)
        elane = lax.broadcasted_iota(jnp.int32, (N_EXPERTS, nb), 1)
        erow2 = lax.broadcasted_iota(jnp.int32, (N_EXPERTS, nb), 0)
        ends = jnp.sum(jnp.where(elane == erow2, endb, 0.0), axis=0, keepdims=True)
        mrow = lax.broadcasted_iota(jnp.int32, (8, nb), 0)
        meta_ref[...] = jnp.where(mrow == 0, be, jnp.where(mrow == 1, used, jnp.where(mrow == 2, ends, 0.0))
                                  ).astype(jnp.int32)
        cnt_ref[...] = jnp.zeros_like(cnt_ref)

    @pl.when(ph == 1)
    def _():
        kr = lax.broadcasted_iota(jnp.int32, (tm, tm), 0)
        kc = lax.broadcasted_iota(jnp.int32, (tm, tm), 1)
        su = jnp.where(kr < kc, 1.0, 0.0).astype(BF16)
        csum = jnp.dot(oh.astype(BF16), su, preferred_element_type=F32)
        pos = csum + (off_ref[...] + cnt_ref[...])[:, 0:1]
        d0 = jnp.sum(oh0 * pos, axis=0, keepdims=True)
        d1 = jnp.sum(oh1 * pos, axis=0, keepdims=True)
        row = lax.broadcasted_iota(jnp.int32, (8, tm), 0)
        dest_ref[...] = jnp.where(row == 0, d0, jnp.where(row == 1, d1, 0.0)).astype(jnp.int32)
        cnt_ref[...] += tile_cnt


def _moe_rank(eid, n_blocks, *, bm, tm=1024):
    t = eid.shape[1]
    nb = -(-n_blocks // LANES) * LANES
    return pl.pallas_call(
        functools.partial(_rank_kernel, bm=bm), grid=(2, t // tm),
        in_specs=[pl.BlockSpec((8, tm), lambda p, i: (0, i))],
        out_specs=[pl.BlockSpec((8, tm), lambda p, i: (0, i * p)),
                   pl.BlockSpec((8, nb), lambda p, i: (0, 0))],
        out_shape=[jax.ShapeDtypeStruct((8, t), jnp.int32), jax.ShapeDtypeStruct((8, nb), jnp.int32)],
        scratch_shapes=[pltpu.VMEM((N_EXPERTS, LANES), F32), pltpu.VMEM((N_EXPERTS, LANES), F32)],
        compiler_params=_cparams("arbitrary", "arbitrary"), name="moe_rank")(eid)


def _dispatch_kernel(dest_ref, ends_ref, x_ref, g_ref, rows_hbm, zero_ref, hbuf_ref, sem, rsem, *, tm, bm):
    @pl.when(pl.program_id(0) == 0)
    def _():
        zero_ref[...] = jnp.zeros_like(zero_ref)

        def tail(e):
            start = ends_ref[0, e] - bm
            return pltpu.make_async_copy(zero_ref, rows_hbm.at[pl.ds(pl.multiple_of(start * SLAB, SLAB), bm * SLAB), :],
                                         sem)

        def nonempty(e):
            return ends_ref[0, e] > jnp.where(e > 0, ends_ref[0, jnp.maximum(e - 1, 0)], 0)

        def fill(e, carry):
            @pl.when(nonempty(e))
            def _():
                tail(e).start()
            return carry

        def drain(e, carry):
            @pl.when(nonempty(e))
            def _():
                tail(e).wait()
            return carry

        lax.fori_loop(0, N_EXPERTS, fill, 0)
        lax.fori_loop(0, N_EXPERTS, drain, 0)

        def unused(b):
            return pltpu.make_async_copy(
                zero_ref, rows_hbm.at[pl.ds(pl.multiple_of(b * (bm * SLAB), SLAB), bm * SLAB), :], sem)

        def fill_unused(b, carry):
            unused(b).start()
            return carry

        def drain_unused(b, carry):
            unused(b).wait()
            return carry

        first_unused = ends_ref[0, N_EXPERTS - 1] // bm
        n_blocks = rows_hbm.shape[0] // (bm * SLAB)
        lax.fori_loop(first_unused, n_blocks, fill_unused, 0)
        lax.fori_loop(first_unused, n_blocks, drain_unused, 0)

    i = pl.program_id(0)
    slot = lax.rem(i, 2)

    def drain_slot(s):
        for _ in range(TOP_K):
            pltpu.make_async_copy(hbuf_ref.at[s], rows_hbm.at[pl.ds(0, tm * SLAB), :], rsem.at[s]).wait()

    for s in range(2):
        @pl.when((slot == s) & (i >= 2))
        def _():
            drain_slot(s)

        @pl.when(slot == s)
        def _():
            _store_slabs(hbuf_ref.at[s], _ffn_norm(x_ref[...], g_ref[...]), tm)

            def issue(g, carry):
                for u in range(DMA_UNROLL):
                    tok = g * DMA_UNROLL + u
                    for k in range(TOP_K):
                        pltpu.make_async_copy(_slab(hbuf_ref.at[s], tok),
                                              _slab(rows_hbm, dest_ref[0, 0, k * tm + tok]),
                                              rsem.at[s]).start(priority=k)
                return carry

            lax.fori_loop(0, tm // DMA_UNROLL, issue, 0)

    last = pl.num_programs(0) - 1

    @pl.when(i == last)
    def _():
        for s in range(2):
            @pl.when((slot == s) | (last >= 1))
            def _():
                drain_slot(s)


def _moe_dispatch(dest_tiles, expert_ends, x2, gain, n_rows, *, tm, bm):
    t, d = x2.shape
    return pl.pallas_call(
        functools.partial(_dispatch_kernel, tm=tm, bm=bm), grid=(t // tm,),
        in_specs=[pl.BlockSpec((1, 1, TOP_K * tm), lambda i: (i, 0, 0), memory_space=pltpu.SMEM),
                  pl.BlockSpec(memory_space=pltpu.SMEM),
                  pl.BlockSpec((tm, d), lambda i: (i, 0)),
                  pl.BlockSpec((1, d), lambda i: (0, 0))],
        out_specs=pl.BlockSpec(memory_space=pl.ANY),
        out_shape=jax.ShapeDtypeStruct((n_rows * SLAB, LANES), F32),
        scratch_shapes=[pltpu.VMEM((bm * SLAB, LANES), F32), pltpu.VMEM((2, tm * SLAB, LANES), F32),
                        pltpu.SemaphoreType.DMA(()), pltpu.SemaphoreType.DMA((2,))],
        compiler_params=pltpu.CompilerParams(dimension_semantics=("arbitrary",), has_side_effects=True,
                                             vmem_limit_bytes=VMEM_LIMIT),
        name="moe_dispatch")(dest_tiles, expert_ends, x2, gain.reshape(1, d))


COMBINE_TM = 512
EXPERT_RING = 3


def _experts_kernel(be_ref, used_ref, x_hbm, wg_ref, wu_ref, wd_ref, y_ref, wgb_ref, wub_ref, wdb_ref, xbuf_ref,
                    xsem):
    b = pl.program_id(0)
    used = used_ref[0]
    bm = y_ref.shape[0] // SLAB

    def fetch(blk):
        slot = lax.rem(blk, EXPERT_RING)
        rows = pl.ds(pl.multiple_of(blk * (bm * SLAB), SLAB), bm * SLAB)
        return pltpu.make_async_copy(x_hbm.at[rows, :], xbuf_ref.at[slot], xsem.at[slot])

    @pl.when(b == 0)
    def _():
        for first in range(EXPERT_RING - 1):
            @pl.when(first < used)
            def _():
                fetch(first).start()

    @pl.when(b + (EXPERT_RING - 1) < used)
    def _():
        fetch(b + (EXPERT_RING - 1)).start()

    prev = be_ref[jnp.maximum(b - 1, 0)]

    @pl.when((b == 0) | (be_ref[b] != prev))
    def _():
        wgb_ref[...] = wg_ref[0, 0].astype(BF16)
        wub_ref[...] = wu_ref[0, 0].astype(BF16)
        wdb_ref[...] = wd_ref[0, 0].astype(BF16)

    @pl.when(b < used)
    def _():
        fetch(b).wait()
        x = _load_slabs(xbuf_ref.at[lax.rem(b, EXPERT_RING)], bm).astype(BF16)
        gate = jnp.dot(x, wgb_ref[...], preferred_element_type=F32)
        up = jnp.dot(x, wub_ref[...], preferred_element_type=F32)
        hid = (_silu(gate) * up).astype(BF16)
        _store_slabs(y_ref, jnp.dot(hid, wdb_ref[...], preferred_element_type=F32), bm)

    @pl.when(b >= used_ref[0])
    def _():
        y_ref[...] = jnp.zeros_like(y_ref)


def _moe_experts(block_expert, n_used, x_rows, w_gate, w_up, w_down, layer, *, bm):
    n_rows = x_rows.shape[0] // SLAB
    d, de = w_gate.shape[2], w_gate.shape[3]
    grid_spec = pltpu.PrefetchScalarGridSpec(
        num_scalar_prefetch=2, grid=(n_rows // bm,),
        in_specs=[pl.BlockSpec(memory_space=pl.ANY),
                  pl.BlockSpec((1, 1, d, de), lambda b, be, nu: (layer, be[b], 0, 0)),
                  pl.BlockSpec((1, 1, d, de), lambda b, be, nu: (layer, be[b], 0, 0)),
                  pl.BlockSpec((1, 1, de, d), lambda b, be, nu: (layer, be[b], 0, 0))],
        out_specs=pl.BlockSpec((bm * SLAB, LANES), lambda b, be, nu: (b, 0)),
        scratch_shapes=[pltpu.VMEM((d, de), BF16), pltpu.VMEM((d, de), BF16), pltpu.VMEM((de, d), BF16),
                        pltpu.VMEM((EXPERT_RING, bm * SLAB, LANES), F32), pltpu.SemaphoreType.DMA((EXPERT_RING,))])
    return pl.pallas_call(
        _experts_kernel, grid_spec=grid_spec, out_shape=jax.ShapeDtypeStruct((n_rows * SLAB, LANES), F32),
        compiler_params=_cparams("arbitrary"), name="moe_experts",
    )(block_expert, n_used, x_rows, w_gate, w_up, w_down)


def _combine_kernel(dest_ref, next_ref, x_ref, gate_ref, y_hbm, o_ref, buf_ref, sem, *, tm):
    _combine_tile(dest_ref, next_ref, x_ref, gate_ref, y_hbm, o_ref, buf_ref, sem, tm)


def _combine_tile(dest_ref, next_ref, x_ref, gate_ref, y_hbm, o_ref, buf_ref, sem, tm):
    i = pl.program_id(0)
    slot = lax.rem(i, 2)

    def gather(idx_ref, s):
        def issue(g, carry):
            for u in range(DMA_UNROLL):
                tok = g * DMA_UNROLL + u
                for k in range(TOP_K):
                    pltpu.make_async_copy(_slab(y_hbm, idx_ref[0, 0, k * tm + tok]), _slab(buf_ref.at[s, k], tok),
                                          sem.at[s]).start(priority=k)
            return carry

        lax.fori_loop(0, tm // DMA_UNROLL, issue, 0)

    for s in range(2):
        @pl.when((i == 0) & (slot == s))
        def _():
            gather(dest_ref, s)

        @pl.when((i + 1 < pl.num_programs(0)) & (slot == 1 - s))
        def _():
            gather(next_ref, s)

    er = lax.broadcasted_iota(jnp.int32, (8, LANES), 0)
    ec = lax.broadcasted_iota(jnp.int32, (8, LANES), 1)
    eye = jnp.where(er == ec, 1.0, 0.0).astype(F32)
    gcol = lax.dot_general(gate_ref[...], eye, (((0,), (0,)), ((), ())), preferred_element_type=F32,
                           precision=lax.Precision.HIGHEST)
    for s in range(2):
        @pl.when(slot == s)
        def _():
            for k in range(TOP_K):
                pltpu.make_async_copy(y_hbm.at[pl.ds(0, tm * SLAB), :], buf_ref.at[s, k], sem.at[s]).wait()
            o_ref[...] = (x_ref[...] + gcol[:, 0:1] * _load_slabs(buf_ref.at[s, 0], tm)
                          + gcol[:, 1:2] * _load_slabs(buf_ref.at[s, 1], tm))


def _moe_combine(dest_tiles, x2, gates, y_rows, *, tm):
    t, d = x2.shape
    n = t // tm
    dest_spec = functools.partial(pl.BlockSpec, (1, 1, TOP_K * tm), memory_space=pltpu.SMEM)
    return pl.pallas_call(
        functools.partial(_combine_kernel, tm=tm), grid=(n,),
        in_specs=[dest_spec(lambda i: (i, 0, 0)),
                  dest_spec(lambda i: (jnp.minimum(i + 1, n - 1), 0, 0)),
                  pl.BlockSpec((tm, d), lambda i: (i, 0)),
                  pl.BlockSpec((8, tm), lambda i: (0, i)),
                  pl.BlockSpec(memory_space=pl.ANY)],
        out_specs=pl.BlockSpec((tm, d), lambda i: (i, 0)),
        out_shape=jax.ShapeDtypeStruct((t, d), F32),
        scratch_shapes=[pltpu.VMEM((2, TOP_K, tm * SLAB, LANES), F32), pltpu.SemaphoreType.DMA((2,))],
        compiler_params=_cparams("arbitrary"), name="moe_combine")(dest_tiles, dest_tiles, x2, gates, y_rows)


def _dest_tiles(dest, tm):
    t = dest.shape[1]
    return dest[0:TOP_K].reshape(TOP_K, t // tm, tm).transpose(1, 0, 2).reshape(t // tm, 1, TOP_K * tm)


def _moe_expert_rows(x2, eid, gain, w_gate, w_up, w_down, layer, *, tm=1024):
    t, d = x2.shape
    bm = MOE_BM
    n_rows = t * TOP_K + N_EXPERTS * bm
    n_blocks = n_rows // bm
    dest, meta = _moe_rank(eid, n_blocks, bm=bm)
    x_rows = _moe_dispatch(_dest_tiles(dest, tm), meta[2:3, :N_EXPERTS], x2, gain, n_rows, tm=tm, bm=bm)
    y_rows = _moe_experts(meta[0, :n_blocks], meta[1, 0:1], x_rows, w_gate, w_up, w_down, layer, bm=bm)
    return dest, y_rows


def _even_odd_perm():
    half = RET_QK_DIM // 2
    src = np.concatenate([2 * np.arange(half), 2 * np.arange(half) + 1])
    perm = np.zeros((RET_QK_DIM, RET_QK_DIM), np.float32)
    perm[src, np.arange(RET_QK_DIM)] = 1.0
    return jnp.asarray(perm, dtype=BF16)


def kernel(x, ln_mix_even, w_in_even, conv_w_even, a_log_even, dt_bias_even, q_gain_even, k_gain_even, sinks_even, gdn_gain_even, w_out_even, ln_mix_odd, w_in_odd, w_out_odd, ln_ffn, w_router_group, b_router_group, w_router_expert, b_router_expert, w_gate, w_up, w_down):
    bsz, s, d = x.shape
    t = bsz * s
    x2 = x.reshape(t, d).astype(F32)
    depth = ln_ffn.shape[0]
    pending = None

    def settle(x2, pending):
        dest, gates, y_rows = pending
        return _moe_combine(_dest_tiles(dest, COMBINE_TM), x2, gates, y_rows, tm=COMBINE_TM)

    for layer in range(depth):
        i = layer // 2
        router = _router_operands(ln_ffn[layer], w_router_group[layer], b_router_group[layer],
                                  w_router_expert[layer], b_router_expert[layer])
        if layer % 2 == 0:
            if pending is not None:
                x2, pending = settle(x2, pending), None
            w_in = w_in_even[i]
            swa_end = SWA_Q_W + 2 * SWA_KV_W
            w_main = jnp.concatenate([w_in[:, :SWA_Q_W], w_in[:, swa_end:EVEN_MAIN], w_in[:, SWA_Q_W:swa_end]],
                                     axis=1).astype(BF16)
            w_gates = jnp.pad(w_in[:, EVEN_MAIN:].astype(F32), ((0, 0), (0, LANES - 2 * GDN_HEADS)))
            p, pg = _norm_proj(x2, ln_mix_even[i], w_main, w_gates, conv_w=conv_w_even[i],
                               conv_cols=(EVEN_COL_GDN, EVEN_COL_Z), seq_len=s, tm=1024)
            p = p.reshape(bsz, s, EVEN_MAIN)
            y_a = _swa(p, sinks_even[i], q_gain_even[i], k_gain_even[i], bsz, s)
            y_b = _gdn(p, pg.reshape(bsz, s, LANES), a_log_even[i], dt_bias_even[i], gdn_gain_even[i], bsz, s)
            w_out = w_out_even[i].astype(BF16)
            x2, eid, gates = _out_proj(x2, [y_a.reshape(t, SWA_Q_W), y_b.reshape(t, GDN_W)],
                                       [w_out[:SWA_Q_W], w_out[SWA_Q_W:]], router)
        else:
            res = _norm_proj(x2, ln_mix_odd[i], w_in_odd[i].astype(BF16), perm=_even_odd_perm(),
                             n_perm=2 * RET_HEADS, moe_out=pending)
            if pending is not None:
                p, x2 = res
                pending = None
            else:
                p = res
            y = _retention(p.reshape(bsz, s, ODD_IN), bsz, s)
            x2, eid, gates = _out_proj(x2, [y.reshape(t, RET_V_W)], [w_out_odd[i].astype(BF16)], router)
        dest, y_rows = _moe_expert_rows(x2, eid, ln_ffn[layer], w_gate, w_up, w_down, layer)
        pending = (dest, gates, y_rows)
    x2 = settle(x2, pending)
    return x2.reshape(bsz, s, d).astype(x.dtype)
```

```python
import functools

import numpy as np
import jax
import jax.numpy as jnp
from jax import lax
from jax.experimental import pallas as pl
from jax.experimental.pallas import tpu as pltpu

F32 = jnp.float32
BF16 = jnp.bfloat16

D_MODEL = 1024
SWA_Q_HEADS = 8
SWA_KV_HEADS = 2
SWA_HEAD_DIM = 64
WINDOW = 128
ROT_DIM = SWA_HEAD_DIM // 4
ROPE_THETA = 500000.0
GDN_HEADS = 4
GDN_HEAD_DIM = 128
CONV_K = 4
GDN_CHUNK = 64
RET_HEADS = 4
RET_QK_DIM = 256
RET_V_DIM = 512
RET_CHUNK = 128
XPOS_THETA = 10000.0
N_GROUPS = 4
EXPERTS_PER_GROUP = 8
N_EXPERTS = N_GROUPS * EXPERTS_PER_GROUP
TOP_K = 2
D_EXPERT = 512
EPS = 1e-6

SWA_Q_W = SWA_Q_HEADS * SWA_HEAD_DIM
SWA_KV_W = SWA_KV_HEADS * SWA_HEAD_DIM
GDN_W = GDN_HEADS * GDN_HEAD_DIM
EVEN_MAIN = SWA_Q_W + 2 * SWA_KV_W + 4 * GDN_W
EVEN_COL_Q = 0
EVEN_COL_GDN = SWA_Q_W
EVEN_COL_Z = EVEN_COL_GDN + 3 * GDN_W
EVEN_COL_K = EVEN_COL_Z + GDN_W
EVEN_COL_V = EVEN_COL_K + SWA_KV_W
RET_QK_W = RET_HEADS * RET_QK_DIM
RET_V_W = RET_HEADS * RET_V_DIM
ODD_IN = 2 * RET_QK_W + 2 * RET_V_W

LANES = 128
VMEM_LIMIT = 56 * 1024 * 1024

NEG_INF = float("-inf")


def _cparams(*sem):
    return pltpu.CompilerParams(dimension_semantics=sem, vmem_limit_bytes=VMEM_LIMIT)


def _resident(shape):
    return pl.BlockSpec(shape, lambda i: (0,) * len(shape), pipeline_mode=pl.Buffered(1))


def _silu(x):
    return x * (1.0 / (1.0 + jnp.exp(-x)))


def _norm_proj_kernel(x_ref, g_ref, w_ref, *rest, tn, has_gate, n_perm, perm_w, conv_cols, seq_tiles):
    rest = list(rest)
    wg_ref = rest.pop(0) if has_gate else None
    perm_ref = rest.pop(0) if n_perm else None
    cw_ref = rest.pop(0) if conv_cols else None
    o_ref = rest.pop(0)
    og_ref = rest.pop(0) if has_gate else None
    wp_ref = rest.pop(0) if n_perm else None
    cbuf_ref = rest.pop(0) if conv_cols else None
    tm = x_ref.shape[0]
    pad = 8

    if conv_cols:
        @pl.when(pl.program_id(0) % seq_tiles == 0)
        def _():
            cbuf_ref[0:pad, :] = jnp.zeros((pad, cbuf_ref.shape[1]), F32)

    if n_perm:
        @pl.when(pl.program_id(0) == 0)
        def _():
            for j in range(n_perm):
                cols = slice(j * perm_w, (j + 1) * perm_w)
                wp_ref[:, cols] = jnp.dot(w_ref[:, cols], perm_ref[...], preferred_element_type=F32).astype(BF16)

    x = x_ref[...]
    h = x * lax.rsqrt(jnp.mean(x * x, axis=-1, keepdims=True) + EPS) * g_ref[...]
    hb = h.astype(BF16)
    n = o_ref.shape[-1]
    for lo in range(0, n, tn):
        hi = min(lo + tn, n)
        src = wp_ref if hi <= n_perm * perm_w else w_ref
        res = jnp.dot(hb, src[:, lo:hi], preferred_element_type=F32)
        if conv_cols and conv_cols[0] <= lo and hi <= conv_cols[1]:
            cc = slice(lo - conv_cols[0], hi - conv_cols[0])
            cbuf_ref[pad:pad + tm, cc] = res
            res = None
            for j in range(CONV_K):
                off = pad - (CONV_K - 1) + j
                term = cbuf_ref[off:off + tm, cc] * cw_ref[j:j + 1, cc]
                res = term if res is None else res + term
            res = _silu(res)
            cbuf_ref[0:pad, cc] = cbuf_ref[tm:tm + pad, cc]
        o_ref[:, lo:hi] = res.astype(o_ref.dtype)
    if has_gate:
        h_lo = (h - hb.astype(F32)).astype(BF16)
        r = jnp.dot(hb, wg_ref[...], preferred_element_type=F32)
        og_ref[...] = r[:, :LANES] + r[:, LANES:] + jnp.dot(h_lo, wg_ref[:, :LANES], preferred_element_type=F32)


def _norm_proj(x2, gain, w_bf16, w_gate_f32=None, perm=None, n_perm=0, conv_w=None, conv_cols=None, seq_len=None,
               *, tm=512, tn=512):
    t, d = x2.shape
    n = w_bf16.shape[1]
    has_gate = w_gate_f32 is not None
    perm_w = perm.shape[0] if n_perm else 0
    assert (n_perm * perm_w) % tn == 0
    seq_tiles = seq_len // tm if conv_cols else 0
    assert not conv_cols or (conv_cols[0] % tn == 0 and conv_cols[1] % tn == 0 and seq_len % tm == 0)
    in_specs = [pl.BlockSpec((tm, d), lambda i: (i, 0)),
                _resident((1, d)),
                _resident((d, n))]
    args = [x2, gain.reshape(1, d), w_bf16]
    out_shape = [jax.ShapeDtypeStruct((t, n), BF16)]
    out_specs = [pl.BlockSpec((tm, n), lambda i: (i, 0))]
    scratch = []
    if has_gate:
        wg_hi = w_gate_f32.astype(BF16)
        wg_lo = (w_gate_f32 - wg_hi.astype(F32)).astype(BF16)
        in_specs.append(_resident((d, 2 * LANES)))
        args.append(jnp.concatenate([wg_hi, wg_lo], axis=1))
        out_shape.append(jax.ShapeDtypeStruct((t, LANES), F32))
        out_specs.append(pl.BlockSpec((tm, LANES), lambda i: (i, 0)))
    if n_perm:
        in_specs.append(_resident((perm_w, perm_w)))
        args.append(perm)
        scratch.append(pltpu.VMEM((d, n_perm * perm_w), BF16))
    if conv_cols:
        cw = conv_cols[1] - conv_cols[0]
        in_specs.append(_resident((CONV_K, cw)))
        args.append(conv_w.astype(F32))
        scratch.append(pltpu.VMEM((tm + 8, cw), F32))
    res = pl.pallas_call(
        functools.partial(_norm_proj_kernel, tn=tn, has_gate=has_gate, n_perm=n_perm, perm_w=perm_w,
                          conv_cols=conv_cols, seq_tiles=seq_tiles),
        grid=(t // tm,), in_specs=in_specs, out_specs=out_specs, out_shape=out_shape, scratch_shapes=scratch,
        compiler_params=_cparams("arbitrary"), name="norm_proj")(*args)
    return res if len(res) > 1 else res[0]


def _out_proj_kernel(*refs, n_in, route):
    res_ref = refs[0]
    a_refs = refs[1:1 + n_in]
    w_refs = refs[1 + n_in:1 + 2 * n_in]
    rest = refs[1 + 2 * n_in:]
    acc = res_ref[...]
    for a_ref, w_ref in zip(a_refs, w_refs):
        acc = acc + jnp.dot(a_ref[...], w_ref[...], preferred_element_type=F32)
    if route:
        g_ref, wr_ref, wrlo_ref, br_ref, o_ref, eid_ref, gate_ref = rest
        _route_tile(acc, g_ref, wr_ref, wrlo_ref, br_ref, eid_ref, gate_ref)
    else:
        (o_ref,) = rest
    o_ref[...] = acc


def _out_proj(res, acts, ws, router=None, *, tm=1024):
    t, d = res.shape
    n_in = len(acts)
    in_specs = [pl.BlockSpec((tm, d), lambda i: (i, 0))]
    in_specs += [pl.BlockSpec((tm, a.shape[1]), lambda i: (i, 0)) for a in acts]
    in_specs += [_resident(w.shape) for w in ws]
    out_specs = [pl.BlockSpec((tm, d), lambda i: (i, 0))]
    out_shape = [jax.ShapeDtypeStruct((t, d), F32)]
    extra = ()
    if router is not None:
        extra = tuple(router)
        in_specs += [_resident(r.shape) for r in extra]
        out_specs += [pl.BlockSpec((8, tm), lambda i: (0, i))] * 2
        out_shape += [jax.ShapeDtypeStruct((8, t), jnp.int32), jax.ShapeDtypeStruct((8, t), F32)]
    out = pl.pallas_call(
        functools.partial(_out_proj_kernel, n_in=n_in, route=router is not None),
        grid=(t // tm,), in_specs=in_specs, out_specs=out_specs, out_shape=out_shape,
        compiler_params=_cparams("parallel"), name="out_proj")(res, *acts, *ws, *extra)
    return out if router is not None else out[0]


def _retention_kernel(q_ref, k_ref, v_ref, z_ref, cos_ref, sin_ref, cosk_ref, sink_ref, decay_ref, xi_ref, zeta_ref,
                      gam_ref, o_ref, state_ref, *, c, n_chunks):
    half = RET_QK_DIM // 2

    @pl.when(pl.program_id(2) == 0)
    def _():
        state_ref[...] = jnp.zeros_like(state_ref)

    def rotate(t, cos, sin):
        a = t[:, :half].astype(F32)
        b = t[:, half:].astype(F32)
        return jnp.concatenate([a * cos - b * sin, b * cos + a * sin], axis=-1)

    q = rotate(q_ref[0], cos_ref[...], sin_ref[...])
    k = rotate(k_ref[0], cosk_ref[...], sink_ref[...])
    decay = decay_ref[0]
    gamma_c = gam_ref[0, 0:1, 0:1]
    xi = xi_ref[0][:, 0:1]
    zeta = zeta_ref[0][:, 0:1]
    chunks = range(n_chunks)
    rows = [slice(ci * c, (ci + 1) * c) for ci in chunks]
    qb = [q[r].astype(BF16) for r in rows]
    kb = [k[r].astype(BF16) for r in rows]
    kz = [(k[r] * zeta).astype(BF16) for r in rows]
    vc = [v_ref[0, r, :] for r in rows]
    inner = [lax.dot_general(qb[i], kb[i], (((1,), (1,)), ((), ())), preferred_element_type=F32) * decay
             for i in chunks]
    kv = [lax.dot_general(kz[i], vc[i], (((0,), (0,)), ((), ())), preferred_element_type=F32) for i in chunks]
    intra = [jnp.dot(inner[i].astype(BF16), vc[i], preferred_element_type=F32) for i in chunks]
    states = [state_ref[...]]
    for i in chunks:
        states.append(states[i] * gamma_c + kv[i])
    state_ref[...] = states[-1]
    cross = [jnp.dot(qb[i], states[i].astype(BF16), preferred_element_type=F32) for i in chunks]
    for i in chunks:
        out = intra[i] + cross[i] * xi
        o = out * lax.rsqrt(jnp.mean(out * out, axis=-1, keepdims=True) + EPS)
        z = z_ref[0, rows[i], :].astype(F32)
        o_ref[0, rows[i], :] = (_silu(z) * o).astype(o_ref.dtype)


RET_KERNEL_CHUNK = 256


def _retention(p, bsz, s, *, blk=1024):
    h = RET_HEADS
    c = RET_KERNEL_CHUNK
    half = RET_QK_DIM // 2
    pos = jnp.arange(s, dtype=F32)
    freq = 1.0 / (XPOS_THETA ** jnp.linspace(0.0, 1.0, half, dtype=F32))
    ang = pos[:, None] * freq[None, :]
    cos, sin = jnp.cos(ang), jnp.sin(ang)
    kscale = RET_QK_DIM ** -0.5
    log_gamma = jnp.log(1.0 - 2.0 ** (-5.0 - jnp.arange(h, dtype=F32)))
    idx = jnp.arange(c, dtype=F32)
    rel = idx[:, None] - idx[None, :]
    decay = jnp.exp(jnp.where(rel >= 0, rel * log_gamma[:, None, None], NEG_INF))
    xi = jnp.broadcast_to(jnp.exp((idx + 1.0) * log_gamma[:, None])[..., None], (h, c, LANES))
    zeta = jnp.broadcast_to(jnp.exp((c - 1.0 - idx) * log_gamma[:, None])[..., None], (h, c, LANES))
    gam = jnp.broadcast_to(jnp.exp(c * log_gamma)[:, None, None], (h, 8, LANES))
    nq = RET_QK_W // RET_QK_DIM
    nv = RET_V_W // RET_V_DIM
    return pl.pallas_call(
        functools.partial(_retention_kernel, c=c, n_chunks=blk // c),
        grid=(bsz, h, s // blk),
        in_specs=[
            pl.BlockSpec((1, blk, RET_QK_DIM), lambda b, hh, i: (b, i, hh)),
            pl.BlockSpec((1, blk, RET_QK_DIM), lambda b, hh, i: (b, i, nq + hh)),
            pl.BlockSpec((1, blk, RET_V_DIM), lambda b, hh, i: (b, i, nv + hh)),
            pl.BlockSpec((1, blk, RET_V_DIM), lambda b, hh, i: (b, i, 2 * nv + hh)),
            pl.BlockSpec((blk, half), lambda b, hh, i: (i, 0)),
            pl.BlockSpec((blk, half), lambda b, hh, i: (i, 0)),
            pl.BlockSpec((blk, half), lambda b, hh, i: (i, 0)),
            pl.BlockSpec((blk, half), lambda b, hh, i: (i, 0)),
            pl.BlockSpec((1, c, c), lambda b, hh, i: (hh, 0, 0)),
            pl.BlockSpec((1, c, LANES), lambda b, hh, i: (hh, 0, 0)),
            pl.BlockSpec((1, c, LANES), lambda b, hh, i: (hh, 0, 0)),
            pl.BlockSpec((1, 8, LANES), lambda b, hh, i: (hh, 0, 0)),
        ],
        out_specs=pl.BlockSpec((1, blk, RET_V_DIM), lambda b, hh, i: (b, i, hh)),
        out_shape=jax.ShapeDtypeStruct((bsz, s, RET_V_W), BF16),
        scratch_shapes=[pltpu.VMEM((RET_QK_DIM, RET_V_DIM), F32)],
        compiler_params=_cparams("parallel", "parallel", "arbitrary"), name="retention",
    )(p, p, p, p, cos, sin, cos * kscale, sin * kscale, decay, xi, zeta, gam)


def _swa_kernel(sink_ref, q_ref, kc_ref, kp_ref, vc_ref, vp_ref, cos_ref, sin_ref, cosp_ref, sinp_ref,
                qg_ref, kg_ref, bd_ref, rp_ref, o_ref, *, nq):
    w = WINDOW
    dh = SWA_HEAD_DIM
    grp = SWA_Q_HEADS // SWA_KV_HEADS
    n = pl.program_id(1)

    def head_rms(x, gain, bd):
        sq = x * x
        hi = sq.astype(BF16)
        lo = (sq - hi.astype(F32)).astype(BF16)
        ssum = jnp.dot(hi, bd, preferred_element_type=F32) + jnp.dot(lo, bd, preferred_element_type=F32)
        return x * lax.rsqrt(ssum * (1.0 / dh) + EPS) * gain

    def rope(x, cos, sin, rp):
        partner = jnp.dot(x.astype(BF16), rp, preferred_element_type=F32)
        return x * cos + partner * sin

    cos, sin = cos_ref[...], sin_ref[...]
    cosq = jnp.concatenate([cos] * (SWA_Q_W // LANES), axis=1)
    sinq = jnp.concatenate([sin] * (SWA_Q_W // LANES), axis=1)
    q = rope(head_rms(q_ref[0].astype(F32), qg_ref[...], bd_ref[...]), cosq, sinq, rp_ref[...])
    q = (q * (dh ** -0.5)).astype(BF16)
    bdk = bd_ref[0:SWA_KV_W, 0:SWA_KV_W]
    rpk = rp_ref[0:SWA_KV_W, 0:SWA_KV_W]
    kcur = rope(head_rms(kc_ref[0].astype(F32), kg_ref[...], bdk), cos, sin, rpk)
    kprev = rope(head_rms(kp_ref[0].astype(F32), kg_ref[...], bdk), cosp_ref[...], sinp_ref[...], rpk)
    kall = jnp.concatenate([kprev, kcur], axis=0).astype(BF16)
    vall = jnp.concatenate([vp_ref[0], vc_ref[0]], axis=0)

    i = lax.broadcasted_iota(jnp.int32, (w, 2 * w), 0)
    j = lax.broadcasted_iota(jnp.int32, (w, 2 * w), 1)
    band = (j > i) & (j <= i + w)
    masks = [band & ((j >= w) | (n > 0))] + [band] * (nq - 1)
    units = [(qb, hq) for qb in range(nq) for hq in range(SWA_Q_HEADS)]
    kh = {(qb, hk): kall[qb * w:(qb + 2) * w, hk * dh:(hk + 1) * dh] for qb in range(nq) for hk in range(SWA_KV_HEADS)}
    vh = {(qb, hk): vall[qb * w:(qb + 2) * w, hk * dh:(hk + 1) * dh] for qb in range(nq) for hk in range(SWA_KV_HEADS)}
    sc = [lax.dot_general(q[qb * w:(qb + 1) * w, hq * dh:(hq + 1) * dh], kh[qb, hq // grp], (((1,), (1,)), ((), ())),
                          preferred_element_type=F32) for qb, hq in units]
    sc = [jnp.where(masks[qb], sc[u], NEG_INF) for u, (qb, hq) in enumerate(units)]
    m = [jnp.maximum(jnp.max(sc[u], axis=-1, keepdims=True), sink_ref[hq]) for u, (qb, hq) in enumerate(units)]
    p = [jnp.exp(sc[u] - m[u]) for u in range(len(units))]
    denom = [jnp.sum(p[u], axis=-1, keepdims=True) + jnp.exp(sink_ref[hq] - m[u]) for u, (qb, hq) in enumerate(units)]
    pv = [jnp.dot(p[u].astype(BF16), vh[qb, hq // grp], preferred_element_type=F32) for u, (qb, hq) in enumerate(units)]
    for qb in range(nq):
        row = [pv[qb * SWA_Q_HEADS + hq] * (1.0 / denom[qb * SWA_Q_HEADS + hq]) for hq in range(SWA_Q_HEADS)]
        o_ref[0, qb * w:(qb + 1) * w, :] = jnp.concatenate(row, axis=-1).astype(o_ref.dtype)


def _rope_tables(s):
    half = ROT_DIM // 2
    pos = jnp.arange(s, dtype=F32)
    inv_freq = 1.0 / (ROPE_THETA ** (jnp.arange(half, dtype=F32) * 2.0 / ROT_DIM))
    ang = pos[:, None] * inv_freq[None, :]
    cos, sin = jnp.cos(ang), jnp.sin(ang)
    ones = jnp.ones((s, SWA_HEAD_DIM - ROT_DIM), F32)
    cos_h = jnp.concatenate([cos, cos, ones], axis=1)
    sin_h = jnp.concatenate([-sin, sin, 0.0 * ones], axis=1)
    reps = LANES // SWA_HEAD_DIM
    return jnp.tile(cos_h, (1, reps)), jnp.tile(sin_h, (1, reps))


def _swa(p, sinks, q_gain, k_gain, bsz, s, *, nq=4):
    w = WINDOW
    blk = nq * w
    cos, sin = _rope_tables(s)
    qg = jnp.tile(q_gain.astype(F32), SWA_Q_HEADS).reshape(1, SWA_Q_W)
    kg = jnp.tile(k_gain.astype(F32), SWA_KV_HEADS).reshape(1, SWA_KV_W)
    lanes = np.arange(SWA_Q_W)
    seg = lanes // SWA_HEAD_DIM
    bd = jnp.asarray(seg[:, None] == seg[None, :], dtype=BF16)
    half = ROT_DIM // 2
    in_head = lanes % SWA_HEAD_DIM
    src = np.where(in_head < half, lanes + half, lanes - half)
    rp = np.zeros((SWA_Q_W, SWA_Q_W), np.float32)
    rot = in_head < ROT_DIM
    rp[src[rot], lanes[rot]] = 1.0
    rp = jnp.asarray(rp, dtype=BF16)
    qblk, kblk, vblk = EVEN_COL_Q // SWA_Q_W, EVEN_COL_K // SWA_KV_W, EVEN_COL_V // SWA_KV_W
    prev = lambda i: jnp.maximum(i * nq - 1, 0)
    return pl.pallas_call(
        functools.partial(_swa_kernel, nq=nq),
        grid=(bsz, s // blk),
        in_specs=[
            pl.BlockSpec(memory_space=pltpu.SMEM),
            pl.BlockSpec((1, blk, SWA_Q_W), lambda b, i: (b, i, qblk)),
            pl.BlockSpec((1, blk, SWA_KV_W), lambda b, i: (b, i, kblk)),
            pl.BlockSpec((1, w, SWA_KV_W), lambda b, i: (b, prev(i), kblk)),
            pl.BlockSpec((1, blk, SWA_KV_W), lambda b, i: (b, i, vblk)),
            pl.BlockSpec((1, w, SWA_KV_W), lambda b, i: (b, prev(i), vblk)),
            pl.BlockSpec((blk, LANES), lambda b, i: (i, 0)),
            pl.BlockSpec((blk, LANES), lambda b, i: (i, 0)),
            pl.BlockSpec((w, LANES), lambda b, i: (prev(i), 0)),
            pl.BlockSpec((w, LANES), lambda b, i: (prev(i), 0)),
            pl.BlockSpec((1, SWA_Q_W), lambda b, i: (0, 0)),
            pl.BlockSpec((1, SWA_KV_W), lambda b, i: (0, 0)),
            pl.BlockSpec((SWA_Q_W, SWA_Q_W), lambda b, i: (0, 0)),
            pl.BlockSpec((SWA_Q_W, SWA_Q_W), lambda b, i: (0, 0)),
        ],
        out_specs=pl.BlockSpec((1, blk, SWA_Q_W), lambda b, i: (b, i, 0)),
        out_shape=jax.ShapeDtypeStruct((bsz, s, SWA_Q_W), BF16),
        compiler_params=_cparams("parallel", "parallel"), name="swa",
    )(sinks.astype(F32), p, p, p, p, p, cos, sin, cos, sin, qg, kg, bd, rp)


def _softplus(x):
    return jnp.maximum(x, 0.0) + jnp.log1p(jnp.exp(-jnp.abs(x)))


def _split3(x):
    x1 = x.astype(BF16)
    r1 = x - x1.astype(F32)
    x2 = r1.astype(BF16)
    return x1, x2, (r1 - x2.astype(F32)).astype(BF16)


def _gdn_kernel(q_ref, k_ref, v_ref, z_ref, gate_ref, alog_ref, dt_ref, gain_ref, o_ref, state_ref, *, blk):
    c = GDN_CHUNK
    dk = GDN_HEAD_DIM
    nh = GDN_HEADS
    sub = LANES
    nt = (((1,), (1,)), ((), ()))

    @pl.when(pl.program_id(1) == 0)
    def _():
        state_ref[...] = jnp.zeros_like(state_ref)

    gates = gate_ref[0]
    g_all = -jnp.exp(alog_ref[...]) * _softplus(gates + dt_ref[...])
    beta_all = 1.0 / (1.0 + jnp.exp(-gates))
    ii = lax.broadcasted_iota(jnp.int32, (sub, sub), 0)
    jj = lax.broadcasted_iota(jnp.int32, (sub, sub), 1)
    same = (ii // c) == (jj // c)
    incl = same & (ii >= jj)
    strict = same & (ii > jj)
    tri = jnp.where(incl, 1.0, 0.0).astype(BF16)
    eye = jnp.where(ii == jj, 1.0, 0.0).astype(F32)
    er = lax.broadcasted_iota(jnp.int32, (8, LANES), 0)
    ec = lax.broadcasted_iota(jnp.int32, (8, LANES), 1)
    eye8 = jnp.where(er == ec, 1.0, 0.0).astype(BF16)
    g_parts = _split3(g_all)
    g_rows = sum(lax.dot_general(eye8, gp, nt, preferred_element_type=F32) for gp in g_parts)
    gr_parts = _split3(g_rows)
    gc_cols, gc_rows = [], []
    for sb in range(blk // sub):
        rows = slice(sb * sub, (sb + 1) * sub)
        gc_cols.append(sum(jnp.dot(tri, gp[rows], preferred_element_type=F32) for gp in g_parts))
        gc_rows.append(sum(lax.dot_general(gp[:, rows], tri, nt, preferred_element_type=F32) for gp in gr_parts))

    n_sub = blk // sub
    units = [(hd, sb) for hd in range(nh) for sb in range(n_sub)]
    dot = functools.partial(jnp.dot, preferred_element_type=F32)

    def dot_nt(x, y):
        return lax.dot_general(x, y, nt, preferred_element_type=F32)

    qs, ks, vs, betas = [], [], [], []
    for hd in range(nh):
        cols = slice(hd * dk, (hd + 1) * dk)
        q = q_ref[0, :, cols].astype(F32)
        k = k_ref[0, :, cols].astype(F32)
        qs.append(q * lax.rsqrt(jnp.sum(q * q, axis=-1, keepdims=True) + EPS) * (dk ** -0.5))
        ks.append(k * lax.rsqrt(jnp.sum(k * k, axis=-1, keepdims=True) + EPS))
        vs.append(v_ref[0, :, cols].astype(F32))
        betas.append(beta_all[:, nh + hd:nh + hd + 1])

    def rows_of(sb):
        return slice(sb * sub, (sb + 1) * sub)

    gcb = [jnp.broadcast_to(gc_cols[sb][:, hd:hd + 1], (sub, LANES)) for hd, sb in units]
    decay = [jnp.exp(jnp.where(incl, gcb[u] - gc_rows[sb][hd:hd + 1, :], NEG_INF)) for u, (hd, sb) in enumerate(units)]
    eg = [jnp.exp(g) for g in gcb]
    q_u = [qs[hd][rows_of(sb)] for hd, sb in units]
    k_u = [ks[hd][rows_of(sb)] for hd, sb in units]
    kb = [k_u[u] * betas[hd][rows_of(sb)] for u, (hd, sb) in enumerate(units)]
    kbf = [x.astype(BF16) for x in k_u]
    a = [dot_nt(kb[u].astype(BF16), kbf[u]) for u in range(len(units))]
    m = [jnp.where(strict, -(a[u] * decay[u]), 0.0) for u in range(len(units))]
    r = [eye + x for x in m]
    mb = [x.astype(BF16) for x in m]
    pw = [dot(x, x) for x in mb]
    for _ in range(int(np.log2(c)) - 2):
        pb = [x.astype(BF16) for x in pw]
        both = [dot(jnp.concatenate([r[u].astype(BF16), pb[u]], axis=0), pb[u]) for u in range(len(units))]
        r = [r[u] + both[u][:sub] for u in range(len(units))]
        pw = [x[sub:] for x in both]
    r = [r[u] + dot(r[u].astype(BF16), pw[u].astype(BF16)) for u in range(len(units))]
    rhs = [jnp.concatenate([kb[u] * eg[u], vs[hd][rows_of(sb)] * betas[hd][rows_of(sb)]], axis=1).astype(BF16)
           for u, (hd, sb) in enumerate(units)]
    wu = [dot(r[u].astype(BF16), rhs[u]).astype(BF16) for u in range(len(units))]
    qk = [dot_nt(q_u[u].astype(BF16), kbf[u]) for u in range(len(units))]
    iwu = [dot((qk[u] * decay[u]).astype(BF16), wu[u]) for u in range(len(units))]
    qe = [(q_u[u] * eg[u] - iwu[u][:, :dk]).astype(BF16) for u in range(len(units))]
    per_chunk = sub // c
    kwu, g_last = {}, {}
    for u, (hd, sb) in enumerate(units):
        for ci in range(per_chunk):
            crow = slice(ci * c, (ci + 1) * c)
            gl = gcb[u][(ci + 1) * c - 1:(ci + 1) * c, :]
            k_dec = (k_u[u][crow] * jnp.exp(gl - gcb[u][crow])).astype(BF16)
            g_last[hd, sb * per_chunk + ci] = jnp.exp(gl)
            kwu[hd, sb * per_chunk + ci] = lax.dot_general(k_dec, wu[u][crow], (((0,), (0,)), ((), ())),
                                                           preferred_element_type=F32)
    outs = [[] for _ in range(nh)]
    for ch in range(blk // c):
        u_sb, ci = ch // per_chunk, ch % per_chunk
        crow = slice(ci * c, (ci + 1) * c)
        states = [state_ref[hd] for hd in range(nh)]
        sbf = [x.astype(BF16) for x in states]
        for hd in range(nh):
            u = hd * n_sub + u_sb
            outs[hd].append(dot(qe[u][crow], sbf[hd]) + iwu[u][crow, dk:])
        corr = [dot(kwu[hd, ch][:, :dk].astype(BF16), sbf[hd]) for hd in range(nh)]
        for hd in range(nh):
            state_ref[hd] = states[hd] * g_last[hd, ch] + (kwu[hd, ch][:, dk:] - corr[hd])
    for hd in range(nh):
        out = jnp.concatenate(outs[hd], axis=0)
        o = out * lax.rsqrt(jnp.mean(out * out, axis=-1, keepdims=True) + EPS) * gain_ref[...]
        z = z_ref[0, :, hd * dk:(hd + 1) * dk].astype(F32)
        o_ref[0, :, hd * dk:(hd + 1) * dk] = (o * _silu(z)).astype(o_ref.dtype)


def _gdn(p, pg, a_log, dt_bias, gdn_gain, bsz, s, *, blk=512):
    dk = GDN_HEAD_DIM
    assert EVEN_COL_GDN % GDN_W == 0 and EVEN_COL_Z % GDN_W == 0
    lane_pad = (0, LANES - GDN_HEADS)
    alog = jnp.pad(a_log.astype(F32), lane_pad).reshape(1, LANES)
    dt = jnp.pad(dt_bias.astype(F32), lane_pad).reshape(1, LANES)
    return pl.pallas_call(
        functools.partial(_gdn_kernel, blk=blk),
        grid=(bsz, s // blk),
        in_specs=[pl.BlockSpec((1, blk, GDN_W), lambda b, i: (b, i, EVEN_COL_GDN // GDN_W)),
                  pl.BlockSpec((1, blk, GDN_W), lambda b, i: (b, i, EVEN_COL_GDN // GDN_W + 1)),
                  pl.BlockSpec((1, blk, GDN_W), lambda b, i: (b, i, EVEN_COL_GDN // GDN_W + 2)),
                  pl.BlockSpec((1, blk, GDN_W), lambda b, i: (b, i, EVEN_COL_Z // GDN_W)),
                  pl.BlockSpec((1, blk, LANES), lambda b, i: (b, i, 0)),
                  pl.BlockSpec((1, LANES), lambda b, i: (0, 0)),
                  pl.BlockSpec((1, LANES), lambda b, i: (0, 0)),
                  pl.BlockSpec((1, dk), lambda b, i: (0, 0))],
        out_specs=pl.BlockSpec((1, blk, GDN_W), lambda b, i: (b, i, 0)),
        out_shape=jax.ShapeDtypeStruct((bsz, s, GDN_W), BF16),
        scratch_shapes=[pltpu.VMEM((GDN_HEADS, dk, dk), F32)],
        compiler_params=_cparams("parallel", "arbitrary"), name="gdn",
    )(p, p, p, p, pg, alog, dt, gdn_gain.astype(F32).reshape(1, dk))


MOE_BM = 512
ROUTER_ROWS = LANES
EXPERT_ROW0 = 8


SLAB = 8
DMA_UNROLL = 4


def _store_slabs(ref, val, rows):
    for sl in range(SLAB):
        ref[pl.ds(sl, rows, stride=SLAB), :] = val[:, sl * LANES:(sl + 1) * LANES]


def _load_slabs(ref, rows):
    return jnp.concatenate([ref[pl.ds(sl, rows, stride=SLAB), :] for sl in range(SLAB)], axis=1)


def _slab(ref, row):
    return ref.at[pl.ds(pl.multiple_of(row * SLAB, SLAB), SLAB), :]


def _ffn_norm(x, gain):
    return x * lax.rsqrt(jnp.mean(x * x, axis=-1, keepdims=True) + EPS) * gain


def _route_tile(x, g_ref, wr_ref, wrlo_ref, br_ref, eid_ref, gate_ref):
    h = _ffn_norm(x, g_ref[...])
    tm = h.shape[0]
    h_hi = h.astype(BF16)
    h_lo = (h - h_hi.astype(F32)).astype(BF16)
    nt = (((1,), (1,)), ((), ()))
    logits = (lax.dot_general(wr_ref[...], h_hi, nt, preferred_element_type=F32)
              + lax.dot_general(wr_ref[...], h_lo, nt, preferred_element_type=F32)
              + lax.dot_general(wrlo_ref[...], h_hi, nt, preferred_element_type=F32)) + br_ref[...]
    row = lax.broadcasted_iota(jnp.int32, (8, tm), 0)
    gl = jnp.where(row < N_GROUPS, logits[0:8, :], NEG_INF)
    gmax = jnp.max(gl, axis=0, keepdims=True)
    grp_p = 1.0 / jnp.sum(jnp.exp(gl - gmax), axis=0, keepdims=True)
    grp_idx = jnp.min(jnp.where(gl == gmax, row, 8), axis=0, keepdims=True)
    el = jnp.zeros((8, tm), F32)
    for g in range(N_GROUPS):
        lo = EXPERT_ROW0 + g * EXPERTS_PER_GROUP
        el = jnp.where(grp_idx == g, logits[lo:lo + EXPERTS_PER_GROUP, :], el)
    m1 = jnp.max(el, axis=0, keepdims=True)
    i1 = jnp.min(jnp.where(el == m1, row, 8), axis=0, keepdims=True)
    el2 = jnp.where(row == i1, NEG_INF, el)
    m2 = jnp.max(el2, axis=0, keepdims=True)
    i2 = jnp.min(jnp.where(el2 == m2, row, 8), axis=0, keepdims=True)
    z = jnp.sum(jnp.exp(el - m1), axis=0, keepdims=True)
    p1 = 1.0 / z
    p2 = jnp.exp(m2 - m1) / z
    scale = grp_p / (p1 + p2)
    e1 = grp_idx * EXPERTS_PER_GROUP + i1
    e2 = grp_idx * EXPERTS_PER_GROUP + i2
    eid_ref[...] = jnp.where(row == 0, e1, jnp.where(row == 1, e2, 0))
    gate_ref[...] = jnp.where(row == 0, p1 * scale, jnp.where(row == 1, p2 * scale, 0.0))


def _router_operands(gain, w_rg, b_rg, w_re, b_re):
    d = w_rg.shape[0]
    wr = jnp.zeros((ROUTER_ROWS, d), F32)
    wr = wr.at[0:N_GROUPS].set(w_rg.astype(F32).T).at[EXPERT_ROW0:EXPERT_ROW0 + N_EXPERTS].set(w_re.astype(F32).T)
    br = jnp.zeros((ROUTER_ROWS, 1), F32)
    br = br.at[0:N_GROUPS, 0].set(b_rg.astype(F32)).at[EXPERT_ROW0:EXPERT_ROW0 + N_EXPERTS, 0].set(b_re.astype(F32))
    wr_hi = wr.astype(BF16)
    wr_lo = (wr - wr_hi.astype(F32)).astype(BF16)
    return gain.astype(F32).reshape(1, d), wr_hi, wr_lo, br


def _rank_kernel(eid_ref, dest_ref, meta_ref, cnt_ref, off_ref, *, bm):
    ph = pl.program_id(0)
    i = pl.program_id(1)
    tm = eid_ref.shape[1]
    erow = lax.broadcasted_iota(jnp.int32, (N_EXPERTS, tm), 0)
    oh0 = jnp.where(erow == eid_ref[0:1, :], 1.0, 0.0).astype(F32)
    oh1 = jnp.where(erow == eid_ref[1:2, :], 1.0, 0.0).astype(F32)
    oh = oh0 + oh1
    tile_cnt = jnp.sum(oh, axis=1, keepdims=True)

    @pl.when((ph == 0) & (i == 0))
    def _():
        cnt_ref[...] = jnp.zeros_like(cnt_ref)

    @pl.when(ph == 0)
    def _():
        cnt_ref[...] += tile_cnt

    @pl.when((ph == 1) & (i == 0))
    def _():
        cnt = cnt_ref[...]
        padded = jnp.ceil(cnt * (1.0 / bm)) * bm
        er = lax.broadcasted_iota(jnp.int32, (N_EXPERTS, N_EXPERTS), 0)
        ec = lax.broadcasted_iota(jnp.int32, (N_EXPERTS, N_EXPERTS), 1)
        lower = jnp.where(ec < er, 1.0, 0.0).astype(F32)
        start = jnp.dot(lower, padded, preferred_element_type=F32, precision=lax.Precision.HIGHEST)
        off_ref[...] = start
        end = start + padded
        nb = meta_ref.shape[1]
        blk_row = lax.broadcasted_iota(jnp.int32, (N_EXPERTS, nb), 1).astype(F32) * bm
        endb = jnp.concatenate([end] * (nb // LANES), axis=1)
        be = jnp.sum(jnp.where(endb <= blk_row, 1.0, 0.0), axis=0, keepdims=True)
        be = jnp.minimum(be, N_EXPERTS - 1.0)
        used = jnp.concatenate([end[N_EXPERTS - 1:N_EXPERTS, :]] * (nb // LANES), axis=1) * (1.0 / bm)
        elane = lax.broadcasted_iota(jnp.int32, (N_EXPERTS, nb), 1)
        erow2 = lax.broadcasted_iota(jnp.int32, (N_EXPERTS, nb), 0)
        ends = jnp.sum(jnp.where(elane == erow2, endb, 0.0), axis=0, keepdims=True)
        mrow = lax.broadcasted_iota(jnp.int32, (8, nb), 0)
        meta_ref[...] = jnp.where(mrow == 0, be, jnp.where(mrow == 1, used, jnp.where(mrow == 2, ends, 0.0))
                                  ).astype(jnp.int32)
        cnt_ref[...] = jnp.zeros_like(cnt_ref)

    @pl.when(ph == 1)
    def _():
        kr = lax.broadcasted_iota(jnp.int32, (tm, tm), 0)
        kc = lax.broadcasted_iota(jnp.int32, (tm, tm), 1)
        su = jnp.where(kr < kc, 1.0, 0.0).astype(BF16)
        csum = jnp.dot(oh.astype(BF16), su, preferred_element_type=F32)
        pos = csum + (off_ref[...] + cnt_ref[...])[:, 0:1]
        d0 = jnp.sum(oh0 * pos, axis=0, keepdims=True)
        d1 = jnp.sum(oh1 * pos, axis=0, keepdims=True)
        row = lax.broadcasted_iota(jnp.int32, (8, tm), 0)
        dest_ref[...] = jnp.where(row == 0, d0, jnp.where(row == 1, d1, 0.0)).astype(jnp.int32)
        cnt_ref[...] += tile_cnt


def _moe_rank(eid, n_blocks, *, bm, tm=1024):
    t = eid.shape[1]
    nb = -(-n_blocks // LANES) * LANES
    return pl.pallas_call(
        functools.partial(_rank_kernel, bm=bm), grid=(2, t // tm),
        in_specs=[pl.BlockSpec((8, tm), lambda p, i: (0, i))],
        out_specs=[pl.BlockSpec((8, tm), lambda p, i: (0, i * p)),
                   pl.BlockSpec((8, nb), lambda p, i: (0, 0))],
        out_shape=[jax.ShapeDtypeStruct((8, t), jnp.int32), jax.ShapeDtypeStruct((8, nb), jnp.int32)],
        scratch_shapes=[pltpu.VMEM((N_EXPERTS, LANES), F32), pltpu.VMEM((N_EXPERTS, LANES), F32)],
        compiler_params=_cparams("arbitrary", "arbitrary"), name="moe_rank")(eid)


def _dispatch_kernel(dest_ref, ends_ref, x_ref, g_ref, rows_hbm, zero_ref, hbuf_ref, sem, rsem, *, tm, bm):
    @pl.when(pl.program_id(0) == 0)
    def _():
        zero_ref[...] = jnp.zeros_like(zero_ref)

        def tail(e):
            start = ends_ref[0, e] - bm
            return pltpu.make_async_copy(zero_ref, rows_hbm.at[pl.ds(pl.multiple_of(start * SLAB, SLAB), bm * SLAB), :],
                                         sem)

        def nonempty(e):
            return ends_ref[0, e] > jnp.where(e > 0, ends_ref[0, jnp.maximum(e - 1, 0)], 0)

        def fill(e, carry):
            @pl.when(nonempty(e))
            def _():
                tail(e).start()
            return carry

        def drain(e, carry):
            @pl.when(nonempty(e))
            def _():
                tail(e).wait()
            return carry

        lax.fori_loop(0, N_EXPERTS, fill, 0)
        lax.fori_loop(0, N_EXPERTS, drain, 0)

        def unused(b):
            return pltpu.make_async_copy(
                zero_ref, rows_hbm.at[pl.ds(pl.multiple_of(b * (bm * SLAB), SLAB), bm * SLAB), :], sem)

        def fill_unused(b, carry):
            unused(b).start()
            return carry

        def drain_unused(b, carry):
            unused(b).wait()
            return carry

        first_unused = ends_ref[0, N_EXPERTS - 1] // bm
        n_blocks = rows_hbm.shape[0] // (bm * SLAB)
        lax.fori_loop(first_unused, n_blocks, fill_unused, 0)
        lax.fori_loop(first_unused, n_blocks, drain_unused, 0)

    i = pl.program_id(0)
    slot = lax.rem(i, 2)

    def drain_slot(s):
        for _ in range(TOP_K):
            pltpu.make_async_copy(hbuf_ref.at[s], rows_hbm.at[pl.ds(0, tm * SLAB), :], rsem.at[s]).wait()

    for s in range(2):
        @pl.when((slot == s) & (i >= 2))
        def _():
            drain_slot(s)

        @pl.when(slot == s)
        def _():
            _store_slabs(hbuf_ref.at[s], _ffn_norm(x_ref[...], g_ref[...]), tm)

            def issue(g, carry):
                for u in range(DMA_UNROLL):
                    tok = g * DMA_UNROLL + u
                    for k in range(TOP_K):
                        pltpu.make_async_copy(_slab(hbuf_ref.at[s], tok),
                                              _slab(rows_hbm, dest_ref[0, 0, k * tm + tok]),
                                              rsem.at[s]).start(priority=k)
                return carry

            lax.fori_loop(0, tm // DMA_UNROLL, issue, 0)

    last = pl.num_programs(0) - 1

    @pl.when(i == last)
    def _():
        for s in range(2):
            @pl.when((slot == s) | (last >= 1))
            def _():
                drain_slot(s)


def _moe_dispatch(dest_tiles, expert_ends, x2, gain, n_rows, *, tm, bm):
    t, d = x2.shape
    return pl.pallas_call(
        functools.partial(_dispatch_kernel, tm=tm, bm=bm), grid=(t // tm,),
        in_specs=[pl.BlockSpec((1, 1, TOP_K * tm), lambda i: (i, 0, 0), memory_space=pltpu.SMEM),
                  pl.BlockSpec(memory_space=pltpu.SMEM),
                  pl.BlockSpec((tm, d), lambda i: (i, 0)),
                  pl.BlockSpec((1, d), lambda i: (0, 0))],
        out_specs=pl.BlockSpec(memory_space=pl.ANY),
        out_shape=jax.ShapeDtypeStruct((n_rows * SLAB, LANES), F32),
        scratch_shapes=[pltpu.VMEM((bm * SLAB, LANES), F32), pltpu.VMEM((2, tm * SLAB, LANES), F32),
                        pltpu.SemaphoreType.DMA(()), pltpu.SemaphoreType.DMA((2,))],
        compiler_params=pltpu.CompilerParams(dimension_semantics=("arbitrary",), has_side_effects=True,
                                             vmem_limit_bytes=VMEM_LIMIT),
        name="moe_dispatch")(dest_tiles, expert_ends, x2, gain.reshape(1, d))


COMBINE_TM = 512
EXPERT_RING = 3


def _experts_kernel(be_ref, used_ref, x_hbm, wg_ref, wu_ref, wd_ref, y_ref, wgb_ref, wub_ref, wdb_ref, xbuf_ref,
                    xsem):
    b = pl.program_id(0)
    used = used_ref[0]
    bm = y_ref.shape[0] // SLAB

    def fetch(blk):
        slot = lax.rem(blk, EXPERT_RING)
        rows = pl.ds(pl.multiple_of(blk * (bm * SLAB), SLAB), bm * SLAB)
        return pltpu.make_async_copy(x_hbm.at[rows, :], xbuf_ref.at[slot], xsem.at[slot])

    @pl.when(b == 0)
    def _():
        for first in range(EXPERT_RING - 1):
            @pl.when(first < used)
            def _():
                fetch(first).start()

    @pl.when(b + (EXPERT_RING - 1) < used)
    def _():
        fetch(b + (EXPERT_RING - 1)).start()

    prev = be_ref[jnp.maximum(b - 1, 0)]

    @pl.when((b == 0) | (be_ref[b] != prev))
    def _():
        wgb_ref[...] = wg_ref[0, 0].astype(BF16)
        wub_ref[...] = wu_ref[0, 0].astype(BF16)
        wdb_ref[...] = wd_ref[0, 0].astype(BF16)

    @pl.when(b < used)
    def _():
        fetch(b).wait()
        x = _load_slabs(xbuf_ref.at[lax.rem(b, EXPERT_RING)], bm).astype(BF16)
        gate = jnp.dot(x, wgb_ref[...], preferred_element_type=F32)
        up = jnp.dot(x, wub_ref[...], preferred_element_type=F32)
        hid = (_silu(gate) * up).astype(BF16)
        _store_slabs(y_ref, jnp.dot(hid, wdb_ref[...], preferred_element_type=F32), bm)

    @pl.when(b >= used_ref[0])
    def _():
        y_ref[...] = jnp.zeros_like(y_ref)


def _moe_experts(block_expert, n_used, x_rows, w_gate, w_up, w_down, layer, *, bm):
    n_rows = x_rows.shape[0] // SLAB
    d, de = w_gate.shape[2], w_gate.shape[3]
    grid_spec = pltpu.PrefetchScalarGridSpec(
        num_scalar_prefetch=2, grid=(n_rows // bm,),
        in_specs=[pl.BlockSpec(memory_space=pl.ANY),
                  pl.BlockSpec((1, 1, d, de), lambda b, be, nu: (layer, be[b], 0, 0)),
                  pl.BlockSpec((1, 1, d, de), lambda b, be, nu: (layer, be[b], 0, 0)),
                  pl.BlockSpec((1, 1, de, d), lambda b, be, nu: (layer, be[b], 0, 0))],
        out_specs=pl.BlockSpec((bm * SLAB, LANES), lambda b, be, nu: (b, 0)),
        scratch_shapes=[pltpu.VMEM((d, de), BF16), pltpu.VMEM((d, de), BF16), pltpu.VMEM((de, d), BF16),
                        pltpu.VMEM((EXPERT_RING, bm * SLAB, LANES), F32), pltpu.SemaphoreType.DMA((EXPERT_RING,))])
    return pl.pallas_call(
        _experts_kernel, grid_spec=grid_spec, out_shape=jax.ShapeDtypeStruct((n_rows * SLAB, LANES), F32),
        compiler_params=_cparams("arbitrary"), name="moe_experts",
    )(block_expert, n_used, x_rows, w_gate, w_up, w_down)


def _combine_kernel(dest_ref, next_ref, x_ref, gate_ref, y_hbm, o_ref, buf_ref, sem, *, tm):
    _combine_tile(dest_ref, next_ref, x_ref, gate_ref, y_hbm, o_ref, buf_ref, sem, tm)


def _combine_tile(dest_ref, next_ref, x_ref, gate_ref, y_hbm, o_ref, buf_ref, sem, tm):
    i = pl.program_id(0)
    slot = lax.rem(i, 2)

    def gather(idx_ref, s):
        def issue(g, carry):
            for u in range(DMA_UNROLL):
                tok = g * DMA_UNROLL + u
                for k in range(TOP_K):
                    pltpu.make_async_copy(_slab(y_hbm, idx_ref[0, 0, k * tm + tok]), _slab(buf_ref.at[s, k], tok),
                                          sem.at[s]).start(priority=k)
            return carry

        lax.fori_loop(0, tm // DMA_UNROLL, issue, 0)

    for s in range(2):
        @pl.when((i == 0) & (slot == s))
        def _():
            gather(dest_ref, s)

        @pl.when((i + 1 < pl.num_programs(0)) & (slot == 1 - s))
        def _():
            gather(next_ref, s)

    er = lax.broadcasted_iota(jnp.int32, (8, LANES), 0)
    ec = lax.broadcasted_iota(jnp.int32, (8, LANES), 1)
    eye = jnp.where(er == ec, 1.0, 0.0).astype(F32)
    gcol = lax.dot_general(gate_ref[...], eye, (((0,), (0,)), ((), ())), preferred_element_type=F32,
                           precision=lax.Precision.HIGHEST)
    for s in range(2):
        @pl.when(slot == s)
        def _():
            for k in range(TOP_K):
                pltpu.make_async_copy(y_hbm.at[pl.ds(0, tm * SLAB), :], buf_ref.at[s, k], sem.at[s]).wait()
            o_ref[...] = (x_ref[...] + gcol[:, 0:1] * _load_slabs(buf_ref.at[s, 0], tm)
                          + gcol[:, 1:2] * _load_slabs(buf_ref.at[s, 1], tm))


def _moe_combine(dest_tiles, x2, gates, y_rows, *, tm):
    t, d = x2.shape
    n = t // tm
    dest_spec = functools.partial(pl.BlockSpec, (1, 1, TOP_K * tm), memory_space=pltpu.SMEM)
    return pl.pallas_call(
        functools.partial(_combine_kernel, tm=tm), grid=(n,),
        in_specs=[dest_spec(lambda i: (i, 0, 0)),
                  dest_spec(lambda i: (jnp.minimum(i + 1, n - 1), 0, 0)),
                  pl.BlockSpec((tm, d), lambda i: (i, 0)),
                  pl.BlockSpec((8, tm), lambda i: (0, i)),
                  pl.BlockSpec(memory_space=pl.ANY)],
        out_specs=pl.BlockSpec((tm, d), lambda i: (i, 0)),
        out_shape=jax.ShapeDtypeStruct((t, d), F32),
        scratch_shapes=[pltpu.VMEM((2, TOP_K, tm * SLAB, LANES), F32), pltpu.SemaphoreType.DMA((2,))],
        compiler_params=_cparams("arbitrary"), name="moe_combine")(dest_tiles, dest_tiles, x2, gates, y_rows)


def _dest_tiles(dest, tm):
    t = dest.shape[1]
    return dest[0:TOP_K].reshape(TOP_K, t // tm, tm).transpose(1, 0, 2).reshape(t // tm, 1, TOP_K * tm)


def _moe_expert_rows(x2, eid, gain, w_gate, w_up, w_down, layer, *, tm=1024):
    t, d = x2.shape
    bm = MOE_BM
    n_rows = t * TOP_K + N_EXPERTS * bm
    n_blocks = n_rows // bm
    dest, meta = _moe_rank(eid, n_blocks, bm=bm)
    x_rows = _moe_dispatch(_dest_tiles(dest, tm), meta[2:3, :N_EXPERTS], x2, gain, n_rows, tm=tm, bm=bm)
    y_rows = _moe_experts(meta[0, :n_blocks], meta[1, 0:1], x_rows, w_gate, w_up, w_down, layer, bm=bm)
    return dest, y_rows


def _even_odd_perm():
    half = RET_QK_DIM // 2
    src = np.concatenate([2 * np.arange(half), 2 * np.arange(half) + 1])
    perm = np.zeros((RET_QK_DIM, RET_QK_DIM), np.float32)
    perm[src, np.arange(RET_QK_DIM)] = 1.0
    return jnp.asarray(perm, dtype=BF16)


def kernel(x, ln_mix_even, w_in_even, conv_w_even, a_log_even, dt_bias_even, q_gain_even, k_gain_even, sinks_even, gdn_gain_even, w_out_even, ln_mix_odd, w_in_odd, w_out_odd, ln_ffn, w_router_group, b_router_group, w_router_expert, b_router_expert, w_gate, w_up, w_down):
    bsz, s, d = x.shape
    t = bsz * s
    x2 = x.reshape(t, d).astype(F32)
    depth = ln_ffn.shape[0]
    pending = None

    def settle(x2, pending):
        dest, gates, y_rows = pending
        return _moe_combine(_dest_tiles(dest, COMBINE_TM), x2, gates, y_rows, tm=COMBINE_TM)

    for layer in range(depth):
        i = layer // 2
        router = _router_operands(ln_ffn[layer], w_router_group[layer], b_router_group[layer],
                                  w_router_expert[layer], b_router_expert[layer])
        if pending is not None:
            x2, pending = settle(x2, pending), None
        if layer % 2 == 0:
            w_in = w_in_even[i]
            swa_end = SWA_Q_W + 2 * SWA_KV_W
            w_main = jnp.concatenate([w_in[:, :SWA_Q_W], w_in[:, swa_end:EVEN_MAIN], w_in[:, SWA_Q_W:swa_end]],
                                     axis=1).astype(BF16)
            w_gates = jnp.pad(w_in[:, EVEN_MAIN:].astype(F32), ((0, 0), (0, LANES - 2 * GDN_HEADS)))
            p, pg = _norm_proj(x2, ln_mix_even[i], w_main, w_gates, conv_w=conv_w_even[i],
                               conv_cols=(EVEN_COL_GDN, EVEN_COL_Z), seq_len=s, tm=1024)
            p = p.reshape(bsz, s, EVEN_MAIN)
            y_a = _swa(p, sinks_even[i], q_gain_even[i], k_gain_even[i], bsz, s)
            y_b = _gdn(p, pg.reshape(bsz, s, LANES), a_log_even[i], dt_bias_even[i], gdn_gain_even[i], bsz, s)
            w_out = w_out_even[i].astype(BF16)
            x2, eid, gates = _out_proj(x2, [y_a.reshape(t, SWA_Q_W), y_b.reshape(t, GDN_W)],
                                       [w_out[:SWA_Q_W], w_out[SWA_Q_W:]], router)
        else:
            p = _norm_proj(x2, ln_mix_odd[i], w_in_odd[i].astype(BF16), perm=_even_odd_perm(), n_perm=2 * RET_HEADS)
            y = _retention(p.reshape(bsz, s, ODD_IN), bsz, s)
            x2, eid, gates = _out_proj(x2, [y.reshape(t, RET_V_W)], [w_out_odd[i].astype(BF16)], router)
        dest, y_rows = _moe_expert_rows(x2, eid, ln_ffn[layer], w_gate, w_up, w_down, layer)
        pending = (dest, gates, y_rows)
    x2 = settle(x2, pending)
    return x2.reshape(bsz, s, d).astype(x.dtype)
```

```python
import functools

import numpy as np
import jax
import jax.numpy as jnp
from jax import lax
from jax.experimental import pallas as pl
from jax.experimental.pallas import tpu as pltpu

F32 = jnp.float32
BF16 = jnp.bfloat16

D_MODEL = 1024
SWA_Q_HEADS = 8
SWA_KV_HEADS = 2
SWA_HEAD_DIM = 64
WINDOW = 128
ROT_DIM = SWA_HEAD_DIM // 4
ROPE_THETA = 500000.0
GDN_HEADS = 4
GDN_HEAD_DIM = 128
CONV_K = 4
GDN_CHUNK = 64
RET_HEADS = 4
RET_QK_DIM = 256
RET_V_DIM = 512
RET_CHUNK = 128
XPOS_THETA = 10000.0
N_GROUPS = 4
EXPERTS_PER_GROUP = 8
N_EXPERTS = N_GROUPS * EXPERTS_PER_GROUP
TOP_K = 2
D_EXPERT = 512
EPS = 1e-6

SWA_Q_W = SWA_Q_HEADS * SWA_HEAD_DIM
SWA_KV_W = SWA_KV_HEADS * SWA_HEAD_DIM
GDN_W = GDN_HEADS * GDN_HEAD_DIM
EVEN_MAIN = SWA_Q_W + 2 * SWA_KV_W + 4 * GDN_W
EVEN_COL_Q = 0
EVEN_COL_GDN = SWA_Q_W
EVEN_COL_Z = EVEN_COL_GDN + 3 * GDN_W
EVEN_COL_K = EVEN_COL_Z + GDN_W
EVEN_COL_V = EVEN_COL_K + SWA_KV_W
RET_QK_W = RET_HEADS * RET_QK_DIM
RET_V_W = RET_HEADS * RET_V_DIM
ODD_IN = 2 * RET_QK_W + 2 * RET_V_W

LANES = 128
VMEM_LIMIT = 56 * 1024 * 1024

NEG_INF = float("-inf")


def _cparams(*sem):
    return pltpu.CompilerParams(dimension_semantics=sem, vmem_limit_bytes=VMEM_LIMIT)


def _resident(shape):
    return pl.BlockSpec(shape, lambda i: (0,) * len(shape), pipeline_mode=pl.Buffered(1))


def _silu(x):
    return x * (1.0 / (1.0 + jnp.exp(-x)))


def _norm_proj_kernel(x_ref, g_ref, w_ref, *rest, tn, has_gate, n_perm, perm_w, conv_cols, seq_tiles):
    rest = list(rest)
    wg_ref = rest.pop(0) if has_gate else None
    perm_ref = rest.pop(0) if n_perm else None
    cw_ref = rest.pop(0) if conv_cols else None
    o_ref = rest.pop(0)
    og_ref = rest.pop(0) if has_gate else None
    wp_ref = rest.pop(0) if n_perm else None
    cbuf_ref = rest.pop(0) if conv_cols else None
    tm = x_ref.shape[0]
    pad = 8

    if conv_cols:
        @pl.when(pl.program_id(0) % seq_tiles == 0)
        def _():
            cbuf_ref[0:pad, :] = jnp.zeros((pad, cbuf_ref.shape[1]), F32)

    if n_perm:
        @pl.when(pl.program_id(0) == 0)
        def _():
            for j in range(n_perm):
                cols = slice(j * perm_w, (j + 1) * perm_w)
                wp_ref[:, cols] = jnp.dot(w_ref[:, cols], perm_ref[...], preferred_element_type=F32).astype(BF16)

    x = x_ref[...]
    h = x * lax.rsqrt(jnp.mean(x * x, axis=-1, keepdims=True) + EPS) * g_ref[...]
    hb = h.astype(BF16)
    n = o_ref.shape[-1]
    for lo in range(0, n, tn):
        hi = min(lo + tn, n)
        src = wp_ref if hi <= n_perm * perm_w else w_ref
        res = jnp.dot(hb, src[:, lo:hi], preferred_element_type=F32)
        if conv_cols and conv_cols[0] <= lo and hi <= conv_cols[1]:
            cc = slice(lo - conv_cols[0], hi - conv_cols[0])
            cbuf_ref[pad:pad + tm, cc] = res
            res = None
            for j in range(CONV_K):
                off = pad - (CONV_K - 1) + j
                term = cbuf_ref[off:off + tm, cc] * cw_ref[j:j + 1, cc]
                res = term if res is None else res + term
            res = _silu(res)
            cbuf_ref[0:pad, cc] = cbuf_ref[tm:tm + pad, cc]
        o_ref[:, lo:hi] = res.astype(o_ref.dtype)
    if has_gate:
        h_lo = (h - hb.astype(F32)).astype(BF16)
        r = jnp.dot(hb, wg_ref[...], preferred_element_type=F32)
        og_ref[...] = r[:, :LANES] + r[:, LANES:] + jnp.dot(h_lo, wg_ref[:, :LANES], preferred_element_type=F32)


def _norm_proj(x2, gain, w_bf16, w_gate_f32=None, perm=None, n_perm=0, conv_w=None, conv_cols=None, seq_len=None,
               *, tm=512, tn=512):
    t, d = x2.shape
    n = w_bf16.shape[1]
    has_gate = w_gate_f32 is not None
    perm_w = perm.shape[0] if n_perm else 0
    assert (n_perm * perm_w) % tn == 0
    seq_tiles = seq_len // tm if conv_cols else 0
    assert not conv_cols or (conv_cols[0] % tn == 0 and conv_cols[1] % tn == 0 and seq_len % tm == 0)
    in_specs = [pl.BlockSpec((tm, d), lambda i: (i, 0)),
                _resident((1, d)),
                _resident((d, n))]
    args = [x2, gain.reshape(1, d), w_bf16]
    out_shape = [jax.ShapeDtypeStruct((t, n), BF16)]
    out_specs = [pl.BlockSpec((tm, n), lambda i: (i, 0))]
    scratch = []
    if has_gate:
        wg_hi = w_gate_f32.astype(BF16)
        wg_lo = (w_gate_f32 - wg_hi.astype(F32)).astype(BF16)
        in_specs.append(_resident((d, 2 * LANES)))
        args.append(jnp.concatenate([wg_hi, wg_lo], axis=1))
        out_shape.append(jax.ShapeDtypeStruct((t, LANES), F32))
        out_specs.append(pl.BlockSpec((tm, LANES), lambda i: (i, 0)))
    if n_perm:
        in_specs.append(_resident((perm_w, perm_w)))
        args.append(perm)
        scratch.append(pltpu.VMEM((d, n_perm * perm_w), BF16))
    if conv_cols:
        cw = conv_cols[1] - conv_cols[0]
        in_specs.append(_resident((CONV_K, cw)))
        args.append(conv_w.astype(F32))
        scratch.append(pltpu.VMEM((tm + 8, cw), F32))
    res = pl.pallas_call(
        functools.partial(_norm_proj_kernel, tn=tn, has_gate=has_gate, n_perm=n_perm, perm_w=perm_w,
                          conv_cols=conv_cols, seq_tiles=seq_tiles),
        grid=(t // tm,), in_specs=in_specs, out_specs=out_specs, out_shape=out_shape, scratch_shapes=scratch,
        compiler_params=_cparams("arbitrary"), name="norm_proj")(*args)
    return res if len(res) > 1 else res[0]


def _out_proj_kernel(*refs, n_in, route):
    res_ref = refs[0]
    a_refs = refs[1:1 + n_in]
    w_refs = refs[1 + n_in:1 + 2 * n_in]
    rest = refs[1 + 2 * n_in:]
    acc = res_ref[...]
    for a_ref, w_ref in zip(a_refs, w_refs):
        acc = acc + jnp.dot(a_ref[...], w_ref[...], preferred_element_type=F32)
    if route:
        g_ref, wr_ref, wrlo_ref, br_ref, o_ref, eid_ref, gate_ref = rest
        _route_tile(acc, g_ref, wr_ref, wrlo_ref, br_ref, eid_ref, gate_ref)
    else:
        (o_ref,) = rest
    o_ref[...] = acc


def _out_proj(res, acts, ws, router=None, *, tm=1024):
    t, d = res.shape
    n_in = len(acts)
    in_specs = [pl.BlockSpec((tm, d), lambda i: (i, 0))]
    in_specs += [pl.BlockSpec((tm, a.shape[1]), lambda i: (i, 0)) for a in acts]
    in_specs += [_resident(w.shape) for w in ws]
    out_specs = [pl.BlockSpec((tm, d), lambda i: (i, 0))]
    out_shape = [jax.ShapeDtypeStruct((t, d), F32)]
    extra = ()
    if router is not None:
        extra = tuple(router)
        in_specs += [_resident(r.shape) for r in extra]
        out_specs += [pl.BlockSpec((8, tm), lambda i: (0, i))] * 2
        out_shape += [jax.ShapeDtypeStruct((8, t), jnp.int32), jax.ShapeDtypeStruct((8, t), F32)]
    out = pl.pallas_call(
        functools.partial(_out_proj_kernel, n_in=n_in, route=router is not None),
        grid=(t // tm,), in_specs=in_specs, out_specs=out_specs, out_shape=out_shape,
        compiler_params=_cparams("parallel"), name="out_proj")(res, *acts, *ws, *extra)
    return out if router is not None else out[0]


def _retention_kernel(q_ref, k_ref, v_ref, z_ref, cos_ref, sin_ref, decay_ref, xi_ref, zeta_ref, gam_ref,
                      o_ref, state_ref, *, c, n_chunks):
    half = RET_QK_DIM // 2

    @pl.when(pl.program_id(2) == 0)
    def _():
        state_ref[...] = jnp.zeros_like(state_ref)

    cos = cos_ref[...]
    sin = sin_ref[...]

    def rotate(t):
        a = t[:, :half].astype(F32)
        b = t[:, half:].astype(F32)
        return jnp.concatenate([a * cos - b * sin, b * cos + a * sin], axis=-1)

    q = rotate(q_ref[0])
    k = rotate(k_ref[0])
    decay = decay_ref[0]
    gamma_c = gam_ref[0, 0:1, 0:1]
    xi = xi_ref[0][:, 0:1]
    zeta = zeta_ref[0][:, 0:1]
    chunks = range(n_chunks)
    rows = [slice(ci * c, (ci + 1) * c) for ci in chunks]
    qb = [q[r].astype(BF16) for r in rows]
    kb = [k[r].astype(BF16) for r in rows]
    kz = [(k[r] * zeta).astype(BF16) for r in rows]
    vc = [v_ref[0, r, :] for r in rows]
    inner = [lax.dot_general(qb[i], kb[i], (((1,), (1,)), ((), ())), preferred_element_type=F32) * decay
             for i in chunks]
    kv = [lax.dot_general(kz[i], vc[i], (((0,), (0,)), ((), ())), preferred_element_type=F32) for i in chunks]
    intra = [jnp.dot(inner[i].astype(BF16), vc[i], preferred_element_type=F32) for i in chunks]
    states = [state_ref[...]]
    for i in chunks:
        states.append(states[i] * gamma_c + kv[i])
    state_ref[...] = states[-1]
    cross = [jnp.dot(qb[i], states[i].astype(BF16), preferred_element_type=F32) for i in chunks]
    for i in chunks:
        out = intra[i] + cross[i] * xi
        o = out * lax.rsqrt(jnp.mean(out * out, axis=-1, keepdims=True) + EPS)
        z = z_ref[0, rows[i], :].astype(F32)
        o_ref[0, rows[i], :] = (_silu(z) * o).astype(o_ref.dtype)


RET_KERNEL_CHUNK = 256


def _retention(p, bsz, s, *, blk=1024):
    h = RET_HEADS
    c = RET_KERNEL_CHUNK
    half = RET_QK_DIM // 2
    pos = jnp.arange(s, dtype=F32)
    freq = 1.0 / (XPOS_THETA ** jnp.linspace(0.0, 1.0, half, dtype=F32))
    ang = pos[:, None] * freq[None, :]
    cos, sin = jnp.cos(ang), jnp.sin(ang)
    kscale = RET_QK_DIM ** -0.5
    log_gamma = jnp.log(1.0 - 2.0 ** (-5.0 - jnp.arange(h, dtype=F32)))
    idx = jnp.arange(c, dtype=F32)
    rel = idx[:, None] - idx[None, :]
    decay = jnp.exp(jnp.where(rel >= 0, rel * log_gamma[:, None, None], NEG_INF))
    xi = jnp.broadcast_to(jnp.exp((idx + 1.0) * log_gamma[:, None])[..., None], (h, c, LANES))
    zeta = jnp.broadcast_to(jnp.exp((c - 1.0 - idx) * log_gamma[:, None])[..., None], (h, c, LANES))
    gam = jnp.broadcast_to(jnp.exp(c * log_gamma)[:, None, None], (h, 8, LANES))
    nq = RET_QK_W // RET_QK_DIM
    nv = RET_V_W // RET_V_DIM
    return pl.pallas_call(
        functools.partial(_retention_kernel, c=c, n_chunks=blk // c),
        grid=(bsz, h, s // blk),
        in_specs=[
            pl.BlockSpec((1, blk, RET_QK_DIM), lambda b, hh, i: (b, i, hh)),
            pl.BlockSpec((1, blk, RET_QK_DIM), lambda b, hh, i: (b, i, nq + hh)),
            pl.BlockSpec((1, blk, RET_V_DIM), lambda b, hh, i: (b, i, nv + hh)),
            pl.BlockSpec((1, blk, RET_V_DIM), lambda b, hh, i: (b, i, 2 * nv + hh)),
            pl.BlockSpec((blk, half), lambda b, hh, i: (i, 0)),
            pl.BlockSpec((blk, half), lambda b, hh, i: (i, 0)),
            pl.BlockSpec((1, c, c), lambda b, hh, i: (hh, 0, 0)),
            pl.BlockSpec((1, c, LANES), lambda b, hh, i: (hh, 0, 0)),
            pl.BlockSpec((1, c, LANES), lambda b, hh, i: (hh, 0, 0)),
            pl.BlockSpec((1, 8, LANES), lambda b, hh, i: (hh, 0, 0)),
        ],
        out_specs=pl.BlockSpec((1, blk, RET_V_DIM), lambda b, hh, i: (b, i, hh)),
        out_shape=jax.ShapeDtypeStruct((bsz, s, RET_V_W), BF16),
        scratch_shapes=[pltpu.VMEM((RET_QK_DIM, RET_V_DIM), F32)],
        compiler_params=_cparams("parallel", "parallel", "arbitrary"), name="retention",
    )(p, p, p, p, cos, sin, decay * kscale, xi, zeta * kscale, gam)


def _swa_kernel(sink_ref, q_ref, kc_ref, kp_ref, vc_ref, vp_ref, cos_ref, sin_ref, cosp_ref, sinp_ref,
                qg_ref, kg_ref, bd_ref, rp_ref, o_ref, *, nq):
    w = WINDOW
    dh = SWA_HEAD_DIM
    grp = SWA_Q_HEADS // SWA_KV_HEADS
    n = pl.program_id(1)

    def head_rms(x, gain, bd):
        sq = x * x
        hi = sq.astype(BF16)
        lo = (sq - hi.astype(F32)).astype(BF16)
        ssum = jnp.dot(hi, bd, preferred_element_type=F32) + jnp.dot(lo, bd, preferred_element_type=F32)
        return x * lax.rsqrt(ssum * (1.0 / dh) + EPS) * gain

    def rope(x, cos, sin, rp):
        partner = jnp.dot(x.astype(BF16), rp, preferred_element_type=F32)
        return x * cos + partner * sin

    cos, sin = cos_ref[...], sin_ref[...]
    cosq = jnp.concatenate([cos] * (SWA_Q_W // LANES), axis=1)
    sinq = jnp.concatenate([sin] * (SWA_Q_W // LANES), axis=1)
    q = rope(head_rms(q_ref[0].astype(F32), qg_ref[...], bd_ref[...]), cosq, sinq, rp_ref[...])
    q = (q * (dh ** -0.5)).astype(BF16)
    bdk = bd_ref[0:SWA_KV_W, 0:SWA_KV_W]
    rpk = rp_ref[0:SWA_KV_W, 0:SWA_KV_W]
    kcur = rope(head_rms(kc_ref[0].astype(F32), kg_ref[...], bdk), cos, sin, rpk)
    kprev = rope(head_rms(kp_ref[0].astype(F32), kg_ref[...], bdk), cosp_ref[...], sinp_ref[...], rpk)
    kall = jnp.concatenate([kprev, kcur], axis=0).astype(BF16)
    vall = jnp.concatenate([vp_ref[0], vc_ref[0]], axis=0)

    i = lax.broadcasted_iota(jnp.int32, (w, 2 * w), 0)
    j = lax.broadcasted_iota(jnp.int32, (w, 2 * w), 1)
    band = (j > i) & (j <= i + w)
    masks = [band & ((j >= w) | (n > 0))] + [band] * (nq - 1)
    units = [(qb, hq) for qb in range(nq) for hq in range(SWA_Q_HEADS)]
    kh = {(qb, hk): kall[qb * w:(qb + 2) * w, hk * dh:(hk + 1) * dh] for qb in range(nq) for hk in range(SWA_KV_HEADS)}
    vh = {(qb, hk): vall[qb * w:(qb + 2) * w, hk * dh:(hk + 1) * dh] for qb in range(nq) for hk in range(SWA_KV_HEADS)}
    sc = [lax.dot_general(q[qb * w:(qb + 1) * w, hq * dh:(hq + 1) * dh], kh[qb, hq // grp], (((1,), (1,)), ((), ())),
                          preferred_element_type=F32) for qb, hq in units]
    sc = [jnp.where(masks[qb], sc[u], NEG_INF) for u, (qb, hq) in enumerate(units)]
    m = [jnp.maximum(jnp.max(sc[u], axis=-1, keepdims=True), sink_ref[hq]) for u, (qb, hq) in enumerate(units)]
    p = [jnp.exp(sc[u] - m[u]) for u in range(len(units))]
    denom = [jnp.sum(p[u], axis=-1, keepdims=True) + jnp.exp(sink_ref[hq] - m[u]) for u, (qb, hq) in enumerate(units)]
    pv = [jnp.dot(p[u].astype(BF16), vh[qb, hq // grp], preferred_element_type=F32) for u, (qb, hq) in enumerate(units)]
    for qb in range(nq):
        row = [pv[qb * SWA_Q_HEADS + hq] * (1.0 / denom[qb * SWA_Q_HEADS + hq]) for hq in range(SWA_Q_HEADS)]
        o_ref[0, qb * w:(qb + 1) * w, :] = jnp.concatenate(row, axis=-1).astype(o_ref.dtype)


def _rope_tables(s):
    half = ROT_DIM // 2
    pos = jnp.arange(s, dtype=F32)
    inv_freq = 1.0 / (ROPE_THETA ** (jnp.arange(half, dtype=F32) * 2.0 / ROT_DIM))
    ang = pos[:, None] * inv_freq[None, :]
    cos, sin = jnp.cos(ang), jnp.sin(ang)
    ones = jnp.ones((s, SWA_HEAD_DIM - ROT_DIM), F32)
    cos_h = jnp.concatenate([cos, cos, ones], axis=1)
    sin_h = jnp.concatenate([-sin, sin, 0.0 * ones], axis=1)
    reps = LANES // SWA_HEAD_DIM
    return jnp.tile(cos_h, (1, reps)), jnp.tile(sin_h, (1, reps))


def _swa(p, sinks, q_gain, k_gain, bsz, s, *, nq=4):
    w = WINDOW
    blk = nq * w
    cos, sin = _rope_tables(s)
    qg = jnp.tile(q_gain.astype(F32), SWA_Q_HEADS).reshape(1, SWA_Q_W)
    kg = jnp.tile(k_gain.astype(F32), SWA_KV_HEADS).reshape(1, SWA_KV_W)
    lanes = np.arange(SWA_Q_W)
    seg = lanes // SWA_HEAD_DIM
    bd = jnp.asarray(seg[:, None] == seg[None, :], dtype=BF16)
    half = ROT_DIM // 2
    in_head = lanes % SWA_HEAD_DIM
    src = np.where(in_head < half, lanes + half, lanes - half)
    rp = np.zeros((SWA_Q_W, SWA_Q_W), np.float32)
    rot = in_head < ROT_DIM
    rp[src[rot], lanes[rot]] = 1.0
    rp = jnp.asarray(rp, dtype=BF16)
    qblk, kblk, vblk = EVEN_COL_Q // SWA_Q_W, EVEN_COL_K // SWA_KV_W, EVEN_COL_V // SWA_KV_W
    prev = lambda i: jnp.maximum(i * nq - 1, 0)
    return pl.pallas_call(
        functools.partial(_swa_kernel, nq=nq),
        grid=(bsz, s // blk),
        in_specs=[
            pl.BlockSpec(memory_space=pltpu.SMEM),
            pl.BlockSpec((1, blk, SWA_Q_W), lambda b, i: (b, i, qblk)),
            pl.BlockSpec((1, blk, SWA_KV_W), lambda b, i: (b, i, kblk)),
            pl.BlockSpec((1, w, SWA_KV_W), lambda b, i: (b, prev(i), kblk)),
            pl.BlockSpec((1, blk, SWA_KV_W), lambda b, i: (b, i, vblk)),
            pl.BlockSpec((1, w, SWA_KV_W), lambda b, i: (b, prev(i), vblk)),
            pl.BlockSpec((blk, LANES), lambda b, i: (i, 0)),
            pl.BlockSpec((blk, LANES), lambda b, i: (i, 0)),
            pl.BlockSpec((w, LANES), lambda b, i: (prev(i), 0)),
            pl.BlockSpec((w, LANES), lambda b, i: (prev(i), 0)),
            pl.BlockSpec((1, SWA_Q_W), lambda b, i: (0, 0)),
            pl.BlockSpec((1, SWA_KV_W), lambda b, i: (0, 0)),
            pl.BlockSpec((SWA_Q_W, SWA_Q_W), lambda b, i: (0, 0)),
            pl.BlockSpec((SWA_Q_W, SWA_Q_W), lambda b, i: (0, 0)),
        ],
        out_specs=pl.BlockSpec((1, blk, SWA_Q_W), lambda b, i: (b, i, 0)),
        out_shape=jax.ShapeDtypeStruct((bsz, s, SWA_Q_W), BF16),
        compiler_params=_cparams("parallel", "parallel"), name="swa",
    )(sinks.astype(F32), p, p, p, p, p, cos, sin, cos, sin, qg, kg, bd, rp)


def _softplus(x):
    return jnp.maximum(x, 0.0) + jnp.log1p(jnp.exp(-jnp.abs(x)))


def _split3(x):
    x1 = x.astype(BF16)
    r1 = x - x1.astype(F32)
    x2 = r1.astype(BF16)
    return x1, x2, (r1 - x2.astype(F32)).astype(BF16)


def _gdn_kernel(q_ref, k_ref, v_ref, z_ref, gate_ref, alog_ref, dt_ref, gain_ref, o_ref, state_ref, *, blk):
    c = GDN_CHUNK
    dk = GDN_HEAD_DIM
    nh = GDN_HEADS
    sub = LANES
    nt = (((1,), (1,)), ((), ()))

    @pl.when(pl.program_id(1) == 0)
    def _():
        state_ref[...] = jnp.zeros_like(state_ref)

    gates = gate_ref[0]
    g_all = -jnp.exp(alog_ref[...]) * _softplus(gates + dt_ref[...])
    beta_all = 1.0 / (1.0 + jnp.exp(-gates))
    ii = lax.broadcasted_iota(jnp.int32, (sub, sub), 0)
    jj = lax.broadcasted_iota(jnp.int32, (sub, sub), 1)
    same = (ii // c) == (jj // c)
    incl = same & (ii >= jj)
    strict = same & (ii > jj)
    tri = jnp.where(incl, 1.0, 0.0).astype(BF16)
    eye = jnp.where(ii == jj, 1.0, 0.0).astype(F32)
    er = lax.broadcasted_iota(jnp.int32, (8, LANES), 0)
    ec = lax.broadcasted_iota(jnp.int32, (8, LANES), 1)
    eye8 = jnp.where(er == ec, 1.0, 0.0).astype(BF16)
    g_parts = _split3(g_all)
    g_rows = sum(lax.dot_general(eye8, gp, nt, preferred_element_type=F32) for gp in g_parts)
    gr_parts = _split3(g_rows)
    gc_cols, gc_rows = [], []
    for sb in range(blk // sub):
        rows = slice(sb * sub, (sb + 1) * sub)
        gc_cols.append(sum(jnp.dot(tri, gp[rows], preferred_element_type=F32) for gp in g_parts))
        gc_rows.append(sum(lax.dot_general(gp[:, rows], tri, nt, preferred_element_type=F32) for gp in gr_parts))

    n_sub = blk // sub
    units = [(hd, sb) for hd in range(nh) for sb in range(n_sub)]
    dot = functools.partial(jnp.dot, preferred_element_type=F32)

    def dot_nt(x, y):
        return lax.dot_general(x, y, nt, preferred_element_type=F32)

    qs, ks, vs, betas = [], [], [], []
    for hd in range(nh):
        cols = slice(hd * dk, (hd + 1) * dk)
        q = q_ref[0, :, cols].astype(F32)
        k = k_ref[0, :, cols].astype(F32)
        qs.append(q * lax.rsqrt(jnp.sum(q * q, axis=-1, keepdims=True) + EPS) * (dk ** -0.5))
        ks.append(k * lax.rsqrt(jnp.sum(k * k, axis=-1, keepdims=True) + EPS))
        vs.append(v_ref[0, :, cols].astype(F32))
        betas.append(beta_all[:, nh + hd:nh + hd + 1])

    def rows_of(sb):
        return slice(sb * sub, (sb + 1) * sub)

    gcb = [jnp.broadcast_to(gc_cols[sb][:, hd:hd + 1], (sub, LANES)) for hd, sb in units]
    decay = [jnp.exp(jnp.where(incl, gcb[u] - gc_rows[sb][hd:hd + 1, :], NEG_INF)) for u, (hd, sb) in enumerate(units)]
    eg = [jnp.exp(g) for g in gcb]
    q_u = [qs[hd][rows_of(sb)] for hd, sb in units]
    k_u = [ks[hd][rows_of(sb)] for hd, sb in units]
    kb = [k_u[u] * betas[hd][rows_of(sb)] for u, (hd, sb) in enumerate(units)]
    kbf = [x.astype(BF16) for x in k_u]
    a = [dot_nt(kb[u].astype(BF16), kbf[u]) for u in range(len(units))]
    m = [jnp.where(strict, -(a[u] * decay[u]), 0.0) for u in range(len(units))]
    r = [eye + x for x in m]
    mb = [x.astype(BF16) for x in m]
    pw = [dot(x, x) for x in mb]
    for _ in range(int(np.log2(c)) - 2):
        pb = [x.astype(BF16) for x in pw]
        both = [dot(jnp.concatenate([r[u].astype(BF16), pb[u]], axis=0), pb[u]) for u in range(len(units))]
        r = [r[u] + both[u][:sub] for u in range(len(units))]
        pw = [x[sub:] for x in both]
    r = [r[u] + dot(r[u].astype(BF16), pw[u].astype(BF16)) for u in range(len(units))]
    rhs = [jnp.concatenate([kb[u] * eg[u], vs[hd][rows_of(sb)] * betas[hd][rows_of(sb)]], axis=1).astype(BF16)
           for u, (hd, sb) in enumerate(units)]
    wu = [dot(r[u].astype(BF16), rhs[u]).astype(BF16) for u in range(len(units))]
    qk = [dot_nt(q_u[u].astype(BF16), kbf[u]) for u in range(len(units))]
    iwu = [dot((qk[u] * decay[u]).astype(BF16), wu[u]) for u in range(len(units))]
    qe = [(q_u[u] * eg[u] - iwu[u][:, :dk]).astype(BF16) for u in range(len(units))]
    per_chunk = sub // c
    kwu, g_last = {}, {}
    for u, (hd, sb) in enumerate(units):
        for ci in range(per_chunk):
            crow = slice(ci * c, (ci + 1) * c)
            gl = gcb[u][(ci + 1) * c - 1:(ci + 1) * c, :]
            k_dec = (k_u[u][crow] * jnp.exp(gl - gcb[u][crow])).astype(BF16)
            g_last[hd, sb * per_chunk + ci] = jnp.exp(gl)
            kwu[hd, sb * per_chunk + ci] = lax.dot_general(k_dec, wu[u][crow], (((0,), (0,)), ((), ())),
                                                           preferred_element_type=F32)
    outs = [[] for _ in range(nh)]
    for ch in range(blk // c):
        u_sb, ci = ch // per_chunk, ch % per_chunk
        crow = slice(ci * c, (ci + 1) * c)
        states = [state_ref[hd] for hd in range(nh)]
        sbf = [x.astype(BF16) for x in states]
        for hd in range(nh):
            u = hd * n_sub + u_sb
            outs[hd].append(dot(qe[u][crow], sbf[hd]) + iwu[u][crow, dk:])
        corr = [dot(kwu[hd, ch][:, :dk].astype(BF16), sbf[hd]) for hd in range(nh)]
        for hd in range(nh):
            state_ref[hd] = states[hd] * g_last[hd, ch] + (kwu[hd, ch][:, dk:] - corr[hd])
    for hd in range(nh):
        out = jnp.concatenate(outs[hd], axis=0)
        o = out * lax.rsqrt(jnp.mean(out * out, axis=-1, keepdims=True) + EPS) * gain_ref[...]
        z = z_ref[0, :, hd * dk:(hd + 1) * dk].astype(F32)
        o_ref[0, :, hd * dk:(hd + 1) * dk] = (o * _silu(z)).astype(o_ref.dtype)


def _gdn(p, pg, a_log, dt_bias, gdn_gain, bsz, s, *, blk=1024):
    dk = GDN_HEAD_DIM
    assert EVEN_COL_GDN % GDN_W == 0 and EVEN_COL_Z % GDN_W == 0
    lane_pad = (0, LANES - GDN_HEADS)
    alog = jnp.pad(a_log.astype(F32), lane_pad).reshape(1, LANES)
    dt = jnp.pad(dt_bias.astype(F32), lane_pad).reshape(1, LANES)
    return pl.pallas_call(
        functools.partial(_gdn_kernel, blk=blk),
        grid=(bsz, s // blk),
        in_specs=[pl.BlockSpec((1, blk, GDN_W), lambda b, i: (b, i, EVEN_COL_GDN // GDN_W)),
                  pl.BlockSpec((1, blk, GDN_W), lambda b, i: (b, i, EVEN_COL_GDN // GDN_W + 1)),
                  pl.BlockSpec((1, blk, GDN_W), lambda b, i: (b, i, EVEN_COL_GDN // GDN_W + 2)),
                  pl.BlockSpec((1, blk, GDN_W), lambda b, i: (b, i, EVEN_COL_Z // GDN_W)),
                  pl.BlockSpec((1, blk, LANES), lambda b, i: (b, i, 0)),
                  pl.BlockSpec((1, LANES), lambda b, i: (0, 0)),
                  pl.BlockSpec((1, LANES), lambda b, i: (0, 0)),
                  pl.BlockSpec((1, dk), lambda b, i: (0, 0))],
        out_specs=pl.BlockSpec((1, blk, GDN_W), lambda b, i: (b, i, 0)),
        out_shape=jax.ShapeDtypeStruct((bsz, s, GDN_W), BF16),
        scratch_shapes=[pltpu.VMEM((GDN_HEADS, dk, dk), F32)],
        compiler_params=_cparams("parallel", "arbitrary"), name="gdn",
    )(p, p, p, p, pg, alog, dt, gdn_gain.astype(F32).reshape(1, dk))


MOE_BM = 512
ROUTER_ROWS = LANES
EXPERT_ROW0 = 8


SLAB = 8
DMA_UNROLL = 4


def _store_slabs(ref, val, rows):
    for sl in range(SLAB):
        ref[pl.ds(sl, rows, stride=SLAB), :] = val[:, sl * LANES:(sl + 1) * LANES]


def _load_slabs(ref, rows):
    return jnp.concatenate([ref[pl.ds(sl, rows, stride=SLAB), :] for sl in range(SLAB)], axis=1)


def _slab(ref, row):
    return ref.at[pl.ds(pl.multiple_of(row * SLAB, SLAB), SLAB), :]


def _ffn_norm(x, gain):
    return x * lax.rsqrt(jnp.mean(x * x, axis=-1, keepdims=True) + EPS) * gain


def _route_tile(x, g_ref, wr_ref, wrlo_ref, br_ref, eid_ref, gate_ref):
    h = _ffn_norm(x, g_ref[...])
    tm = h.shape[0]
    h_hi = h.astype(BF16)
    h_lo = (h - h_hi.astype(F32)).astype(BF16)
    nt = (((1,), (1,)), ((), ()))
    logits = (lax.dot_general(wr_ref[...], h_hi, nt, preferred_element_type=F32)
              + lax.dot_general(wr_ref[...], h_lo, nt, preferred_element_type=F32)
              + lax.dot_general(wrlo_ref[...], h_hi, nt, preferred_element_type=F32)) + br_ref[...]
    row = lax.broadcasted_iota(jnp.int32, (8, tm), 0)
    gl = jnp.where(row < N_GROUPS, logits[0:8, :], NEG_INF)
    gmax = jnp.max(gl, axis=0, keepdims=True)
    grp_p = 1.0 / jnp.sum(jnp.exp(gl - gmax), axis=0, keepdims=True)
    grp_idx = jnp.min(jnp.where(gl == gmax, row, 8), axis=0, keepdims=True)
    el = jnp.zeros((8, tm), F32)
    for g in range(N_GROUPS):
        lo = EXPERT_ROW0 + g * EXPERTS_PER_GROUP
        el = jnp.where(grp_idx == g, logits[lo:lo + EXPERTS_PER_GROUP, :], el)
    m1 = jnp.max(el, axis=0, keepdims=True)
    i1 = jnp.min(jnp.where(el == m1, row, 8), axis=0, keepdims=True)
    el2 = jnp.where(row == i1, NEG_INF, el)
    m2 = jnp.max(el2, axis=0, keepdims=True)
    i2 = jnp.min(jnp.where(el2 == m2, row, 8), axis=0, keepdims=True)
    z = jnp.sum(jnp.exp(el - m1), axis=0, keepdims=True)
    p1 = 1.0 / z
    p2 = jnp.exp(m2 - m1) / z
    scale = grp_p / (p1 + p2)
    e1 = grp_idx * EXPERTS_PER_GROUP + i1
    e2 = grp_idx * EXPERTS_PER_GROUP + i2
    eid_ref[...] = jnp.where(row == 0, e1, jnp.where(row == 1, e2, 0))
    gate_ref[...] = jnp.where(row == 0, p1 * scale, jnp.where(row == 1, p2 * scale, 0.0))


def _router_operands(gain, w_rg, b_rg, w_re, b_re):
    d = w_rg.shape[0]
    wr = jnp.zeros((ROUTER_ROWS, d), F32)
    wr = wr.at[0:N_GROUPS].set(w_rg.astype(F32).T).at[EXPERT_ROW0:EXPERT_ROW0 + N_EXPERTS].set(w_re.astype(F32).T)
    br = jnp.zeros((ROUTER_ROWS, 1), F32)
    br = br.at[0:N_GROUPS, 0].set(b_rg.astype(F32)).at[EXPERT_ROW0:EXPERT_ROW0 + N_EXPERTS, 0].set(b_re.astype(F32))
    wr_hi = wr.astype(BF16)
    wr_lo = (wr - wr_hi.astype(F32)).astype(BF16)
    return gain.astype(F32).reshape(1, d), wr_hi, wr_lo, br


def _rank_kernel(eid_ref, dest_ref, meta_ref, cnt_ref, off_ref, *, bm):
    ph = pl.program_id(0)
    i = pl.program_id(1)
    tm = eid_ref.shape[1]
    erow = lax.broadcasted_iota(jnp.int32, (N_EXPERTS, tm), 0)
    oh0 = jnp.where(erow == eid_ref[0:1, :], 1.0, 0.0).astype(F32)
    oh1 = jnp.where(erow == eid_ref[1:2, :], 1.0, 0.0).astype(F32)
    oh = oh0 + oh1
    tile_cnt = jnp.sum(oh, axis=1, keepdims=True)

    @pl.when((ph == 0) & (i == 0))
    def _():
        cnt_ref[...] = jnp.zeros_like(cnt_ref)

    @pl.when(ph == 0)
    def _():
        cnt_ref[...] += tile_cnt

    @pl.when((ph == 1) & (i == 0))
    def _():
        cnt = cnt_ref[...]
        padded = jnp.ceil(cnt * (1.0 / bm)) * bm
        er = lax.broadcasted_iota(jnp.int32, (N_EXPERTS, N_EXPERTS), 0)
        ec = lax.broadcasted_iota(jnp.int32, (N_EXPERTS, N_EXPERTS), 1)
        lower = jnp.where(ec < er, 1.0, 0.0).astype(F32)
        start = jnp.dot(lower, padded, preferred_element_type=F32, precision=lax.Precision.HIGHEST)
        off_ref[...] = start
        end = start + padded
        nb = meta_ref.shape[1]
        blk_row = lax.broadcasted_iota(jnp.int32, (N_EXPERTS, nb), 1).astype(F32) * bm
        endb = jnp.concatenate([end] * (nb // LANES), axis=1)
        be = jnp.sum(jnp.where(endb <= blk_row, 1.0, 0.0), axis=0, keepdims=True)
        be = jnp.minimum(be, N_EXPERTS - 1.0)
        used = jnp.concatenate([end[N_EXPERTS - 1:N_EXPERTS, :]] * (nb // LANES), axis=1) * (1.0 / bm)
        elane = lax.broadcasted_iota(jnp.int32, (N_EXPERTS, nb), 1)
        erow2 = lax.broadcasted_iota(jnp.int32, (N_EXPERTS, nb), 0)
        ends = jnp.sum(jnp.where(elane == erow2, endb, 0.0), axis=0, keepdims=True)
        mrow = lax.broadcasted_iota(jnp.int32, (8, nb), 0)
        meta_ref[...] = jnp.where(mrow == 0, be, jnp.where(mrow == 1, used, jnp.where(mrow == 2, ends, 0.0))
                                  ).astype(jnp.int32)
        cnt_ref[...] = jnp.zeros_like(cnt_ref)

    @pl.when(ph == 1)
    def _():
        kr = lax.broadcasted_iota(jnp.int32, (tm, tm), 0)
        kc = lax.broadcasted_iota(jnp.int32, (tm, tm), 1)
        su = jnp.where(kr < kc, 1.0, 0.0).astype(BF16)
        csum = jnp.dot(oh.astype(BF16), su, preferred_element_type=F32)
        pos = csum + (off_ref[...] + cnt_ref[...])[:, 0:1]
        d0 = jnp.sum(oh0 * pos, axis=0, keepdims=True)
        d1 = jnp.sum(oh1 * pos, axis=0, keepdims=True)
        row = lax.broadcasted_iota(jnp.int32, (8, tm), 0)
        dest_ref[...] = jnp.where(row == 0, d0, jnp.where(row == 1, d1, 0.0)).astype(jnp.int32)
        cnt_ref[...] += tile_cnt


def _moe_rank(eid, n_blocks, *, bm, tm=1024):
    t = eid.shape[1]
    nb = -(-n_blocks // LANES) * LANES
    return pl.pallas_call(
        functools.partial(_rank_kernel, bm=bm), grid=(2, t // tm),
        in_specs=[pl.BlockSpec((8, tm), lambda p, i: (0, i))],
        out_specs=[pl.BlockSpec((8, tm), lambda p, i: (0, i * p)),
                   pl.BlockSpec((8, nb), lambda p, i: (0, 0))],
        out_shape=[jax.ShapeDtypeStruct((8, t), jnp.int32), jax.ShapeDtypeStruct((8, nb), jnp.int32)],
        scratch_shapes=[pltpu.VMEM((N_EXPERTS, LANES), F32), pltpu.VMEM((N_EXPERTS, LANES), F32)],
        compiler_params=_cparams("arbitrary", "arbitrary"), name="moe_rank")(eid)


def _dispatch_kernel(dest_ref, ends_ref, x_ref, g_ref, rows_hbm, zero_ref, hbuf_ref, sem, rsem, *, tm, bm):
    @pl.when(pl.program_id(0) == 0)
    def _():
        zero_ref[...] = jnp.zeros_like(zero_ref)

        def tail(e):
            start = ends_ref[0, e] - bm
            return pltpu.make_async_copy(zero_ref, rows_hbm.at[pl.ds(pl.multiple_of(start * SLAB, SLAB), bm * SLAB), :],
                                         sem)

        def nonempty(e):
            return ends_ref[0, e] > jnp.where(e > 0, ends_ref[0, jnp.maximum(e - 1, 0)], 0)

        def fill(e, carry):
            @pl.when(nonempty(e))
            def _():
                tail(e).start()
            return carry

        def drain(e, carry):
            @pl.when(nonempty(e))
            def _():
                tail(e).wait()
            return carry

        lax.fori_loop(0, N_EXPERTS, fill, 0)
        lax.fori_loop(0, N_EXPERTS, drain, 0)

        def unused(b):
            return pltpu.make_async_copy(
                zero_ref, rows_hbm.at[pl.ds(pl.multiple_of(b * (bm * SLAB), SLAB), bm * SLAB), :], sem)

        def fill_unused(b, carry):
            unused(b).start()
            return carry

        def drain_unused(b, carry):
            unused(b).wait()
            return carry

        first_unused = ends_ref[0, N_EXPERTS - 1] // bm
        n_blocks = rows_hbm.shape[0] // (bm * SLAB)
        lax.fori_loop(first_unused, n_blocks, fill_unused, 0)
        lax.fori_loop(first_unused, n_blocks, drain_unused, 0)

    i = pl.program_id(0)
    slot = lax.rem(i, 2)

    def drain_slot(s):
        for _ in range(TOP_K):
            pltpu.make_async_copy(hbuf_ref.at[s], rows_hbm.at[pl.ds(0, tm * SLAB), :], rsem.at[s]).wait()

    for s in range(2):
        @pl.when((slot == s) & (i >= 2))
        def _():
            drain_slot(s)

        @pl.when(slot == s)
        def _():
            _store_slabs(hbuf_ref.at[s], _ffn_norm(x_ref[...], g_ref[...]), tm)

            def issue(g, carry):
                for u in range(DMA_UNROLL):
                    tok = g * DMA_UNROLL + u
                    for k in range(TOP_K):
                        pltpu.make_async_copy(_slab(hbuf_ref.at[s], tok),
                                              _slab(rows_hbm, dest_ref[0, 0, k * tm + tok]),
                                              rsem.at[s]).start(priority=k)
                return carry

            lax.fori_loop(0, tm // DMA_UNROLL, issue, 0)

    last = pl.num_programs(0) - 1

    @pl.when(i == last)
    def _():
        for s in range(2):
            @pl.when((slot == s) | (last >= 1))
            def _():
                drain_slot(s)


def _moe_dispatch(dest_tiles, expert_ends, x2, gain, n_rows, *, tm, bm):
    t, d = x2.shape
    return pl.pallas_call(
        functools.partial(_dispatch_kernel, tm=tm, bm=bm), grid=(t // tm,),
        in_specs=[pl.BlockSpec((1, 1, TOP_K * tm), lambda i: (i, 0, 0), memory_space=pltpu.SMEM),
                  pl.BlockSpec(memory_space=pltpu.SMEM),
                  pl.BlockSpec((tm, d), lambda i: (i, 0)),
                  pl.BlockSpec((1, d), lambda i: (0, 0))],
        out_specs=pl.BlockSpec(memory_space=pl.ANY),
        out_shape=jax.ShapeDtypeStruct((n_rows * SLAB, LANES), F32),
        scratch_shapes=[pltpu.VMEM((bm * SLAB, LANES), F32), pltpu.VMEM((2, tm * SLAB, LANES), F32),
                        pltpu.SemaphoreType.DMA(()), pltpu.SemaphoreType.DMA((2,))],
        compiler_params=pltpu.CompilerParams(dimension_semantics=("arbitrary",), has_side_effects=True,
                                             vmem_limit_bytes=VMEM_LIMIT),
        name="moe_dispatch")(dest_tiles, expert_ends, x2, gain.reshape(1, d))


COMBINE_TM = 512
EXPERT_RING = 3


def _experts_kernel(be_ref, used_ref, x_hbm, wg_ref, wu_ref, wd_ref, y_ref, wgb_ref, wub_ref, wdb_ref, xbuf_ref,
                    xsem):
    b = pl.program_id(0)
    used = used_ref[0]
    bm = y_ref.shape[0] // SLAB

    def fetch(blk):
        slot = lax.rem(blk, EXPERT_RING)
        rows = pl.ds(pl.multiple_of(blk * (bm * SLAB), SLAB), bm * SLAB)
        return pltpu.make_async_copy(x_hbm.at[rows, :], xbuf_ref.at[slot], xsem.at[slot])

    @pl.when(b == 0)
    def _():
        for first in range(EXPERT_RING - 1):
            @pl.when(first < used)
            def _():
                fetch(first).start()

    @pl.when(b + (EXPERT_RING - 1) < used)
    def _():
        fetch(b + (EXPERT_RING - 1)).start()

    prev = be_ref[jnp.maximum(b - 1, 0)]

    @pl.when((b == 0) | (be_ref[b] != prev))
    def _():
        wgb_ref[...] = wg_ref[0, 0].astype(BF16)
        wub_ref[...] = wu_ref[0, 0].astype(BF16)
        wdb_ref[...] = wd_ref[0, 0].astype(BF16)

    @pl.when(b < used)
    def _():
        fetch(b).wait()
        x = _load_slabs(xbuf_ref.at[lax.rem(b, EXPERT_RING)], bm).astype(BF16)
        gate = jnp.dot(x, wgb_ref[...], preferred_element_type=F32)
        up = jnp.dot(x, wub_ref[...], preferred_element_type=F32)
        hid = (_silu(gate) * up).astype(BF16)
        _store_slabs(y_ref, jnp.dot(hid, wdb_ref[...], preferred_element_type=F32), bm)

    @pl.when(b >= used_ref[0])
    def _():
        y_ref[...] = jnp.zeros_like(y_ref)


def _moe_experts(block_expert, n_used, x_rows, w_gate, w_up, w_down, layer, *, bm):
    n_rows = x_rows.shape[0] // SLAB
    d, de = w_gate.shape[2], w_gate.shape[3]
    grid_spec = pltpu.PrefetchScalarGridSpec(
        num_scalar_prefetch=2, grid=(n_rows // bm,),
        in_specs=[pl.BlockSpec(memory_space=pl.ANY),
                  pl.BlockSpec((1, 1, d, de), lambda b, be, nu: (layer, be[b], 0, 0)),
                  pl.BlockSpec((1, 1, d, de), lambda b, be, nu: (layer, be[b], 0, 0)),
                  pl.BlockSpec((1, 1, de, d), lambda b, be, nu: (layer, be[b], 0, 0))],
        out_specs=pl.BlockSpec((bm * SLAB, LANES), lambda b, be, nu: (b, 0)),
        scratch_shapes=[pltpu.VMEM((d, de), BF16), pltpu.VMEM((d, de), BF16), pltpu.VMEM((de, d), BF16),
                        pltpu.VMEM((EXPERT_RING, bm * SLAB, LANES), F32), pltpu.SemaphoreType.DMA((EXPERT_RING,))])
    return pl.pallas_call(
        _experts_kernel, grid_spec=grid_spec, out_shape=jax.ShapeDtypeStruct((n_rows * SLAB, LANES), F32),
        compiler_params=_cparams("arbitrary"), name="moe_experts",
    )(block_expert, n_used, x_rows, w_gate, w_up, w_down)


def _combine_kernel(dest_ref, next_ref, x_ref, gate_ref, y_hbm, o_ref, buf_ref, sem, *, tm):
    _combine_tile(dest_ref, next_ref, x_ref, gate_ref, y_hbm, o_ref, buf_ref, sem, tm)


def _combine_tile(dest_ref, next_ref, x_ref, gate_ref, y_hbm, o_ref, buf_ref, sem, tm):
    i = pl.program_id(0)
    slot = lax.rem(i, 2)

    def gather(idx_ref, s):
        def issue(g, carry):
            for u in range(DMA_UNROLL):
                tok = g * DMA_UNROLL + u
                for k in range(TOP_K):
                    pltpu.make_async_copy(_slab(y_hbm, idx_ref[0, 0, k * tm + tok]), _slab(buf_ref.at[s, k], tok),
                                          sem.at[s]).start(priority=k)
            return carry

        lax.fori_loop(0, tm // DMA_UNROLL, issue, 0)

    for s in range(2):
        @pl.when((i == 0) & (slot == s))
        def _():
            gather(dest_ref, s)

        @pl.when((i + 1 < pl.num_programs(0)) & (slot == 1 - s))
        def _():
            gather(next_ref, s)

    er = lax.broadcasted_iota(jnp.int32, (8, LANES), 0)
    ec = lax.broadcasted_iota(jnp.int32, (8, LANES), 1)
    eye = jnp.where(er == ec, 1.0, 0.0).astype(F32)
    gcol = lax.dot_general(gate_ref[...], eye, (((0,), (0,)), ((), ())), preferred_element_type=F32,
                           precision=lax.Precision.HIGHEST)
    for s in range(2):
        @pl.when(slot == s)
        def _():
            for k in range(TOP_K):
                pltpu.make_async_copy(y_hbm.at[pl.ds(0, tm * SLAB), :], buf_ref.at[s, k], sem.at[s]).wait()
            o_ref[...] = (x_ref[...] + gcol[:, 0:1] * _load_slabs(buf_ref.at[s, 0], tm)
                          + gcol[:, 1:2] * _load_slabs(buf_ref.at[s, 1], tm))


def _moe_combine(dest_tiles, x2, gates, y_rows, *, tm):
    t, d = x2.shape
    n = t // tm
    dest_spec = functools.partial(pl.BlockSpec, (1, 1, TOP_K * tm), memory_space=pltpu.SMEM)
    return pl.pallas_call(
        functools.partial(_combine_kernel, tm=tm), grid=(n,),
        in_specs=[dest_spec(lambda i: (i, 0, 0)),
                  dest_spec(lambda i: (jnp.minimum(i + 1, n - 1), 0, 0)),
                  pl.BlockSpec((tm, d), lambda i: (i, 0)),
                  pl.BlockSpec((8, tm), lambda i: (0, i)),
                  pl.BlockSpec(memory_space=pl.ANY)],
        out_specs=pl.BlockSpec((tm, d), lambda i: (i, 0)),
        out_shape=jax.ShapeDtypeStruct((t, d), F32),
        scratch_shapes=[pltpu.VMEM((2, TOP_K, tm * SLAB, LANES), F32), pltpu.SemaphoreType.DMA((2,))],
        compiler_params=_cparams("arbitrary"), name="moe_combine")(dest_tiles, dest_tiles, x2, gates, y_rows)


def _dest_tiles(dest, tm):
    t = dest.shape[1]
    return dest[0:TOP_K].reshape(TOP_K, t // tm, tm).transpose(1, 0, 2).reshape(t // tm, 1, TOP_K * tm)


def _moe_expert_rows(x2, eid, gain, w_gate, w_up, w_down, layer, *, tm=1024):
    t, d = x2.shape
    bm = MOE_BM
    n_rows = t * TOP_K + N_EXPERTS * bm
    n_blocks = n_rows // bm
    dest, meta = _moe_rank(eid, n_blocks, bm=bm)
    x_rows = _moe_dispatch(_dest_tiles(dest, tm), meta[2:3, :N_EXPERTS], x2, gain, n_rows, tm=tm, bm=bm)
    y_rows = _moe_experts(meta[0, :n_blocks], meta[1, 0:1], x_rows, w_gate, w_up, w_down, layer, bm=bm)
    return dest, y_rows


def _even_odd_perm():
    half = RET_QK_DIM // 2
    src = np.concatenate([2 * np.arange(half), 2 * np.arange(half) + 1])
    perm = np.zeros((RET_QK_DIM, RET_QK_DIM), np.float32)
    perm[src, np.arange(RET_QK_DIM)] = 1.0
    return jnp.asarray(perm, dtype=BF16)


def kernel(x, ln_mix_even, w_in_even, conv_w_even, a_log_even, dt_bias_even, q_gain_even, k_gain_even, sinks_even, gdn_gain_even, w_out_even, ln_mix_odd, w_in_odd, w_out_odd, ln_ffn, w_router_group, b_router_group, w_router_expert, b_router_expert, w_gate, w_up, w_down):
    bsz, s, d = x.shape
    t = bsz * s
    x2 = x.reshape(t, d).astype(F32)
    depth = ln_ffn.shape[0]
    pending = None

    def settle(x2, pending):
        dest, gates, y_rows = pending
        return _moe_combine(_dest_tiles(dest, COMBINE_TM), x2, gates, y_rows, tm=COMBINE_TM)

    for layer in range(depth):
        i = layer // 2
        router = _router_operands(ln_ffn[layer], w_router_group[layer], b_router_group[layer],
                                  w_router_expert[layer], b_router_expert[layer])
        if pending is not None:
            x2, pending = settle(x2, pending), None
        if layer % 2 == 0:
            w_in = w_in_even[i]
            swa_end = SWA_Q_W + 2 * SWA_KV_W
            w_main = jnp.concatenate([w_in[:, :SWA_Q_W], w_in[:, swa_end:EVEN_MAIN], w_in[:, SWA_Q_W:swa_end]],
                                     axis=1).astype(BF16)
            w_gates = jnp.pad(w_in[:, EVEN_MAIN:].astype(F32), ((0, 0), (0, LANES - 2 * GDN_HEADS)))
            p, pg = _norm_proj(x2, ln_mix_even[i], w_main, w_gates, conv_w=conv_w_even[i],
                               conv_cols=(EVEN_COL_GDN, EVEN_COL_Z), seq_len=s, tm=1024)
            p = p.reshape(bsz, s, EVEN_MAIN)
            y_a = _swa(p, sinks_even[i], q_gain_even[i], k_gain_even[i], bsz, s)
            y_b = _gdn(p, pg.reshape(bsz, s, LANES), a_log_even[i], dt_bias_even[i], gdn_gain_even[i], bsz, s)
            w_out = w_out_even[i].astype(BF16)
            x2, eid, gates = _out_proj(x2, [y_a.reshape(t, SWA_Q_W), y_b.reshape(t, GDN_W)],
                                       [w_out[:SWA_Q_W], w_out[SWA_Q_W:]], router)
        else:
            p = _norm_proj(x2, ln_mix_odd[i], w_in_odd[i].astype(BF16), perm=_even_odd_perm(), n_perm=2 * RET_HEADS)
            y = _retention(p.reshape(bsz, s, ODD_IN), bsz, s)
            x2, eid, gates = _out_proj(x2, [y.reshape(t, RET_V_W)], [w_out_odd[i].astype(BF16)], router)
        dest, y_rows = _moe_expert_rows(x2, eid, ln_ffn[layer], w_gate, w_up, w_down, layer)
        pending = (dest, gates, y_rows)
    x2 = settle(x2, pending)
    return x2.reshape(bsz, s, d).astype(x.dtype)
```

```python
import functools

import numpy as np
import jax
import jax.numpy as jnp
from jax import lax
from jax.experimental import pallas as pl
from jax.experimental.pallas import tpu as pltpu

F32 = jnp.float32
BF16 = jnp.bfloat16

D_MODEL = 1024
SWA_Q_HEADS = 8
SWA_KV_HEADS = 2
SWA_HEAD_DIM = 64
WINDOW = 128
ROT_DIM = SWA_HEAD_DIM // 4
ROPE_THETA = 500000.0
GDN_HEADS = 4
GDN_HEAD_DIM = 128
CONV_K = 4
GDN_CHUNK = 64
RET_HEADS = 4
RET_QK_DIM = 256
RET_V_DIM = 512
RET_CHUNK = 128
XPOS_THETA = 10000.0
N_GROUPS = 4
EXPERTS_PER_GROUP = 8
N_EXPERTS = N_GROUPS * EXPERTS_PER_GROUP
TOP_K = 2
D_EXPERT = 512
EPS = 1e-6

SWA_Q_W = SWA_Q_HEADS * SWA_HEAD_DIM
SWA_KV_W = SWA_KV_HEADS * SWA_HEAD_DIM
GDN_W = GDN_HEADS * GDN_HEAD_DIM
EVEN_MAIN = SWA_Q_W + 2 * SWA_KV_W + 4 * GDN_W
EVEN_COL_Q = 0
EVEN_COL_GDN = SWA_Q_W
EVEN_COL_Z = EVEN_COL_GDN + 3 * GDN_W
EVEN_COL_K = EVEN_COL_Z + GDN_W
EVEN_COL_V = EVEN_COL_K + SWA_KV_W
RET_QK_W = RET_HEADS * RET_QK_DIM
RET_V_W = RET_HEADS * RET_V_DIM
ODD_IN = 2 * RET_QK_W + 2 * RET_V_W

LANES = 128
VMEM_LIMIT = 56 * 1024 * 1024

NEG_INF = float("-inf")


def _cparams(*sem):
    return pltpu.CompilerParams(dimension_semantics=sem, vmem_limit_bytes=VMEM_LIMIT)


def _resident(shape):
    return pl.BlockSpec(shape, lambda i: (0,) * len(shape), pipeline_mode=pl.Buffered(1))


def _silu(x):
    return x * (1.0 / (1.0 + jnp.exp(-x)))


def _norm_proj_kernel(x_ref, g_ref, w_ref, *rest, tn, has_gate, n_perm, perm_w, conv_cols, seq_tiles):
    rest = list(rest)
    wg_ref = rest.pop(0) if has_gate else None
    perm_ref = rest.pop(0) if n_perm else None
    cw_ref = rest.pop(0) if conv_cols else None
    o_ref = rest.pop(0)
    og_ref = rest.pop(0) if has_gate else None
    wp_ref = rest.pop(0) if n_perm else None
    cbuf_ref = rest.pop(0) if conv_cols else None
    tm = x_ref.shape[0]
    pad = 8

    if conv_cols:
        @pl.when(pl.program_id(0) % seq_tiles == 0)
        def _():
            cbuf_ref[0:pad, :] = jnp.zeros((pad, cbuf_ref.shape[1]), F32)

    if n_perm:
        @pl.when(pl.program_id(0) == 0)
        def _():
            for j in range(n_perm):
                cols = slice(j * perm_w, (j + 1) * perm_w)
                wp_ref[:, cols] = jnp.dot(w_ref[:, cols], perm_ref[...], preferred_element_type=F32).astype(BF16)

    x = x_ref[...]
    h = x * lax.rsqrt(jnp.mean(x * x, axis=-1, keepdims=True) + EPS) * g_ref[...]
    hb = h.astype(BF16)
    n = o_ref.shape[-1]
    for lo in range(0, n, tn):
        hi = min(lo + tn, n)
        src = wp_ref if hi <= n_perm * perm_w else w_ref
        res = jnp.dot(hb, src[:, lo:hi], preferred_element_type=F32)
        if conv_cols and conv_cols[0] <= lo and hi <= conv_cols[1]:
            cc = slice(lo - conv_cols[0], hi - conv_cols[0])
            cbuf_ref[pad:pad + tm, cc] = res
            res = None
            for j in range(CONV_K):
                off = pad - (CONV_K - 1) + j
                term = cbuf_ref[off:off + tm, cc] * cw_ref[j:j + 1, cc]
                res = term if res is None else res + term
            res = _silu(res)
            cbuf_ref[0:pad, cc] = cbuf_ref[tm:tm + pad, cc]
        o_ref[:, lo:hi] = res.astype(o_ref.dtype)
    if has_gate:
        h_lo = (h - hb.astype(F32)).astype(BF16)
        r = jnp.dot(hb, wg_ref[...], preferred_element_type=F32)
        og_ref[...] = r[:, :LANES] + r[:, LANES:] + jnp.dot(h_lo, wg_ref[:, :LANES], preferred_element_type=F32)


def _norm_proj(x2, gain, w_bf16, w_gate_f32=None, perm=None, n_perm=0, conv_w=None, conv_cols=None, seq_len=None,
               *, tm=512, tn=512):
    t, d = x2.shape
    n = w_bf16.shape[1]
    has_gate = w_gate_f32 is not None
    perm_w = perm.shape[0] if n_perm else 0
    assert (n_perm * perm_w) % tn == 0
    seq_tiles = seq_len // tm if conv_cols else 0
    assert not conv_cols or (conv_cols[0] % tn == 0 and conv_cols[1] % tn == 0 and seq_len % tm == 0)
    in_specs = [pl.BlockSpec((tm, d), lambda i: (i, 0)),
                _resident((1, d)),
                _resident((d, n))]
    args = [x2, gain.reshape(1, d), w_bf16]
    out_shape = [jax.ShapeDtypeStruct((t, n), BF16)]
    out_specs = [pl.BlockSpec((tm, n), lambda i: (i, 0))]
    scratch = []
    if has_gate:
        wg_hi = w_gate_f32.astype(BF16)
        wg_lo = (w_gate_f32 - wg_hi.astype(F32)).astype(BF16)
        in_specs.append(_resident((d, 2 * LANES)))
        args.append(jnp.concatenate([wg_hi, wg_lo], axis=1))
        out_shape.append(jax.ShapeDtypeStruct((t, LANES), F32))
        out_specs.append(pl.BlockSpec((tm, LANES), lambda i: (i, 0)))
    if n_perm:
        in_specs.append(_resident((perm_w, perm_w)))
        args.append(perm)
        scratch.append(pltpu.VMEM((d, n_perm * perm_w), BF16))
    if conv_cols:
        cw = conv_cols[1] - conv_cols[0]
        in_specs.append(_resident((CONV_K, cw)))
        args.append(conv_w.astype(F32))
        scratch.append(pltpu.VMEM((tm + 8, cw), F32))
    res = pl.pallas_call(
        functools.partial(_norm_proj_kernel, tn=tn, has_gate=has_gate, n_perm=n_perm, perm_w=perm_w,
                          conv_cols=conv_cols, seq_tiles=seq_tiles),
        grid=(t // tm,), in_specs=in_specs, out_specs=out_specs, out_shape=out_shape, scratch_shapes=scratch,
        compiler_params=_cparams("arbitrary"), name="norm_proj")(*args)
    return res if len(res) > 1 else res[0]


def _out_proj_kernel(*refs, n_in, route):
    res_ref = refs[0]
    a_refs = refs[1:1 + n_in]
    w_refs = refs[1 + n_in:1 + 2 * n_in]
    rest = refs[1 + 2 * n_in:]
    acc = res_ref[...]
    for a_ref, w_ref in zip(a_refs, w_refs):
        acc = acc + jnp.dot(a_ref[...], w_ref[...], preferred_element_type=F32)
    if route:
        g_ref, wr_ref, wrlo_ref, br_ref, o_ref, eid_ref, gate_ref = rest
        _route_tile(acc, g_ref, wr_ref, wrlo_ref, br_ref, eid_ref, gate_ref)
    else:
        (o_ref,) = rest
    o_ref[...] = acc


def _out_proj(res, acts, ws, router=None, *, tm=1024):
    t, d = res.shape
    n_in = len(acts)
    in_specs = [pl.BlockSpec((tm, d), lambda i: (i, 0))]
    in_specs += [pl.BlockSpec((tm, a.shape[1]), lambda i: (i, 0)) for a in acts]
    in_specs += [_resident(w.shape) for w in ws]
    out_specs = [pl.BlockSpec((tm, d), lambda i: (i, 0))]
    out_shape = [jax.ShapeDtypeStruct((t, d), F32)]
    extra = ()
    if router is not None:
        extra = tuple(router)
        in_specs += [_resident(r.shape) for r in extra]
        out_specs += [pl.BlockSpec((8, tm), lambda i: (0, i))] * 2
        out_shape += [jax.ShapeDtypeStruct((8, t), jnp.int32), jax.ShapeDtypeStruct((8, t), F32)]
    out = pl.pallas_call(
        functools.partial(_out_proj_kernel, n_in=n_in, route=router is not None),
        grid=(t // tm,), in_specs=in_specs, out_specs=out_specs, out_shape=out_shape,
        compiler_params=_cparams("parallel"), name="out_proj")(res, *acts, *ws, *extra)
    return out if router is not None else out[0]


def _retention_kernel(q_ref, k_ref, v_ref, z_ref, cos_ref, sin_ref, decay_ref, xi_ref, zeta_ref, gam_ref,
                      o_ref, state_ref, *, c, n_chunks):
    half = RET_QK_DIM // 2

    @pl.when(pl.program_id(2) == 0)
    def _():
        state_ref[...] = jnp.zeros_like(state_ref)

    cos = cos_ref[...]
    sin = sin_ref[...]

    def rotate(t):
        a = t[:, :half].astype(F32)
        b = t[:, half:].astype(F32)
        return jnp.concatenate([a * cos - b * sin, b * cos + a * sin], axis=-1)

    q = rotate(q_ref[0])
    k = rotate(k_ref[0])
    decay = decay_ref[0]
    gamma_c = gam_ref[0, 0:1, 0:1]
    xi = xi_ref[0][:, 0:1]
    zeta = zeta_ref[0][:, 0:1]
    chunks = range(n_chunks)
    rows = [slice(ci * c, (ci + 1) * c) for ci in chunks]
    qb = [q[r].astype(BF16) for r in rows]
    kb = [k[r].astype(BF16) for r in rows]
    kz = [(k[r] * zeta).astype(BF16) for r in rows]
    vc = [v_ref[0, r, :] for r in rows]
    inner = [lax.dot_general(qb[i], kb[i], (((1,), (1,)), ((), ())), preferred_element_type=F32) * decay
             for i in chunks]
    kv = [lax.dot_general(kz[i], vc[i], (((0,), (0,)), ((), ())), preferred_element_type=F32) for i in chunks]
    intra = [jnp.dot(inner[i].astype(BF16), vc[i], preferred_element_type=F32) for i in chunks]
    states = [state_ref[...]]
    for i in chunks:
        states.append(states[i] * gamma_c + kv[i])
    state_ref[...] = states[-1]
    cross = [jnp.dot(qb[i], states[i].astype(BF16), preferred_element_type=F32) for i in chunks]
    for i in chunks:
        out = intra[i] + cross[i] * xi
        o = out * lax.rsqrt(jnp.mean(out * out, axis=-1, keepdims=True) + EPS)
        z = z_ref[0, rows[i], :].astype(F32)
        o_ref[0, rows[i], :] = (_silu(z) * o).astype(o_ref.dtype)


RET_KERNEL_CHUNK = 256


def _retention(p, bsz, s, *, blk=1024):
    h = RET_HEADS
    c = RET_KERNEL_CHUNK
    half = RET_QK_DIM // 2
    pos = jnp.arange(s, dtype=F32)
    freq = 1.0 / (XPOS_THETA ** jnp.linspace(0.0, 1.0, half, dtype=F32))
    ang = pos[:, None] * freq[None, :]
    cos, sin = jnp.cos(ang), jnp.sin(ang)
    kscale = RET_QK_DIM ** -0.5
    log_gamma = jnp.log(1.0 - 2.0 ** (-5.0 - jnp.arange(h, dtype=F32)))
    idx = jnp.arange(c, dtype=F32)
    rel = idx[:, None] - idx[None, :]
    decay = jnp.exp(jnp.where(rel >= 0, rel * log_gamma[:, None, None], NEG_INF))
    xi = jnp.broadcast_to(jnp.exp((idx + 1.0) * log_gamma[:, None])[..., None], (h, c, LANES))
    zeta = jnp.broadcast_to(jnp.exp((c - 1.0 - idx) * log_gamma[:, None])[..., None], (h, c, LANES))
    gam = jnp.broadcast_to(jnp.exp(c * log_gamma)[:, None, None], (h, 8, LANES))
    nq = RET_QK_W // RET_QK_DIM
    nv = RET_V_W // RET_V_DIM
    return pl.pallas_call(
        functools.partial(_retention_kernel, c=c, n_chunks=blk // c),
        grid=(bsz, h, s // blk),
        in_specs=[
            pl.BlockSpec((1, blk, RET_QK_DIM), lambda b, hh, i: (b, i, hh)),
            pl.BlockSpec((1, blk, RET_QK_DIM), lambda b, hh, i: (b, i, nq + hh)),
            pl.BlockSpec((1, blk, RET_V_DIM), lambda b, hh, i: (b, i, nv + hh)),
            pl.BlockSpec((1, blk, RET_V_DIM), lambda b, hh, i: (b, i, 2 * nv + hh)),
            pl.BlockSpec((blk, half), lambda b, hh, i: (i, 0)),
            pl.BlockSpec((blk, half), lambda b, hh, i: (i, 0)),
            pl.BlockSpec((1, c, c), lambda b, hh, i: (hh, 0, 0)),
            pl.BlockSpec((1, c, LANES), lambda b, hh, i: (hh, 0, 0)),
            pl.BlockSpec((1, c, LANES), lambda b, hh, i: (hh, 0, 0)),
            pl.BlockSpec((1, 8, LANES), lambda b, hh, i: (hh, 0, 0)),
        ],
        out_specs=pl.BlockSpec((1, blk, RET_V_DIM), lambda b, hh, i: (b, i, hh)),
        out_shape=jax.ShapeDtypeStruct((bsz, s, RET_V_W), BF16),
        scratch_shapes=[pltpu.VMEM((RET_QK_DIM, RET_V_DIM), F32)],
        compiler_params=_cparams("parallel", "parallel", "arbitrary"), name="retention",
    )(p, p, p, p, cos, sin, decay * kscale, xi, zeta * kscale, gam)


def _swa_kernel(sink_ref, q_ref, kc_ref, kp_ref, vc_ref, vp_ref, cos_ref, sin_ref, cosp_ref, sinp_ref,
                qg_ref, kg_ref, bd_ref, rp_ref, o_ref, *, nq):
    w = WINDOW
    dh = SWA_HEAD_DIM
    grp = SWA_Q_HEADS // SWA_KV_HEADS
    n = pl.program_id(1)

    def head_rms(x, gain, bd):
        sq = x * x
        hi = sq.astype(BF16)
        lo = (sq - hi.astype(F32)).astype(BF16)
        ssum = jnp.dot(hi, bd, preferred_element_type=F32) + jnp.dot(lo, bd, preferred_element_type=F32)
        return x * lax.rsqrt(ssum * (1.0 / dh) + EPS) * gain

    def rope(x, cos, sin, rp):
        partner = jnp.dot(x.astype(BF16), rp, preferred_element_type=F32)
        return x * cos + partner * sin

    cos, sin = cos_ref[...], sin_ref[...]
    cosq = jnp.concatenate([cos] * (SWA_Q_W // LANES), axis=1)
    sinq = jnp.concatenate([sin] * (SWA_Q_W // LANES), axis=1)
    q = rope(head_rms(q_ref[0].astype(F32), qg_ref[...], bd_ref[...]), cosq, sinq, rp_ref[...])
    q = (q * (dh ** -0.5)).astype(BF16)
    bdk = bd_ref[0:SWA_KV_W, 0:SWA_KV_W]
    rpk = rp_ref[0:SWA_KV_W, 0:SWA_KV_W]
    kcur = rope(head_rms(kc_ref[0].astype(F32), kg_ref[...], bdk), cos, sin, rpk)
    kprev = rope(head_rms(kp_ref[0].astype(F32), kg_ref[...], bdk), cosp_ref[...], sinp_ref[...], rpk)
    kall = jnp.concatenate([kprev, kcur], axis=0).astype(BF16)
    vall = jnp.concatenate([vp_ref[0], vc_ref[0]], axis=0)

    i = lax.broadcasted_iota(jnp.int32, (w, 2 * w), 0)
    j = lax.broadcasted_iota(jnp.int32, (w, 2 * w), 1)
    band = (j > i) & (j <= i + w)
    masks = [band & ((j >= w) | (n > 0))] + [band] * (nq - 1)
    units = [(qb, hq) for qb in range(nq) for hq in range(SWA_Q_HEADS)]
    kh = {(qb, hk): kall[qb * w:(qb + 2) * w, hk * dh:(hk + 1) * dh] for qb in range(nq) for hk in range(SWA_KV_HEADS)}
    vh = {(qb, hk): vall[qb * w:(qb + 2) * w, hk * dh:(hk + 1) * dh] for qb in range(nq) for hk in range(SWA_KV_HEADS)}
    sc = [lax.dot_general(q[qb * w:(qb + 1) * w, hq * dh:(hq + 1) * dh], kh[qb, hq // grp], (((1,), (1,)), ((), ())),
                          preferred_element_type=F32) for qb, hq in units]
    sc = [jnp.where(masks[qb], sc[u], NEG_INF) for u, (qb, hq) in enumerate(units)]
    m = [jnp.maximum(jnp.max(sc[u], axis=-1, keepdims=True), sink_ref[hq]) for u, (qb, hq) in enumerate(units)]
    p = [jnp.exp(sc[u] - m[u]) for u in range(len(units))]
    denom = [jnp.sum(p[u], axis=-1, keepdims=True) + jnp.exp(sink_ref[hq] - m[u]) for u, (qb, hq) in enumerate(units)]
    pv = [jnp.dot(p[u].astype(BF16), vh[qb, hq // grp], preferred_element_type=F32) for u, (qb, hq) in enumerate(units)]
    for qb in range(nq):
        row = [pv[qb * SWA_Q_HEADS + hq] * (1.0 / denom[qb * SWA_Q_HEADS + hq]) for hq in range(SWA_Q_HEADS)]
        o_ref[0, qb * w:(qb + 1) * w, :] = jnp.concatenate(row, axis=-1).astype(o_ref.dtype)


def _rope_tables(s):
    half = ROT_DIM // 2
    pos = jnp.arange(s, dtype=F32)
    inv_freq = 1.0 / (ROPE_THETA ** (jnp.arange(half, dtype=F32) * 2.0 / ROT_DIM))
    ang = pos[:, None] * inv_freq[None, :]
    cos, sin = jnp.cos(ang), jnp.sin(ang)
    ones = jnp.ones((s, SWA_HEAD_DIM - ROT_DIM), F32)
    cos_h = jnp.concatenate([cos, cos, ones], axis=1)
    sin_h = jnp.concatenate([-sin, sin, 0.0 * ones], axis=1)
    reps = LANES // SWA_HEAD_DIM
    return jnp.tile(cos_h, (1, reps)), jnp.tile(sin_h, (1, reps))


def _swa(p, sinks, q_gain, k_gain, bsz, s, *, nq=8):
    w = WINDOW
    blk = nq * w
    cos, sin = _rope_tables(s)
    qg = jnp.tile(q_gain.astype(F32), SWA_Q_HEADS).reshape(1, SWA_Q_W)
    kg = jnp.tile(k_gain.astype(F32), SWA_KV_HEADS).reshape(1, SWA_KV_W)
    lanes = np.arange(SWA_Q_W)
    seg = lanes // SWA_HEAD_DIM
    bd = jnp.asarray(seg[:, None] == seg[None, :], dtype=BF16)
    half = ROT_DIM // 2
    in_head = lanes % SWA_HEAD_DIM
    src = np.where(in_head < half, lanes + half, lanes - half)
    rp = np.zeros((SWA_Q_W, SWA_Q_W), np.float32)
    rot = in_head < ROT_DIM
    rp[src[rot], lanes[rot]] = 1.0
    rp = jnp.asarray(rp, dtype=BF16)
    qblk, kblk, vblk = EVEN_COL_Q // SWA_Q_W, EVEN_COL_K // SWA_KV_W, EVEN_COL_V // SWA_KV_W
    prev = lambda i: jnp.maximum(i * nq - 1, 0)
    return pl.pallas_call(
        functools.partial(_swa_kernel, nq=nq),
        grid=(bsz, s // blk),
        in_specs=[
            pl.BlockSpec(memory_space=pltpu.SMEM),
            pl.BlockSpec((1, blk, SWA_Q_W), lambda b, i: (b, i, qblk)),
            pl.BlockSpec((1, blk, SWA_KV_W), lambda b, i: (b, i, kblk)),
            pl.BlockSpec((1, w, SWA_KV_W), lambda b, i: (b, prev(i), kblk)),
            pl.BlockSpec((1, blk, SWA_KV_W), lambda b, i: (b, i, vblk)),
            pl.BlockSpec((1, w, SWA_KV_W), lambda b, i: (b, prev(i), vblk)),
            pl.BlockSpec((blk, LANES), lambda b, i: (i, 0)),
            pl.BlockSpec((blk, LANES), lambda b, i: (i, 0)),
            pl.BlockSpec((w, LANES), lambda b, i: (prev(i), 0)),
            pl.BlockSpec((w, LANES), lambda b, i: (prev(i), 0)),
            pl.BlockSpec((1, SWA_Q_W), lambda b, i: (0, 0)),
            pl.BlockSpec((1, SWA_KV_W), lambda b, i: (0, 0)),
            pl.BlockSpec((SWA_Q_W, SWA_Q_W), lambda b, i: (0, 0)),
            pl.BlockSpec((SWA_Q_W, SWA_Q_W), lambda b, i: (0, 0)),
        ],
        out_specs=pl.BlockSpec((1, blk, SWA_Q_W), lambda b, i: (b, i, 0)),
        out_shape=jax.ShapeDtypeStruct((bsz, s, SWA_Q_W), BF16),
        compiler_params=_cparams("parallel", "parallel"), name="swa",
    )(sinks.astype(F32), p, p, p, p, p, cos, sin, cos, sin, qg, kg, bd, rp)


def _softplus(x):
    return jnp.maximum(x, 0.0) + jnp.log1p(jnp.exp(-jnp.abs(x)))


def _split3(x):
    x1 = x.astype(BF16)
    r1 = x - x1.astype(F32)
    x2 = r1.astype(BF16)
    return x1, x2, (r1 - x2.astype(F32)).astype(BF16)


def _gdn_kernel(q_ref, k_ref, v_ref, z_ref, gate_ref, alog_ref, dt_ref, gain_ref, o_ref, state_ref, *, blk):
    c = GDN_CHUNK
    dk = GDN_HEAD_DIM
    nh = GDN_HEADS
    sub = LANES
    nt = (((1,), (1,)), ((), ()))

    @pl.when(pl.program_id(1) == 0)
    def _():
        state_ref[...] = jnp.zeros_like(state_ref)

    gates = gate_ref[0]
    g_all = -jnp.exp(alog_ref[...]) * _softplus(gates + dt_ref[...])
    beta_all = 1.0 / (1.0 + jnp.exp(-gates))
    ii = lax.broadcasted_iota(jnp.int32, (sub, sub), 0)
    jj = lax.broadcasted_iota(jnp.int32, (sub, sub), 1)
    same = (ii // c) == (jj // c)
    incl = same & (ii >= jj)
    strict = same & (ii > jj)
    tri = jnp.where(incl, 1.0, 0.0).astype(BF16)
    eye = jnp.where(ii == jj, 1.0, 0.0).astype(F32)
    er = lax.broadcasted_iota(jnp.int32, (8, LANES), 0)
    ec = lax.broadcasted_iota(jnp.int32, (8, LANES), 1)
    eye8 = jnp.where(er == ec, 1.0, 0.0).astype(BF16)
    g_parts = _split3(g_all)
    g_rows = sum(lax.dot_general(eye8, gp, nt, preferred_element_type=F32) for gp in g_parts)
    gr_parts = _split3(g_rows)
    gc_cols, gc_rows = [], []
    for sb in range(blk // sub):
        rows = slice(sb * sub, (sb + 1) * sub)
        gc_cols.append(sum(jnp.dot(tri, gp[rows], preferred_element_type=F32) for gp in g_parts))
        gc_rows.append(sum(lax.dot_general(gp[:, rows], tri, nt, preferred_element_type=F32) for gp in gr_parts))

    n_sub = blk // sub
    units = [(hd, sb) for hd in range(nh) for sb in range(n_sub)]
    dot = functools.partial(jnp.dot, preferred_element_type=F32)

    def dot_nt(x, y):
        return lax.dot_general(x, y, nt, preferred_element_type=F32)

    qs, ks, vs, betas = [], [], [], []
    for hd in range(nh):
        cols = slice(hd * dk, (hd + 1) * dk)
        q = q_ref[0, :, cols].astype(F32)
        k = k_ref[0, :, cols].astype(F32)
        qs.append(q * lax.rsqrt(jnp.sum(q * q, axis=-1, keepdims=True) + EPS) * (dk ** -0.5))
        ks.append(k * lax.rsqrt(jnp.sum(k * k, axis=-1, keepdims=True) + EPS))
        vs.append(v_ref[0, :, cols].astype(F32))
        betas.append(beta_all[:, nh + hd:nh + hd + 1])

    def rows_of(sb):
        return slice(sb * sub, (sb + 1) * sub)

    gcb = [jnp.broadcast_to(gc_cols[sb][:, hd:hd + 1], (sub, LANES)) for hd, sb in units]
    decay = [jnp.exp(jnp.where(incl, gcb[u] - gc_rows[sb][hd:hd + 1, :], NEG_INF)) for u, (hd, sb) in enumerate(units)]
    eg = [jnp.exp(g) for g in gcb]
    q_u = [qs[hd][rows_of(sb)] for hd, sb in units]
    k_u = [ks[hd][rows_of(sb)] for hd, sb in units]
    kb = [k_u[u] * betas[hd][rows_of(sb)] for u, (hd, sb) in enumerate(units)]
    kbf = [x.astype(BF16) for x in k_u]
    a = [dot_nt(kb[u].astype(BF16), kbf[u]) for u in range(len(units))]
    m = [jnp.where(strict, -(a[u] * decay[u]), 0.0) for u in range(len(units))]
    r = [eye + x for x in m]
    mb = [x.astype(BF16) for x in m]
    pw = [dot(x, x) for x in mb]
    for _ in range(int(np.log2(c)) - 2):
        pb = [x.astype(BF16) for x in pw]
        both = [dot(jnp.concatenate([r[u].astype(BF16), pb[u]], axis=0), pb[u]) for u in range(len(units))]
        r = [r[u] + both[u][:sub] for u in range(len(units))]
        pw = [x[sub:] for x in both]
    r = [r[u] + dot(r[u].astype(BF16), pw[u].astype(BF16)) for u in range(len(units))]
    rhs = [jnp.concatenate([kb[u] * eg[u], vs[hd][rows_of(sb)] * betas[hd][rows_of(sb)]], axis=1).astype(BF16)
           for u, (hd, sb) in enumerate(units)]
    wu = [dot(r[u].astype(BF16), rhs[u]).astype(BF16) for u in range(len(units))]
    qk = [dot_nt(q_u[u].astype(BF16), kbf[u]) for u in range(len(units))]
    iwu = [dot((qk[u] * decay[u]).astype(BF16), wu[u]) for u in range(len(units))]
    qe = [(q_u[u] * eg[u] - iwu[u][:, :dk]).astype(BF16) for u in range(len(units))]
    per_chunk = sub // c
    kwu, g_last = {}, {}
    for u, (hd, sb) in enumerate(units):
        for ci in range(per_chunk):
            crow = slice(ci * c, (ci + 1) * c)
            gl = gcb[u][(ci + 1) * c - 1:(ci + 1) * c, :]
            k_dec = (k_u[u][crow] * jnp.exp(gl - gcb[u][crow])).astype(BF16)
            g_last[hd, sb * per_chunk + ci] = jnp.exp(gl)
            kwu[hd, sb * per_chunk + ci] = lax.dot_general(k_dec, wu[u][crow], (((0,), (0,)), ((), ())),
                                                           preferred_element_type=F32)
    outs = [[] for _ in range(nh)]
    for ch in range(blk // c):
        u_sb, ci = ch // per_chunk, ch % per_chunk
        crow = slice(ci * c, (ci + 1) * c)
        states = [state_ref[hd] for hd in range(nh)]
        sbf = [x.astype(BF16) for x in states]
        for hd in range(nh):
            u = hd * n_sub + u_sb
            outs[hd].append(dot(qe[u][crow], sbf[hd]) + iwu[u][crow, dk:])
        corr = [dot(kwu[hd, ch][:, :dk].astype(BF16), sbf[hd]) for hd in range(nh)]
        for hd in range(nh):
            state_ref[hd] = states[hd] * g_last[hd, ch] + (kwu[hd, ch][:, dk:] - corr[hd])
    for hd in range(nh):
        out = jnp.concatenate(outs[hd], axis=0)
        o = out * lax.rsqrt(jnp.mean(out * out, axis=-1, keepdims=True) + EPS) * gain_ref[...]
        z = z_ref[0, :, hd * dk:(hd + 1) * dk].astype(F32)
        o_ref[0, :, hd * dk:(hd + 1) * dk] = (o * _silu(z)).astype(o_ref.dtype)


def _gdn(p, pg, a_log, dt_bias, gdn_gain, bsz, s, *, blk=1024):
    dk = GDN_HEAD_DIM
    assert EVEN_COL_GDN % GDN_W == 0 and EVEN_COL_Z % GDN_W == 0
    lane_pad = (0, LANES - GDN_HEADS)
    alog = jnp.pad(a_log.astype(F32), lane_pad).reshape(1, LANES)
    dt = jnp.pad(dt_bias.astype(F32), lane_pad).reshape(1, LANES)
    return pl.pallas_call(
        functools.partial(_gdn_kernel, blk=blk),
        grid=(bsz, s // blk),
        in_specs=[pl.BlockSpec((1, blk, GDN_W), lambda b, i: (b, i, EVEN_COL_GDN // GDN_W)),
                  pl.BlockSpec((1, blk, GDN_W), lambda b, i: (b, i, EVEN_COL_GDN // GDN_W + 1)),
                  pl.BlockSpec((1, blk, GDN_W), lambda b, i: (b, i, EVEN_COL_GDN // GDN_W + 2)),
                  pl.BlockSpec((1, blk, GDN_W), lambda b, i: (b, i, EVEN_COL_Z // GDN_W)),
                  pl.BlockSpec((1, blk, LANES), lambda b, i: (b, i, 0)),
                  pl.BlockSpec((1, LANES), lambda b, i: (0, 0)),
                  pl.BlockSpec((1, LANES), lambda b, i: (0, 0)),
                  pl.BlockSpec((1, dk), lambda b, i: (0, 0))],
        out_specs=pl.BlockSpec((1, blk, GDN_W), lambda b, i: (b, i, 0)),
        out_shape=jax.ShapeDtypeStruct((bsz, s, GDN_W), BF16),
        scratch_shapes=[pltpu.VMEM((GDN_HEADS, dk, dk), F32)],
        compiler_params=_cparams("parallel", "arbitrary"), name="gdn",
    )(p, p, p, p, pg, alog, dt, gdn_gain.astype(F32).reshape(1, dk))


MOE_BM = 512
ROUTER_ROWS = LANES
EXPERT_ROW0 = 8


SLAB = 8
DMA_UNROLL = 4


def _store_slabs(ref, val, rows):
    for sl in range(SLAB):
        ref[pl.ds(sl, rows, stride=SLAB), :] = val[:, sl * LANES:(sl + 1) * LANES]


def _load_slabs(ref, rows):
    return jnp.concatenate([ref[pl.ds(sl, rows, stride=SLAB), :] for sl in range(SLAB)], axis=1)


def _slab(ref, row):
    return ref.at[pl.ds(pl.multiple_of(row * SLAB, SLAB), SLAB), :]


def _ffn_norm(x, gain):
    return x * lax.rsqrt(jnp.mean(x * x, axis=-1, keepdims=True) + EPS) * gain


def _route_tile(x, g_ref, wr_ref, wrlo_ref, br_ref, eid_ref, gate_ref):
    h = _ffn_norm(x, g_ref[...])
    tm = h.shape[0]
    h_hi = h.astype(BF16)
    h_lo = (h - h_hi.astype(F32)).astype(BF16)
    nt = (((1,), (1,)), ((), ()))
    logits = (lax.dot_general(wr_ref[...], h_hi, nt, preferred_element_type=F32)
              + lax.dot_general(wr_ref[...], h_lo, nt, preferred_element_type=F32)
              + lax.dot_general(wrlo_ref[...], h_hi, nt, preferred_element_type=F32)) + br_ref[...]
    row = lax.broadcasted_iota(jnp.int32, (8, tm), 0)
    gl = jnp.where(row < N_GROUPS, logits[0:8, :], NEG_INF)
    gmax = jnp.max(gl, axis=0, keepdims=True)
    grp_p = 1.0 / jnp.sum(jnp.exp(gl - gmax), axis=0, keepdims=True)
    grp_idx = jnp.min(jnp.where(gl == gmax, row, 8), axis=0, keepdims=True)
    el = jnp.zeros((8, tm), F32)
    for g in range(N_GROUPS):
        lo = EXPERT_ROW0 + g * EXPERTS_PER_GROUP
        el = jnp.where(grp_idx == g, logits[lo:lo + EXPERTS_PER_GROUP, :], el)
    m1 = jnp.max(el, axis=0, keepdims=True)
    i1 = jnp.min(jnp.where(el == m1, row, 8), axis=0, keepdims=True)
    el2 = jnp.where(row == i1, NEG_INF, el)
    m2 = jnp.max(el2, axis=0, keepdims=True)
    i2 = jnp.min(jnp.where(el2 == m2, row, 8), axis=0, keepdims=True)
    z = jnp.sum(jnp.exp(el - m1), axis=0, keepdims=True)
    p1 = 1.0 / z
    p2 = jnp.exp(m2 - m1) / z
    scale = grp_p / (p1 + p2)
    e1 = grp_idx * EXPERTS_PER_GROUP + i1
    e2 = grp_idx * EXPERTS_PER_GROUP + i2
    eid_ref[...] = jnp.where(row == 0, e1, jnp.where(row == 1, e2, 0))
    gate_ref[...] = jnp.where(row == 0, p1 * scale, jnp.where(row == 1, p2 * scale, 0.0))


def _router_operands(gain, w_rg, b_rg, w_re, b_re):
    d = w_rg.shape[0]
    wr = jnp.zeros((ROUTER_ROWS, d), F32)
    wr = wr.at[0:N_GROUPS].set(w_rg.astype(F32).T).at[EXPERT_ROW0:EXPERT_ROW0 + N_EXPERTS].set(w_re.astype(F32).T)
    br = jnp.zeros((ROUTER_ROWS, 1), F32)
    br = br.at[0:N_GROUPS, 0].set(b_rg.astype(F32)).at[EXPERT_ROW0:EXPERT_ROW0 + N_EXPERTS, 0].set(b_re.astype(F32))
    wr_hi = wr.astype(BF16)
    wr_lo = (wr - wr_hi.astype(F32)).astype(BF16)
    return gain.astype(F32).reshape(1, d), wr_hi, wr_lo, br


def _rank_kernel(eid_ref, dest_ref, meta_ref, cnt_ref, off_ref, *, bm):
    ph = pl.program_id(0)
    i = pl.program_id(1)
    tm = eid_ref.shape[1]
    erow = lax.broadcasted_iota(jnp.int32, (N_EXPERTS, tm), 0)
    oh0 = jnp.where(erow == eid_ref[0:1, :], 1.0, 0.0).astype(F32)
    oh1 = jnp.where(erow == eid_ref[1:2, :], 1.0, 0.0).astype(F32)
    oh = oh0 + oh1
    tile_cnt = jnp.sum(oh, axis=1, keepdims=True)

    @pl.when((ph == 0) & (i == 0))
    def _():
        cnt_ref[...] = jnp.zeros_like(cnt_ref)

    @pl.when(ph == 0)
    def _():
        cnt_ref[...] += tile_cnt

    @pl.when((ph == 1) & (i == 0))
    def _():
        cnt = cnt_ref[...]
        padded = jnp.ceil(cnt * (1.0 / bm)) * bm
        er = lax.broadcasted_iota(jnp.int32, (N_EXPERTS, N_EXPERTS), 0)
        ec = lax.broadcasted_iota(jnp.int32, (N_EXPERTS, N_EXPERTS), 1)
        lower = jnp.where(ec < er, 1.0, 0.0).astype(F32)
        start = jnp.dot(lower, padded, preferred_element_type=F32, precision=lax.Precision.HIGHEST)
        off_ref[...] = start
        end = start + padded
        nb = meta_ref.shape[1]
        blk_row = lax.broadcasted_iota(jnp.int32, (N_EXPERTS, nb), 1).astype(F32) * bm
        endb = jnp.concatenate([end] * (nb // LANES), axis=1)
        be = jnp.sum(jnp.where(endb <= blk_row, 1.0, 0.0), axis=0, keepdims=True)
        be = jnp.minimum(be, N_EXPERTS - 1.0)
        used = jnp.concatenate([end[N_EXPERTS - 1:N_EXPERTS, :]] * (nb // LANES), axis=1) * (1.0 / bm)
        elane = lax.broadcasted_iota(jnp.int32, (N_EXPERTS, nb), 1)
        erow2 = lax.broadcasted_iota(jnp.int32, (N_EXPERTS, nb), 0)
        ends = jnp.sum(jnp.where(elane == erow2, endb, 0.0), axis=0, keepdims=True)
        mrow = lax.broadcasted_iota(jnp.int32, (8, nb), 0)
        meta_ref[...] = jnp.where(mrow == 0, be, jnp.where(mrow == 1, used, jnp.where(mrow == 2, ends, 0.0))
                                  ).astype(jnp.int32)
        cnt_ref[...] = jnp.zeros_like(cnt_ref)

    @pl.when(ph == 1)
    def _():
        kr = lax.broadcasted_iota(jnp.int32, (tm, tm), 0)
        kc = lax.broadcasted_iota(jnp.int32, (tm, tm), 1)
        su = jnp.where(kr < kc, 1.0, 0.0).astype(BF16)
        csum = jnp.dot(oh.astype(BF16), su, preferred_element_type=F32)
        pos = csum + (off_ref[...] + cnt_ref[...])[:, 0:1]
        d0 = jnp.sum(oh0 * pos, axis=0, keepdims=True)
        d1 = jnp.sum(oh1 * pos, axis=0, keepdims=True)
        row = lax.broadcasted_iota(jnp.int32, (8, tm), 0)
        dest_ref[...] = jnp.where(row == 0, d0, jnp.where(row == 1, d1, 0.0)).astype(jnp.int32)
        cnt_ref[...] += tile_cnt


def _moe_rank(eid, n_blocks, *, bm, tm=1024):
    t = eid.shape[1]
    nb = -(-n_blocks // LANES) * LANES
    return pl.pallas_call(
        functools.partial(_rank_kernel, bm=bm), grid=(2, t // tm),
        in_specs=[pl.BlockSpec((8, tm), lambda p, i: (0, i))],
        out_specs=[pl.BlockSpec((8, tm), lambda p, i: (0, i * p)),
                   pl.BlockSpec((8, nb), lambda p, i: (0, 0))],
        out_shape=[jax.ShapeDtypeStruct((8, t), jnp.int32), jax.ShapeDtypeStruct((8, nb), jnp.int32)],
        scratch_shapes=[pltpu.VMEM((N_EXPERTS, LANES), F32), pltpu.VMEM((N_EXPERTS, LANES), F32)],
        compiler_params=_cparams("arbitrary", "arbitrary"), name="moe_rank")(eid)


def _dispatch_kernel(dest_ref, ends_ref, x_ref, g_ref, rows_hbm, zero_ref, hbuf_ref, sem, rsem, *, tm, bm):
    @pl.when(pl.program_id(0) == 0)
    def _():
        zero_ref[...] = jnp.zeros_like(zero_ref)

        def tail(e):
            start = ends_ref[0, e] - bm
            return pltpu.make_async_copy(zero_ref, rows_hbm.at[pl.ds(pl.multiple_of(start * SLAB, SLAB), bm * SLAB), :],
                                         sem)

        def nonempty(e):
            return ends_ref[0, e] > jnp.where(e > 0, ends_ref[0, jnp.maximum(e - 1, 0)], 0)

        def fill(e, carry):
            @pl.when(nonempty(e))
            def _():
                tail(e).start()
            return carry

        def drain(e, carry):
            @pl.when(nonempty(e))
            def _():
                tail(e).wait()
            return carry

        lax.fori_loop(0, N_EXPERTS, fill, 0)
        lax.fori_loop(0, N_EXPERTS, drain, 0)

        def unused(b):
            return pltpu.make_async_copy(
                zero_ref, rows_hbm.at[pl.ds(pl.multiple_of(b * (bm * SLAB), SLAB), bm * SLAB), :], sem)

        def fill_unused(b, carry):
            unused(b).start()
            return carry

        def drain_unused(b, carry):
            unused(b).wait()
            return carry

        first_unused = ends_ref[0, N_EXPERTS - 1] // bm
        n_blocks = rows_hbm.shape[0] // (bm * SLAB)
        lax.fori_loop(first_unused, n_blocks, fill_unused, 0)
        lax.fori_loop(first_unused, n_blocks, drain_unused, 0)

    i = pl.program_id(0)
    slot = lax.rem(i, 2)

    def drain_slot(s):
        for _ in range(TOP_K):
            pltpu.make_async_copy(hbuf_ref.at[s], rows_hbm.at[pl.ds(0, tm * SLAB), :], rsem.at[s]).wait()

    for s in range(2):
        @pl.when((slot == s) & (i >= 2))
        def _():
            drain_slot(s)

        @pl.when(slot == s)
        def _():
            _store_slabs(hbuf_ref.at[s], _ffn_norm(x_ref[...], g_ref[...]), tm)

            def issue(g, carry):
                for u in range(DMA_UNROLL):
                    tok = g * DMA_UNROLL + u
                    for k in range(TOP_K):
                        pltpu.make_async_copy(_slab(hbuf_ref.at[s], tok),
                                              _slab(rows_hbm, dest_ref[0, 0, k * tm + tok]),
                                              rsem.at[s]).start(priority=k)
                return carry

            lax.fori_loop(0, tm // DMA_UNROLL, issue, 0)

    last = pl.num_programs(0) - 1

    @pl.when(i == last)
    def _():
        for s in range(2):
            @pl.when((slot == s) | (last >= 1))
            def _():
                drain_slot(s)


def _moe_dispatch(dest_tiles, expert_ends, x2, gain, n_rows, *, tm, bm):
    t, d = x2.shape
    return pl.pallas_call(
        functools.partial(_dispatch_kernel, tm=tm, bm=bm), grid=(t // tm,),
        in_specs=[pl.BlockSpec((1, 1, TOP_K * tm), lambda i: (i, 0, 0), memory_space=pltpu.SMEM),
                  pl.BlockSpec(memory_space=pltpu.SMEM),
                  pl.BlockSpec((tm, d), lambda i: (i, 0)),
                  pl.BlockSpec((1, d), lambda i: (0, 0))],
        out_specs=pl.BlockSpec(memory_space=pl.ANY),
        out_shape=jax.ShapeDtypeStruct((n_rows * SLAB, LANES), F32),
        scratch_shapes=[pltpu.VMEM((bm * SLAB, LANES), F32), pltpu.VMEM((2, tm * SLAB, LANES), F32),
                        pltpu.SemaphoreType.DMA(()), pltpu.SemaphoreType.DMA((2,))],
        compiler_params=pltpu.CompilerParams(dimension_semantics=("arbitrary",), has_side_effects=True,
                                             vmem_limit_bytes=VMEM_LIMIT),
        name="moe_dispatch")(dest_tiles, expert_ends, x2, gain.reshape(1, d))


COMBINE_TM = 512
EXPERT_RING = 3


def _experts_kernel(be_ref, used_ref, x_hbm, wg_ref, wu_ref, wd_ref, y_ref, wgb_ref, wub_ref, wdb_ref, xbuf_ref,
                    xsem):
    b = pl.program_id(0)
    used = used_ref[0]
    bm = y_ref.shape[0] // SLAB

    def fetch(blk):
        slot = lax.rem(blk, EXPERT_RING)
        rows = pl.ds(pl.multiple_of(blk * (bm * SLAB), SLAB), bm * SLAB)
        return pltpu.make_async_copy(x_hbm.at[rows, :], xbuf_ref.at[slot], xsem.at[slot])

    @pl.when(b == 0)
    def _():
        for first in range(EXPERT_RING - 1):
            @pl.when(first < used)
            def _():
                fetch(first).start()

    @pl.when(b + (EXPERT_RING - 1) < used)
    def _():
        fetch(b + (EXPERT_RING - 1)).start()

    prev = be_ref[jnp.maximum(b - 1, 0)]

    @pl.when((b == 0) | (be_ref[b] != prev))
    def _():
        wgb_ref[...] = wg_ref[0, 0].astype(BF16)
        wub_ref[...] = wu_ref[0, 0].astype(BF16)
        wdb_ref[...] = wd_ref[0, 0].astype(BF16)

    @pl.when(b < used)
    def _():
        fetch(b).wait()
        x = _load_slabs(xbuf_ref.at[lax.rem(b, EXPERT_RING)], bm).astype(BF16)
        gate = jnp.dot(x, wgb_ref[...], preferred_element_type=F32)
        up = jnp.dot(x, wub_ref[...], preferred_element_type=F32)
        hid = (_silu(gate) * up).astype(BF16)
        _store_slabs(y_ref, jnp.dot(hid, wdb_ref[...], preferred_element_type=F32), bm)

    @pl.when(b >= used_ref[0])
    def _():
        y_ref[...] = jnp.zeros_like(y_ref)


def _moe_experts(block_expert, n_used, x_rows, w_gate, w_up, w_down, layer, *, bm):
    n_rows = x_rows.shape[0] // SLAB
    d, de = w_gate.shape[2], w_gate.shape[3]
    grid_spec = pltpu.PrefetchScalarGridSpec(
        num_scalar_prefetch=2, grid=(n_rows // bm,),
        in_specs=[pl.BlockSpec(memory_space=pl.ANY),
                  pl.BlockSpec((1, 1, d, de), lambda b, be, nu: (layer, be[b], 0, 0)),
                  pl.BlockSpec((1, 1, d, de), lambda b, be, nu: (layer, be[b], 0, 0)),
                  pl.BlockSpec((1, 1, de, d), lambda b, be, nu: (layer, be[b], 0, 0))],
        out_specs=pl.BlockSpec((bm * SLAB, LANES), lambda b, be, nu: (b, 0)),
        scratch_shapes=[pltpu.VMEM((d, de), BF16), pltpu.VMEM((d, de), BF16), pltpu.VMEM((de, d), BF16),
                        pltpu.VMEM((EXPERT_RING, bm * SLAB, LANES), F32), pltpu.SemaphoreType.DMA((EXPERT_RING,))])
    return pl.pallas_call(
        _experts_kernel, grid_spec=grid_spec, out_shape=jax.ShapeDtypeStruct((n_rows * SLAB, LANES), F32),
        compiler_params=_cparams("arbitrary"), name="moe_experts",
    )(block_expert, n_used, x_rows, w_gate, w_up, w_down)


def _combine_kernel(dest_ref, next_ref, x_ref, gate_ref, y_hbm, o_ref, buf_ref, sem, *, tm):
    _combine_tile(dest_ref, next_ref, x_ref, gate_ref, y_hbm, o_ref, buf_ref, sem, tm)


def _combine_tile(dest_ref, next_ref, x_ref, gate_ref, y_hbm, o_ref, buf_ref, sem, tm):
    i = pl.program_id(0)
    slot = lax.rem(i, 2)

    def gather(idx_ref, s):
        def issue(g, carry):
            for u in range(DMA_UNROLL):
                tok = g * DMA_UNROLL + u
                for k in range(TOP_K):
                    pltpu.make_async_copy(_slab(y_hbm, idx_ref[0, 0, k * tm + tok]), _slab(buf_ref.at[s, k], tok),
                                          sem.at[s]).start(priority=k)
            return carry

        lax.fori_loop(0, tm // DMA_UNROLL, issue, 0)

    for s in range(2):
        @pl.when((i == 0) & (slot == s))
        def _():
            gather(dest_ref, s)

        @pl.when((i + 1 < pl.num_programs(0)) & (slot == 1 - s))
        def _():
            gather(next_ref, s)

    er = lax.broadcasted_iota(jnp.int32, (8, LANES), 0)
    ec = lax.broadcasted_iota(jnp.int32, (8, LANES), 1)
    eye = jnp.where(er == ec, 1.0, 0.0).astype(F32)
    gcol = lax.dot_general(gate_ref[...], eye, (((0,), (0,)), ((), ())), preferred_element_type=F32,
                           precision=lax.Precision.HIGHEST)
    for s in range(2):
        @pl.when(slot == s)
        def _():
            for k in range(TOP_K):
                pltpu.make_async_copy(y_hbm.at[pl.ds(0, tm * SLAB), :], buf_ref.at[s, k], sem.at[s]).wait()
            o_ref[...] = (x_ref[...] + gcol[:, 0:1] * _load_slabs(buf_ref.at[s, 0], tm)
                          + gcol[:, 1:2] * _load_slabs(buf_ref.at[s, 1], tm))


def _moe_combine(dest_tiles, x2, gates, y_rows, *, tm):
    t, d = x2.shape
    n = t // tm
    dest_spec = functools.partial(pl.BlockSpec, (1, 1, TOP_K * tm), memory_space=pltpu.SMEM)
    return pl.pallas_call(
        functools.partial(_combine_kernel, tm=tm), grid=(n,),
        in_specs=[dest_spec(lambda i: (i, 0, 0)),
                  dest_spec(lambda i: (jnp.minimum(i + 1, n - 1), 0, 0)),
                  pl.BlockSpec((tm, d), lambda i: (i, 0)),
                  pl.BlockSpec((8, tm), lambda i: (0, i)),
                  pl.BlockSpec(memory_space=pl.ANY)],
        out_specs=pl.BlockSpec((tm, d), lambda i: (i, 0)),
        out_shape=jax.ShapeDtypeStruct((t, d), F32),
        scratch_shapes=[pltpu.VMEM((2, TOP_K, tm * SLAB, LANES), F32), pltpu.SemaphoreType.DMA((2,))],
        compiler_params=_cparams("arbitrary"), name="moe_combine")(dest_tiles, dest_tiles, x2, gates, y_rows)


def _dest_tiles(dest, tm):
    t = dest.shape[1]
    return dest[0:TOP_K].reshape(TOP_K, t // tm, tm).transpose(1, 0, 2).reshape(t // tm, 1, TOP_K * tm)


def _moe_expert_rows(x2, eid, gain, w_gate, w_up, w_down, layer, *, tm=1024):
    t, d = x2.shape
    bm = MOE_BM
    n_rows = t * TOP_K + N_EXPERTS * bm
    n_blocks = n_rows // bm
    dest, meta = _moe_rank(eid, n_blocks, bm=bm)
    x_rows = _moe_dispatch(_dest_tiles(dest, tm), meta[2:3, :N_EXPERTS], x2, gain, n_rows, tm=tm, bm=bm)
    y_rows = _moe_experts(meta[0, :n_blocks], meta[1, 0:1], x_rows, w_gate, w_up, w_down, layer, bm=bm)
    return dest, y_rows


def _even_odd_perm():
    half = RET_QK_DIM // 2
    src = np.concatenate([2 * np.arange(half), 2 * np.arange(half) + 1])
    perm = np.zeros((RET_QK_DIM, RET_QK_DIM), np.float32)
    perm[src, np.arange(RET_QK_DIM)] = 1.0
    return jnp.asarray(perm, dtype=BF16)


def kernel(x, ln_mix_even, w_in_even, conv_w_even, a_log_even, dt_bias_even, q_gain_even, k_gain_even, sinks_even, gdn_gain_even, w_out_even, ln_mix_odd, w_in_odd, w_out_odd, ln_ffn, w_router_group, b_router_group, w_router_expert, b_router_expert, w_gate, w_up, w_down):
    bsz, s, d = x.shape
    t = bsz * s
    x2 = x.reshape(t, d).astype(F32)
    depth = ln_ffn.shape[0]
    pending = None

    def settle(x2, pending):
        dest, gates, y_rows = pending
        return _moe_combine(_dest_tiles(dest, COMBINE_TM), x2, gates, y_rows, tm=COMBINE_TM)

    for layer in range(depth):
        i = layer // 2
        router = _router_operands(ln_ffn[layer], w_router_group[layer], b_router_group[layer],
                                  w_router_expert[layer], b_router_expert[layer])
        if pending is not None:
            x2, pending = settle(x2, pending), None
        if layer % 2 == 0:
            w_in = w_in_even[i]
            swa_end = SWA_Q_W + 2 * SWA_KV_W
            w_main = jnp.concatenate([w_in[:, :SWA_Q_W], w_in[:, swa_end:EVEN_MAIN], w_in[:, SWA_Q_W:swa_end]],
                                     axis=1).astype(BF16)
            w_gates = jnp.pad(w_in[:, EVEN_MAIN:].astype(F32), ((0, 0), (0, LANES - 2 * GDN_HEADS)))
            p, pg = _norm_proj(x2, ln_mix_even[i], w_main, w_gates, conv_w=conv_w_even[i],
                               conv_cols=(EVEN_COL_GDN, EVEN_COL_Z), seq_len=s, tm=1024)
            p = p.reshape(bsz, s, EVEN_MAIN)
            y_a = _swa(p, sinks_even[i], q_gain_even[i], k_gain_even[i], bsz, s)
            y_b = _gdn(p, pg.reshape(bsz, s, LANES), a_log_even[i], dt_bias_even[i], gdn_gain_even[i], bsz, s)
            w_out = w_out_even[i].astype(BF16)
            x2, eid, gates = _out_proj(x2, [y_a.reshape(t, SWA_Q_W), y_b.reshape(t, GDN_W)],
                                       [w_out[:SWA_Q_W], w_out[SWA_Q_W:]], router)
        else:
            p = _norm_proj(x2, ln_mix_odd[i], w_in_odd[i].astype(BF16), perm=_even_odd_perm(), n_perm=2 * RET_HEADS)
            y = _retention(p.reshape(bsz, s, ODD_IN), bsz, s)
            x2, eid, gates = _out_proj(x2, [y.reshape(t, RET_V_W)], [w_out_odd[i].astype(BF16)], router)
        dest, y_rows = _moe_expert_rows(x2, eid, ln_ffn[layer], w_gate, w_up, w_down, layer)
        pending = (dest, gates, y_rows)
    x2 = settle(x2, pending)
    return x2.reshape(bsz, s, d).astype(x.dtype)
```
